```python
import jax, jax.numpy as jnp
from jax import lax
import numpy as np

D_MODEL = 1024
BATCH = 16
SEQ = 2048
DEPTH = 1
DEC_BATCH = 32
DEC_SEQ = 1
PAST_LEN = 16384
PAGE_SIZE = 128

HEAD_DIM = 64
ATT_HEADS = 8
ATT_W = ATT_HEADS * HEAD_DIM
MEM_HEADS = 4
MEM_W = MEM_HEADS * HEAD_DIM
CONV_W = D_MODEL // 4
MIX_W = ATT_W + CONV_W + MEM_W
PROJ_W = 3 * ATT_W + MEM_W + 3 * CONV_W
N_MEM = 256
CONV_K = 3
MOBA_BLOCK = 256
MOBA_TOPK = 3
Q_BLOCK = 128
N_EXPERTS = 32
TOP_K = 4
D_FF = D_MODEL
SWIGLU_LIMIT = 7.0
SWIGLU_ALPHA = 1.702
EXPERT_BLOCK = 128
LN_EPS = 1e-5
DN_ALPHA = (2 * DEPTH) ** 0.25
DN_BETA = (8 * DEPTH) ** -0.25

kernel_name = 'moba_shortconv_memxattn_moe_step'


def _layernorm(x, g, b):
    xf = x.astype(jnp.float32)
    xc = xf - jnp.mean(xf, -1, keepdims=True)
    var = jnp.mean(xc * xc, -1, keepdims=True)
    return (xc * lax.rsqrt(var + LN_EPS) * g.astype(jnp.float32) + b.astype(jnp.float32)).astype(x.dtype)


def _rms(x):
    xf = x.astype(jnp.float32)
    return (xf * lax.rsqrt(jnp.mean(xf * xf, -1, keepdims=True) + LN_EPS)).astype(x.dtype)


def _alibi_slopes():
    return 2.0 ** (-8.0 * jnp.arange(1, ATT_HEADS + 1, dtype=jnp.float32) / ATT_HEADS)


def _pad_to_blocks(parts):
    length = sum(p.shape[1] for p in parts)
    pad = (-length) % MOBA_BLOCK
    parts = list(parts)
    if pad:
        p0 = parts[0]
        parts.append(jnp.zeros((p0.shape[0], pad) + p0.shape[2:], p0.dtype))
    return jnp.concatenate(parts, axis=1) if len(parts) > 1 else parts[0]


def _moba_attention(q, k, v, q_start, slopes):
    B, Sq, H, hd = q.shape
    NB = k.shape[1] // MOBA_BLOCK
    kblk = k.reshape(B, NB, MOBA_BLOCK, H, hd)
    vblk = v.reshape(B, NB, MOBA_BLOCK, H, hd)
    kmean = jnp.mean(kblk.astype(jnp.float32), axis=2)
    qc_len = Q_BLOCK if Sq % Q_BLOCK == 0 else Sq
    n_chunks = Sq // qc_len
    n_sel = min(MOBA_TOPK, NB)
    scale = hd ** -0.5
    head_idx = jnp.arange(H)[:, None, None]
    slot_rank = jnp.arange(n_sel)
    offs = jnp.arange(MOBA_BLOCK)

    def chunk(args):
        qc, b, c = args
        pos = q_start + c * qc_len + jnp.arange(qc_len)
        own = pos // MOBA_BLOCK
        gate = jnp.einsum('qhd,nhd->hqn', qc.astype(jnp.float32), kmean[b])
        gate = jnp.where(jnp.arange(NB)[None, None, :] < own[None, :, None], gate, -jnp.inf)
        _, sel = lax.top_k(gate, n_sel)
        own_b = jnp.broadcast_to(own[None, :, None], (H, qc_len, 1)).astype(sel.dtype)
        blocks = jnp.concatenate([sel, own_b], axis=-1)
        ksel = kblk[b][blocks, :, head_idx, :]
        vsel = vblk[b][blocks, :, head_idx, :]
        key_pos = blocks[..., None] * MOBA_BLOCK + offs
        slot_ok = jnp.concatenate([slot_rank[None, :] < own[:, None], jnp.ones((qc_len, 1), bool)], axis=-1)
        valid = slot_ok[None, :, :, None] & (key_pos <= pos[None, :, None, None])
        dist = (pos[None, :, None, None] - key_pos).astype(jnp.float32)
        logits = jnp.einsum('qhd,hqnkd->hqnk', qc, ksel, preferred_element_type=jnp.float32) * scale - slopes[:, None, None, None] * dist
        probs = jax.nn.softmax(jnp.where(valid, logits, -jnp.inf), axis=(-2, -1))
        return jnp.einsum('hqnk,hqnkd->qhd', probs.astype(vsel.dtype), vsel)

    idx = jnp.arange(B * n_chunks)
    out = lax.map(chunk, (q.reshape(B * n_chunks, qc_len, H, hd), idx // n_chunks, idx % n_chunks))
    return out.reshape(B, Sq, H, hd)


def _mem_attention(q, mem_k, mem_v):
    logits = jnp.einsum('bthd,bmhd->bhtm', q, mem_k, preferred_element_type=jnp.float32) * (HEAD_DIM ** -0.5)
    probs = jax.nn.softmax(logits, axis=-1)
    return jnp.einsum('bhtm,bmhd->bthd', probs.astype(mem_v.dtype), mem_v)


def _short_conv(u, prev, w):
    T = u.shape[1]
    up = jnp.concatenate([prev.astype(u.dtype), u], axis=1)
    y = sum(w[i] * up[:, i:i + T] for i in range(CONV_K))
    return y, up[:, T:]


def _project(x, w_in):
    B, T, _ = x.shape
    p = x @ w_in
    cuts = [ATT_W, 2 * ATT_W, 3 * ATT_W, 3 * ATT_W + MEM_W, 3 * ATT_W + MEM_W + CONV_W, 3 * ATT_W + MEM_W + 2 * CONV_W]
    q, k, v, qm, gb, gc, h = jnp.split(p, cuts, axis=-1)
    heads = lambda t, n: t.reshape(B, T, n, HEAD_DIM)
    return heads(q, ATT_HEADS), heads(k, ATT_HEADS), heads(v, ATT_HEADS), heads(qm, MEM_HEADS), gb, gc, h


def _moe(x, w_router, b_router, w_gate_up, b_gate_up, w_down, b_down):
    D = x.shape[-1]
    xt = x.reshape(-1, D)
    N = xt.shape[0]
    M = N * TOP_K
    logits = (xt @ w_router).astype(jnp.float32) + b_router.astype(jnp.float32)
    top_v, top_e = lax.top_k(logits, TOP_K)
    gate_w = jax.nn.softmax(top_v, axis=-1)
    flat_e = top_e.reshape(M)
    order = jnp.argsort(flat_e)
    e_s = flat_e[order]
    tok_s = order // TOP_K
    w_s = gate_w.reshape(M)[order]
    counts = jnp.bincount(flat_e, length=N_EXPERTS)
    padded = (counts + EXPERT_BLOCK - 1) // EXPERT_BLOCK * EXPERT_BLOCK
    pad_end = jnp.cumsum(padded)
    pad_start = pad_end - padded
    start = jnp.cumsum(counts) - counts
    dest = pad_start[e_s] + jnp.arange(M) - start[e_s]
    n_blocks = (M + N_EXPERTS * (EXPERT_BLOCK - 1) + EXPERT_BLOCK - 1) // EXPERT_BLOCK
    buf = jnp.zeros((n_blocks * EXPERT_BLOCK, D), xt.dtype).at[dest].set(xt[tok_s])
    block_e = jnp.minimum(jnp.searchsorted(pad_end, jnp.arange(n_blocks) * EXPERT_BLOCK, side='right'), N_EXPERTS - 1)

    def expert_block(args):
        xb, e = args
        gu = xb @ w_gate_up[e] + b_gate_up[e]
        g, u = jnp.split(gu, 2, axis=-1)
        g = jnp.minimum(g, SWIGLU_LIMIT)
        u = jnp.clip(u, -SWIGLU_LIMIT, SWIGLU_LIMIT)
        hid = (u + 1.0) * g * jax.nn.sigmoid(SWIGLU_ALPHA * g)
        return hid @ w_down[e] + b_down[e]

    out = lax.map(expert_block, (buf.reshape(n_blocks, EXPERT_BLOCK, D), block_e)).reshape(n_blocks * EXPERT_BLOCK, D)
    y = jax.ops.segment_sum(out[dest] * w_s[:, None].astype(out.dtype), tok_s, num_segments=N)
    return y.reshape(x.shape)


def _finish(x, o_att, o_conv, o_mem, g_mix, w_out, ln1_g, ln1_b, w_router, b_router, w_gate_up, b_gate_up, w_down, b_down, ln2_g, ln2_b):
    B, T, _ = x.shape
    mix = jnp.concatenate([_rms(o_att.reshape(B, T, ATT_W)), _rms(o_conv), _rms(o_mem.reshape(B, T, MEM_W))], axis=-1) * g_mix
    x1 = _layernorm(DN_ALPHA * x + mix @ w_out, ln1_g, ln1_b)
    return _layernorm(DN_ALPHA * x1 + _moe(x1, w_router, b_router, w_gate_up, b_gate_up, w_down, b_down), ln2_g, ln2_b)


def setup_inputs(seed: int = 0) -> dict:
    key = jax.random.key(seed)
    ks = jax.random.split(key, 24)
    n_pages = PAST_LEN // PAGE_SIZE
    n_used = DEC_BATCH * n_pages
    n_phys = n_used + max(1, n_used // 4)
    f32 = jnp.float32
    nrm = lambda k, shape, s: jax.random.normal(k, shape, f32) * s
    x_prompt = nrm(ks[0], (BATCH, SEQ, D_MODEL), 1.0)
    x_sample = nrm(ks[1], (DEC_BATCH, DEC_SEQ, D_MODEL), 1.0)
    cache_k = nrm(ks[2], (DEPTH, n_phys, PAGE_SIZE, ATT_HEADS, HEAD_DIM), 1.0)
    cache_v = nrm(ks[3], (DEPTH, n_phys, PAGE_SIZE, ATT_HEADS, HEAD_DIM), DN_BETA)
    cache_mem_k = nrm(ks[4], (DEPTH, DEC_BATCH, N_MEM, MEM_HEADS, HEAD_DIM), 1.0)
    cache_mem_v = nrm(ks[5], (DEPTH, DEC_BATCH, N_MEM, MEM_HEADS, HEAD_DIM), 1.0)
    state_conv = nrm(ks[6], (DEPTH, DEC_BATCH, CONV_K - 1, CONV_W), 1.0)
    page_table = jax.random.permutation(ks[7], n_phys)[:n_used].reshape(DEC_BATCH, n_pages).astype(jnp.int32)
    mem_prompt = nrm(ks[8], (BATCH, N_MEM, D_MODEL), 1.0)
    w_in = nrm(ks[9], (DEPTH, D_MODEL, PROJ_W), D_MODEL ** -0.5)
    w_in = w_in.at[:, :, 2 * ATT_W:3 * ATT_W].multiply(DN_BETA)
    w_mem_kv = nrm(ks[10], (DEPTH, D_MODEL, 2 * MEM_W), D_MODEL ** -0.5)
    w_conv = nrm(ks[11], (DEPTH, CONV_K, CONV_W), CONV_K ** -0.5)
    g_mix = 1.0 + nrm(ks[12], (DEPTH, MIX_W), 0.02)
    w_out = nrm(ks[13], (DEPTH, MIX_W, D_MODEL), MIX_W ** -0.5 * DN_BETA)
    ln1_g = 1.0 + nrm(ks[14], (DEPTH, D_MODEL), 0.02)
    ln1_b = nrm(ks[15], (DEPTH, D_MODEL), 0.02)
    w_router = nrm(ks[16], (DEPTH, D_MODEL, N_EXPERTS), D_MODEL ** -0.5)
    b_router = nrm(ks[17], (DEPTH, N_EXPERTS), 0.01)
    w_gate_up = nrm(ks[18], (DEPTH, N_EXPERTS, D_MODEL, 2 * D_FF), D_MODEL ** -0.5 * DN_BETA)
    b_gate_up = nrm(ks[19], (DEPTH, N_EXPERTS, 2 * D_FF), 0.01)
    w_down = nrm(ks[20], (DEPTH, N_EXPERTS, D_FF, D_MODEL), D_FF ** -0.5 * DN_BETA)
    b_down = nrm(ks[21], (DEPTH, N_EXPERTS, D_MODEL), 0.01)
    ln2_g = 1.0 + nrm(ks[22], (DEPTH, D_MODEL), 0.02)
    ln2_b = nrm(ks[23], (DEPTH, D_MODEL), 0.02)
    return {'x_prompt': x_prompt, 'x_sample': x_sample, 'cache_k': cache_k, 'cache_v': cache_v,
            'cache_mem_k': cache_mem_k, 'cache_mem_v': cache_mem_v, 'state_conv': state_conv,
            'page_table': page_table, 'mem_prompt': mem_prompt, 'w_in': w_in, 'w_mem_kv': w_mem_kv,
            'w_conv': w_conv, 'g_mix': g_mix, 'w_out': w_out, 'ln1_g': ln1_g, 'ln1_b': ln1_b,
            'w_router': w_router, 'b_router': b_router, 'w_gate_up': w_gate_up, 'b_gate_up': b_gate_up,
            'w_down': w_down, 'b_down': b_down, 'ln2_g': ln2_g, 'ln2_b': ln2_b}


def reference(x_prompt, x_sample, cache_k, cache_v, cache_mem_k, cache_mem_v, state_conv, page_table, mem_prompt,
              w_in, w_mem_kv, w_conv, g_mix, w_out, ln1_g, ln1_b, w_router, b_router, w_gate_up, b_gate_up,
              w_down, b_down, ln2_g, ln2_b):
    slopes = _alibi_slopes()
    past_len = page_table.shape[1] * PAGE_SIZE
    hp, hs = x_prompt, x_sample
    Bp = hp.shape[0]
    Bs = hs.shape[0]
    kp_rows, vp_rows, conv_p, memk_p, memv_p, ks_rows, vs_rows, conv_s = [], [], [], [], [], [], [], []
    for l in range(DEPTH):
        ffn = (g_mix[l], w_out[l], ln1_g[l], ln1_b[l], w_router[l], b_router[l], w_gate_up[l], b_gate_up[l],
               w_down[l], b_down[l], ln2_g[l], ln2_b[l])
        q, k, v, qm, gb, gc, h = _project(hp, w_in[l])
        o_att = _moba_attention(q, _pad_to_blocks([k]), _pad_to_blocks([v]), 0, slopes)
        mk, mv = jnp.split(mem_prompt @ w_mem_kv[l], 2, axis=-1)
        mk = mk.reshape(Bp, N_MEM, MEM_HEADS, HEAD_DIM)
        mv = mv.reshape(Bp, N_MEM, MEM_HEADS, HEAD_DIM)
        o_mem = _mem_attention(qm, mk, mv)
        cz, c_state = _short_conv(gc * h, jnp.zeros((Bp, CONV_K - 1, CONV_W), hp.dtype), w_conv[l])
        hp = _finish(hp, o_att, gb * cz, o_mem, *ffn)
        kp_rows.append(k)
        vp_rows.append(v)
        conv_p.append(c_state)
        memk_p.append(mk)
        memv_p.append(mv)
        q, k, v, qm, gb, gc, h = _project(hs, w_in[l])
        rows = (Bs, past_len, ATT_HEADS, HEAD_DIM)
        k_all = _pad_to_blocks([cache_k[l, page_table].reshape(rows), k])
        v_all = _pad_to_blocks([cache_v[l, page_table].reshape(rows), v])
        o_att = _moba_attention(q, k_all, v_all, past_len, slopes)
        o_mem = _mem_attention(qm, cache_mem_k[l], cache_mem_v[l])
        cz, c_state = _short_conv(gc * h, state_conv[l], w_conv[l])
        hs = _finish(hs, o_att, gb * cz, o_mem, *ffn)
        ks_rows.append(k)
        vs_rows.append(v)
        conv_s.append(c_state)
    return (hp, hs, jnp.stack(kp_rows), jnp.stack(vp_rows), jnp.stack(conv_p), jnp.stack(memk_p), jnp.stack(memv_p),
            jnp.stack(ks_rows), jnp.stack(vs_rows), jnp.stack(conv_s))
```

```python
import functools

import jax
import jax.numpy as jnp
from jax import lax
from jax.experimental import pallas as pl
from jax.experimental.pallas import tpu as pltpu

F32 = jnp.float32
BF16 = jnp.bfloat16
I32 = jnp.int32
U32 = jnp.uint32

HEAD_DIM = 64
MOBA_BLOCK = 256
MOBA_TOPK = 3
PAGE_SIZE = 128
TOP_K = 4
CONV_K = 3
SWIGLU_LIMIT = 7.0
SWIGLU_ALPHA = 1.702
LN_EPS = 1e-5
Q_SCALE = HEAD_DIM ** -0.5
NEG = -1e30
LANE = 128
SUBLANE = 8
VMEM_LIMIT = 56 * 1024 * 1024
HI = lax.Precision.HIGHEST


def _params(*sem):
    return pltpu.CompilerParams(dimension_semantics=sem, vmem_limit_bytes=VMEM_LIMIT)


def _dot_nt(a, b, precision=None):
    return lax.dot_general(a, b, (((1,), (1,)), ((), ())), precision=precision, preferred_element_type=F32)


def _dot(a, b, precision=None):
    return jnp.dot(a, b, precision=precision, preferred_element_type=F32)


def _layernorm(z, g, b):
    zc = z - jnp.mean(z, axis=-1, keepdims=True)
    var = jnp.mean(zc * zc, axis=-1, keepdims=True)
    return zc * lax.rsqrt(var + LN_EPS) * g + b


def _rms(a):
    return a * lax.rsqrt(jnp.mean(a * a, axis=-1, keepdims=True) + LN_EPS)


def _head_mask(shape, hh):
    lane = lax.broadcasted_iota(I32, shape, len(shape) - 1)
    return (lane >= HEAD_DIM * hh) & (lane < HEAD_DIM * (hh + 1))


def _inproj_kernel(x_ref, w_ref, wc_ref, q_ref, k_ref, v_ref, km_ref, qm_ref, oc_ref, tail_ref, ubuf,
                   *, tile, tiles_per_seq, att_w, mem_w, conv_w):
    i = pl.program_id(0)
    x = x_ref[...].astype(BF16)
    c1, c2, c3 = att_w, 2 * att_w, 3 * att_w
    c4 = c3 + mem_w
    c5, c6, c7 = c4 + conv_w, c4 + 2 * conv_w, c4 + 3 * conv_w

    def proj(lo, hi):
        return _dot(x, w_ref[:, lo:hi])

    q_ref[...] = (proj(0, c1) * Q_SCALE).astype(BF16)
    k = proj(c1, c2)
    k_ref[...] = k
    km_ref[0] = jnp.sum(k.reshape(tile // MOBA_BLOCK, MOBA_BLOCK, att_w), axis=1) * (1.0 / MOBA_BLOCK)
    v_ref[...] = proj(c2, c3)
    qm_ref[...] = (proj(c3, c4) * Q_SCALE).astype(BF16)
    gb = proj(c4, c5)
    u = proj(c5, c6) * proj(c6, c7)

    @pl.when(i % tiles_per_seq == 0)
    def _():
        ubuf[0:SUBLANE, :] = jnp.zeros((SUBLANE, conv_w), F32)

    ubuf[SUBLANE:SUBLANE + tile, :] = u
    u1 = ubuf[SUBLANE - 1:SUBLANE - 1 + tile, :]
    u2 = ubuf[SUBLANE - 2:SUBLANE - 2 + tile, :]
    cz = wc_ref[0:1, :] * u2 + wc_ref[1:2, :] * u1 + wc_ref[2:3, :] * u
    oc_ref[...] = gb * cz
    tail = ubuf[tile:tile + SUBLANE, :]
    tail_ref[0] = tail
    ubuf[0:SUBLANE, :] = tail


def _inproj(x2d, w_bf, w_conv, *, seq, tile, att_w, mem_w, conv_w):
    n, d = x2d.shape
    nt = n // tile
    kern = functools.partial(_inproj_kernel, tile=tile, tiles_per_seq=seq // tile, att_w=att_w, mem_w=mem_w,
                             conv_w=conv_w)
    row = lambda w: pl.BlockSpec((tile, w), lambda i: (i, 0))
    return pl.pallas_call(
        kern,
        grid=(nt,),
        in_specs=[row(d), pl.BlockSpec(w_bf.shape, lambda i: (0, 0)), pl.BlockSpec(w_conv.shape, lambda i: (0, 0))],
        out_specs=[row(att_w), row(att_w), row(att_w),
                   pl.BlockSpec((1, tile // MOBA_BLOCK, att_w), lambda i: (i, 0, 0)),
                   row(mem_w), row(conv_w), pl.BlockSpec((1, SUBLANE, conv_w), lambda i: (i, 0, 0))],
        out_shape=[jax.ShapeDtypeStruct((n, att_w), BF16), jax.ShapeDtypeStruct((n, att_w), F32),
                   jax.ShapeDtypeStruct((n, att_w), F32),
                   jax.ShapeDtypeStruct((nt, tile // MOBA_BLOCK, att_w), F32),
                   jax.ShapeDtypeStruct((n, mem_w), BF16), jax.ShapeDtypeStruct((n, conv_w), F32),
                   jax.ShapeDtypeStruct((nt, SUBLANE, conv_w), F32)],
        scratch_shapes=[pltpu.VMEM((tile + SUBLANE, conv_w), F32)],
        compiler_params=_params("arbitrary"),
        name="inproj",
    )(x2d, w_bf, w_conv)


def _moba_kernel(slopes_ref, q_ref, k_ref, v_ref, km_ref, o_ref, kbf, vT, selb, m_s, l_s, acc_s, *, nblk):
    p = pl.program_id(1)
    i = pl.program_id(2)
    B = MOBA_BLOCK

    @pl.when(i == 0)
    def _():
        for c in range(nblk):
            kbf[c] = k_ref[c * B:(c + 1) * B, :].astype(BF16)
            vT[c] = v_ref[c * B:(c + 1) * B, :].T.astype(BF16)

    q = q_ref[...]
    key_i = lax.broadcasted_iota(I32, (B, B), 0)
    qry_i = lax.broadcasted_iota(I32, (B, B), 1)
    rel = (key_i - qry_i).astype(F32)
    causal = key_i <= qry_i
    blk = lax.broadcasted_iota(I32, (nblk, B), 0)
    sub = lax.broadcasted_iota(I32, (2 * HEAD_DIM, B), 0)
    outs = []
    for hh in range(2):
        slope = slopes_ref[2 * p + hh]
        qh = jnp.where(_head_mask(q.shape, hh), q, jnp.zeros_like(q))
        g = _dot_nt(km_ref[...], qh.astype(F32), precision=HI)
        g = jnp.where(blk < i, g, -jnp.inf)
        sb = jnp.full((nblk, B), NEG, F32)
        for _ in range(MOBA_TOPK):
            m = jnp.max(g, axis=0, keepdims=True)
            cand = (g == m) & (g > -jnp.inf)
            idx = jnp.min(jnp.where(cand, blk, nblk), axis=0, keepdims=True)
            pick = blk == idx
            sb = jnp.where(pick, 0.0, sb)
            g = jnp.where(pick, -jnp.inf, g)
        selb[...] = sb
        base = slope * rel
        s = _dot_nt(kbf[i], qh)
        s = jnp.where(causal, s + base, NEG)
        m0 = jnp.max(s, axis=0, keepdims=True)
        p0 = jnp.exp(s - m0)
        m_s[...] = m0
        l_s[...] = jnp.sum(p0, axis=0, keepdims=True)
        acc_s[...] = _dot(vT[i], p0.astype(BF16))

        def body(j, carry):
            off = slope * (B * (i - j)).astype(F32)
            s = _dot_nt(kbf[j], qh) + base + (selb[pl.ds(j, 1), :] - off)
            m_old = m_s[...]
            m_new = jnp.maximum(m_old, jnp.max(s, axis=0, keepdims=True))
            a = jnp.exp(m_old - m_new)
            pj = jnp.exp(s - m_new)
            l_s[...] = a * l_s[...] + jnp.sum(pj, axis=0, keepdims=True)
            acc_s[...] = a * acc_s[...] + _dot(vT[j], pj.astype(BF16))
            m_s[...] = m_new
            return carry

        lax.fori_loop(0, i, body, 0)
        outs.append(acc_s[...] / l_s[...])
    o_ref[...] = jnp.where(sub < HEAD_DIM, outs[0], outs[1]).T


def _moba_prompt(q, k, v, kmean, slopes, *, batch, seq):
    n, att_w = q.shape
    nblk = seq // MOBA_BLOCK
    B = MOBA_BLOCK
    kern = functools.partial(_moba_kernel, nblk=nblk)
    return pl.pallas_call(
        kern,
        grid_spec=pltpu.PrefetchScalarGridSpec(
            num_scalar_prefetch=1,
            grid=(batch, att_w // LANE, nblk),
            in_specs=[pl.BlockSpec((B, LANE), lambda b, p, i, s: (b * nblk + i, p)),
                      pl.BlockSpec((seq, LANE), lambda b, p, i, s: (b, p)),
                      pl.BlockSpec((seq, LANE), lambda b, p, i, s: (b, p)),
                      pl.BlockSpec((nblk, LANE), lambda b, p, i, s: (b, p))],
            out_specs=pl.BlockSpec((B, LANE), lambda b, p, i, s: (b * nblk + i, p)),
            scratch_shapes=[pltpu.VMEM((nblk, B, LANE), BF16), pltpu.VMEM((nblk, LANE, B), BF16),
                            pltpu.VMEM((nblk, B), F32), pltpu.VMEM((1, B), F32), pltpu.VMEM((1, B), F32),
                            pltpu.VMEM((LANE, B), F32)]),
        out_shape=jax.ShapeDtypeStruct((n, att_w), F32),
        compiler_params=_params("arbitrary", "arbitrary", "arbitrary"),
        name="moba_prompt",
    )(slopes, q, k, v, kmean)


def _memkv_kernel(x_ref, w_ref, mk_ref, mv_ref, *, mem_w):
    r = _dot(x_ref[...].astype(BF16), w_ref[...])
    mk_ref[...] = r[:, :mem_w]
    mv_ref[...] = r[:, mem_w:]


def _memkv(mem2d, w_bf, *, tile, mem_w):
    n, d = mem2d.shape
    return pl.pallas_call(
        functools.partial(_memkv_kernel, mem_w=mem_w),
        grid=(n // tile,),
        in_specs=[pl.BlockSpec((tile, d), lambda i: (i, 0)), pl.BlockSpec(w_bf.shape, lambda i: (0, 0))],
        out_specs=[pl.BlockSpec((tile, mem_w), lambda i: (i, 0))] * 2,
        out_shape=[jax.ShapeDtypeStruct((n, mem_w), F32)] * 2,
        compiler_params=_params("arbitrary"),
        name="memkv",
    )(mem2d, w_bf)


def _memattn_kernel(qm_ref, mk_ref, mv_ref, o_ref, *, tq, mem_w):
    sub = lax.broadcasted_iota(I32, (LANE, tq), 0)
    for pr in range(mem_w // LANE):
        cs = slice(LANE * pr, LANE * (pr + 1))
        qp = qm_ref[:, cs]
        mkp = mk_ref[:, cs].astype(BF16)
        mvT = mv_ref[:, cs].T.astype(BF16)
        outs = []
        for hh in range(2):
            qh = jnp.where(_head_mask(qp.shape, hh), qp, jnp.zeros_like(qp))
            s = _dot_nt(mkp, qh)
            m = jnp.max(s, axis=0, keepdims=True)
            e = jnp.exp(s - m)
            l = jnp.sum(e, axis=0, keepdims=True)
            outs.append(_dot(mvT, e.astype(BF16)) / l)
        o_ref[:, cs] = jnp.where(sub < HEAD_DIM, outs[0], outs[1]).T


def _memattn_prompt(qm, mk, mv, *, batch, seq, n_mem, tq):
    n, mem_w = qm.shape
    nq = seq // tq
    return pl.pallas_call(
        functools.partial(_memattn_kernel, tq=tq, mem_w=mem_w),
        grid=(batch, nq),
        in_specs=[pl.BlockSpec((tq, mem_w), lambda b, i: (b * nq + i, 0)),
                  pl.BlockSpec((n_mem, mem_w), lambda b, i: (b, 0)),
                  pl.BlockSpec((n_mem, mem_w), lambda b, i: (b, 0))],
        out_specs=pl.BlockSpec((tq, mem_w), lambda b, i: (b * nq + i, 0)),
        out_shape=jax.ShapeDtypeStruct((n, mem_w), F32),
        compiler_params=_params("arbitrary", "arbitrary"),
        name="memattn_prompt",
    )(qm, mk, mv)


def _pack_bf16_pair(a, b):
    ua = lax.bitcast_convert_type(a.astype(BF16).astype(F32), U32)
    ub = lax.bitcast_convert_type(b.astype(BF16).astype(F32), U32)
    return ua | (ub >> 16)


def _unpack_bf16_pair(w):
    a = lax.bitcast_convert_type(w & jnp.uint32(0xFFFF0000), F32)
    b = lax.bitcast_convert_type(w << 16, F32)
    return a, b


def _finish1_kernel(x_ref, oa_ref, oc_ref, om_ref, gmix_ref, wout_ref, g1_ref, b1_ref, wrT_ref, br_ref,
                    x1_ref, x1p_ref, route_ref, gw_ref, cnt_ref, carry, *, tile, n_exp, alpha):
    i = pl.program_id(0)

    @pl.when(i == 0)
    def _():
        carry[...] = jnp.zeros_like(carry)

    mix = jnp.concatenate([_rms(oa_ref[...]), _rms(oc_ref[...]), _rms(om_ref[...])], axis=-1) * gmix_ref[...]
    z = alpha * x_ref[...] + _dot(mix.astype(BF16), wout_ref[...])
    x1 = _layernorm(z, g1_ref[...], b1_ref[...])
    x1_ref[...] = x1
    half = x1.shape[1] // 2
    x1p_ref[...] = _pack_bf16_pair(x1[:, :half], x1[:, half:])

    g = _dot_nt(wrT_ref[...], x1, precision=HI) + br_ref[...]
    eidx = lax.broadcasted_iota(I32, (n_exp, tile), 0)
    picks, vals = [], []
    for k in range(TOP_K):
        m = jnp.max(g, axis=0, keepdims=True)
        idx = jnp.min(jnp.where(g == m, eidx, n_exp), axis=0, keepdims=True)
        pick = eidx == idx
        route_ref[k:k + 1, :] = idx
        picks.append(pick)
        vals.append(m)
        g = jnp.where(pick, -jnp.inf, g)
    ex = [jnp.exp(v - vals[0]) for v in vals]
    denom = ex[0] + ex[1] + ex[2] + ex[3]
    for k in range(TOP_K):
        gw_ref[k:k + 1, :] = ex[k] / denom
    gw_ref[TOP_K:, :] = jnp.zeros((gw_ref.shape[0] - TOP_K, tile), F32)

    onehot = jnp.zeros((n_exp, tile), F32)
    for pick in picks:
        onehot = onehot + jnp.where(pick, 1.0, 0.0)
    t_src = lax.broadcasted_iota(I32, (tile, tile), 0)
    t_dst = lax.broadcasted_iota(I32, (tile, tile), 1)
    before = jnp.where(t_src < t_dst, 1.0, 0.0).astype(BF16)
    pos = _dot(onehot.astype(BF16), before) + carry[...]
    for k in range(TOP_K):
        rank = jnp.sum(jnp.where(picks[k], pos, 0.0), axis=0, keepdims=True)
        route_ref[TOP_K + k:TOP_K + k + 1, :] = rank.astype(I32)
    carry[...] = carry[...] + jnp.sum(onehot, axis=1, keepdims=True)
    cnt_ref[...] = jnp.broadcast_to(carry[...], cnt_ref.shape)


def _finish1(x2d, oa, oc, om, g_mix, wout_bf, ln_g, ln_b, w_rT, b_r, *, tile, alpha):
    n, d = x2d.shape
    n_exp = w_rT.shape[0]
    kern = functools.partial(_finish1_kernel, tile=tile, n_exp=n_exp, alpha=alpha)
    row = lambda w: pl.BlockSpec((tile, w), lambda i: (i, 0))
    full = lambda a: pl.BlockSpec(a.shape, lambda i: (0,) * a.ndim)
    return pl.pallas_call(
        kern,
        grid=(n // tile,),
        in_specs=[row(d), row(oa.shape[1]), row(oc.shape[1]), row(om.shape[1]), full(g_mix), full(wout_bf),
                  full(ln_g), full(ln_b), full(w_rT), full(b_r)],
        out_specs=[row(d), row(d // 2), pl.BlockSpec((2 * TOP_K, tile), lambda i: (0, i)),
                   pl.BlockSpec((2 * TOP_K, tile), lambda i: (0, i)), pl.BlockSpec((n_exp, LANE), lambda i: (0, 0))],
        out_shape=[jax.ShapeDtypeStruct((n, d), F32), jax.ShapeDtypeStruct((n, d // 2), U32),
                   jax.ShapeDtypeStruct((2 * TOP_K, n), I32), jax.ShapeDtypeStruct((2 * TOP_K, n), F32),
                   jax.ShapeDtypeStruct((n_exp, LANE), F32)],
        scratch_shapes=[pltpu.VMEM((n_exp, 1), F32)],
        compiler_params=_params("arbitrary"),
        name="finish1",
    )(x2d, oa, oc, om, g_mix, wout_bf, ln_g, ln_b, w_rT, b_r)


def _dispatch_kernel(offs_ref, route_ref, x1p_ref, buf_ref, sem, *, tile):
    def row_copy(t, d):
        return pltpu.make_async_copy(x1p_ref.at[pl.ds(t, 1), :], buf_ref.at[pl.ds(d, 1), :], sem)

    def issue(t, c):
        for k in range(TOP_K):
            d = offs_ref[route_ref[k, t]] + route_ref[TOP_K + k, t]
            row_copy(t, d).start()
        return c

    lax.fori_loop(0, tile, issue, 0)

    def drain(t, c):
        for k in range(TOP_K):
            row_copy(0, 0).wait()
        return c

    lax.fori_loop(0, tile, drain, 0)


def _dispatch(offs, route, x1p, *, tile):
    n, w = x1p.shape
    return pl.pallas_call(
        functools.partial(_dispatch_kernel, tile=tile),
        grid_spec=pltpu.PrefetchScalarGridSpec(
            num_scalar_prefetch=1,
            grid=(n // tile,),
            in_specs=[pl.BlockSpec((2 * TOP_K, tile), lambda i, s: (0, i), memory_space=pltpu.SMEM),
                      pl.BlockSpec((tile, w), lambda i, s: (i, 0))],
            out_specs=pl.BlockSpec(memory_space=pl.ANY),
            scratch_shapes=[pltpu.SemaphoreType.DMA(())]),
        out_shape=jax.ShapeDtypeStruct((n * TOP_K, w), U32),
        compiler_params=_params("arbitrary"),
        name="dispatch",
    )(offs, route, x1p)


def _ffn_kernel(tile_ref, exp_ref, lo_ref, hi_ref, lhs_ref, wgu_ref, bgu_ref, wd_ref, bd_ref, out_ref, *, tm, d_ff):
    g = pl.program_id(0)
    lo = lo_ref[g]
    hi = hi_ref[g]

    @pl.when(hi > lo)
    def _():
        a, b = _unpack_bf16_pair(lhs_ref[...])
        lhs = jnp.concatenate([a, b], axis=1).astype(BF16)
        gu = _dot(lhs, wgu_ref[0]) + bgu_ref[0]
        gate = jnp.minimum(gu[:, :d_ff], SWIGLU_LIMIT)
        up = jnp.clip(gu[:, d_ff:], -SWIGLU_LIMIT, SWIGLU_LIMIT)
        hid = (up + 1.0) * gate * jax.nn.sigmoid(SWIGLU_ALPHA * gate)
        o = _dot(hid.astype(BF16), wd_ref[0]) + bd_ref[0]

        @pl.when(lo == 0)
        def _():
            out_ref[...] = o

        @pl.when(lo > 0)
        def _():
            rows = lax.broadcasted_iota(I32, (tm, 1), 0)
            out_ref[...] = jnp.where((rows >= lo) & (rows < hi), o, out_ref[...])


def _ffn(sched, buf, wgu_bf, b_gu, wd_bf, b_d, *, tm):
    m, w = buf.shape
    n_exp, d, d_ff2 = wgu_bf.shape
    tiles, experts, los, his = sched
    return pl.pallas_call(
        functools.partial(_ffn_kernel, tm=tm, d_ff=d_ff2 // 2),
        grid_spec=pltpu.PrefetchScalarGridSpec(
            num_scalar_prefetch=4,
            grid=(tiles.shape[0],),
            in_specs=[pl.BlockSpec((tm, w), lambda g, t, e, lo, hi: (t[g], 0)),
                      pl.BlockSpec((1, d, d_ff2), lambda g, t, e, lo, hi: (e[g], 0, 0)),
                      pl.BlockSpec((1, 1, d_ff2), lambda g, t, e, lo, hi: (e[g], 0, 0)),
                      pl.BlockSpec((1, d_ff2 // 2, d), lambda g, t, e, lo, hi: (e[g], 0, 0)),
                      pl.BlockSpec((1, 1, d), lambda g, t, e, lo, hi: (e[g], 0, 0))],
            out_specs=pl.BlockSpec((tm, d), lambda g, t, e, lo, hi: (t[g], 0))),
        out_shape=jax.ShapeDtypeStruct((m, d), F32),
        compiler_params=_params("arbitrary"),
        name="expert_ffn",
    )(tiles, experts, los, his, buf, wgu_bf, b_gu, wd_bf, b_d)


def _moe_schedule(counts, n_rows, tm):
    n_exp = counts.shape[0]
    n_tiles = n_rows // tm
    n_items = n_tiles + n_exp - 1
    ends = jnp.cumsum(counts)
    starts = ends - counts
    first_tile = starts // tm
    n_it = jnp.where(counts > 0, (ends - 1) // tm - first_tile + 1, 0)
    it_end = jnp.cumsum(n_it)
    it_start = it_end - n_it
    g = jnp.minimum(jnp.arange(n_items, dtype=I32), it_end[-1] - 1)
    e = jnp.minimum(jnp.searchsorted(it_end, g, side="right"), n_exp - 1).astype(I32)
    tile = (first_tile[e] + g - it_start[e]).astype(I32)
    valid = jnp.arange(n_items, dtype=I32) < it_end[-1]
    lo = jnp.where(valid, jnp.clip(starts[e] - tile * tm, 0, tm), 0).astype(I32)
    hi = jnp.where(valid, jnp.clip(ends[e] - tile * tm, 0, tm), 0).astype(I32)
    return (tile, e, lo, hi), starts.astype(I32)


def _combine_kernel(offs_ref, route_ref, gw_ref, x1_ref, g2_ref, b2_ref, eo_ref, y_ref, gbuf, sem, *, tile, alpha):
    def row_copy(k, t, d):
        return pltpu.make_async_copy(eo_ref.at[pl.ds(d, 1), :], gbuf.at[k, pl.ds(t, 1), :], sem)

    def issue(t, c):
        for k in range(TOP_K):
            d = offs_ref[route_ref[k, t]] + route_ref[TOP_K + k, t]
            row_copy(k, t, d).start()
        return c

    lax.fori_loop(0, tile, issue, 0)

    def drain(t, c):
        for k in range(TOP_K):
            row_copy(0, 0, 0).wait()
        return c

    lax.fori_loop(0, tile, drain, 0)

    r = lax.broadcasted_iota(I32, (tile, tile), 0)
    c = lax.broadcasted_iota(I32, (tile, tile), 1)
    wcol = _dot_nt(jnp.where(r == c, 1.0, 0.0), gw_ref[...], precision=HI)
    moe = wcol[:, 0:1] * gbuf[0]
    for k in range(1, TOP_K):
        moe = moe + wcol[:, k:k + 1] * gbuf[k]
    y_ref[...] = _layernorm(alpha * x1_ref[...] + moe, g2_ref[...], b2_ref[...])


def _combine(offs, route, gw, x1, ln_g, ln_b, eo, *, tile, alpha):
    n, d = x1.shape
    full = lambda a: pl.BlockSpec(a.shape, lambda i, s: (0,) * a.ndim)
    return pl.pallas_call(
        functools.partial(_combine_kernel, tile=tile, alpha=alpha),
        grid_spec=pltpu.PrefetchScalarGridSpec(
            num_scalar_prefetch=1,
            grid=(n // tile,),
            in_specs=[pl.BlockSpec((2 * TOP_K, tile), lambda i, s: (0, i), memory_space=pltpu.SMEM),
                      pl.BlockSpec((2 * TOP_K, tile), lambda i, s: (0, i)),
                      pl.BlockSpec((tile, d), lambda i, s: (i, 0)), full(ln_g), full(ln_b),
                      pl.BlockSpec(memory_space=pl.ANY)],
            out_specs=pl.BlockSpec((tile, d), lambda i, s: (i, 0)),
            scratch_shapes=[pltpu.VMEM((TOP_K, tile, d), F32), pltpu.SemaphoreType.DMA(())]),
        out_shape=jax.ShapeDtypeStruct((n, d), F32),
        compiler_params=_params("arbitrary"),
        name="combine",
    )(offs, route, gw, x1, ln_g, ln_b, eo)


def _finish(x2d, oa, oc, om, wts, *, tile, tm, alpha):
    x1, x1p, route, gw, cnt = _finish1(x2d, oa, oc, om, wts["g_mix"], wts["w_out"], wts["ln1_g"], wts["ln1_b"],
                                       wts["w_rT"], wts["b_r"], tile=tile, alpha=alpha)
    counts = cnt[:, 0].astype(I32)
    sched, offs = _moe_schedule(counts, x2d.shape[0] * TOP_K, tm)
    buf = _dispatch(offs, route, x1p, tile=tile)
    eo = _ffn(sched, buf, wts["w_gu"], wts["b_gu"], wts["w_d"], wts["b_d"], tm=tm)
    return _combine(offs, route, gw, x1, wts["ln2_g"], wts["ln2_b"], eo, tile=tile, alpha=alpha)


def _sample_inproj_kernel(x_ref, w_ref, wc_ref, p0_ref, p1_ref, q_ref, k_ref, v_ref, qm_ref, oc_ref, u_ref,
                          *, att_w, mem_w, conv_w):
    x = x_ref[...].astype(BF16)
    c1, c2, c3 = att_w, 2 * att_w, 3 * att_w
    c4 = c3 + mem_w
    c5, c6, c7 = c4 + conv_w, c4 + 2 * conv_w, c4 + 3 * conv_w

    def proj(lo, hi):
        return _dot(x, w_ref[:, lo:hi])

    q_ref[...] = proj(0, c1) * Q_SCALE
    k_ref[...] = proj(c1, c2)
    v_ref[...] = proj(c2, c3)
    qm_ref[...] = proj(c3, c4) * Q_SCALE
    u = proj(c5, c6) * proj(c6, c7)
    cz = wc_ref[0:1, :] * p0_ref[...] + wc_ref[1:2, :] * p1_ref[...] + wc_ref[2:3, :] * u
    oc_ref[...] = proj(c4, c5) * cz
    u_ref[...] = u


def _sample_inproj(x2d, w_bf, w_conv, prev0, prev1, *, att_w, mem_w, conv_w):
    n = x2d.shape[0]
    args = (x2d, w_bf, w_conv, prev0, prev1)
    full = lambda a: pl.BlockSpec(a.shape, lambda i: (0,) * a.ndim)
    widths = (att_w, att_w, att_w, mem_w, conv_w, conv_w)
    return pl.pallas_call(
        functools.partial(_sample_inproj_kernel, att_w=att_w, mem_w=mem_w, conv_w=conv_w),
        grid=(1,),
        in_specs=[full(a) for a in args],
        out_specs=[pl.BlockSpec((n, w), lambda i: (0, 0)) for w in widths],
        out_shape=[jax.ShapeDtypeStruct((n, w), F32) for w in widths],
        compiler_params=_params("arbitrary"),
        name="sample_inproj",
    )(*args)


def _kscan_kernel(pt_ref, q_ref, ck_ref, sel_ref, kbuf, ksum, sem, *, n_pages, chunk, n_heads):
    b = pl.program_id(0)
    n_chunks = n_pages // chunk
    pages_per_blk = MOBA_BLOCK // PAGE_SIZE
    nblk, att_w = ksum.shape

    def copies(c, slot):
        return [pltpu.make_async_copy(ck_ref.at[pt_ref[b * n_pages + c * chunk + j]], kbuf.at[slot, j], sem.at[slot])
                for j in range(chunk)]

    for cp in copies(0, 0):
        cp.start()

    def body(c, carry):
        slot = c % 2

        @pl.when(c + 1 < n_chunks)
        def _():
            for cp in copies(c + 1, 1 - slot):
                cp.start()

        for cp in copies(c, slot):
            cp.wait()
        for jb in range(chunk // pages_per_blk):
            s = jnp.sum(kbuf[slot, pages_per_blk * jb], axis=0, keepdims=True)
            for pg in range(1, pages_per_blk):
                s = s + jnp.sum(kbuf[slot, pages_per_blk * jb + pg], axis=0, keepdims=True)
            ksum[pl.ds(c * (chunk // pages_per_blk) + jb, 1), :] = s
        return carry

    lax.fori_loop(0, n_chunks, body, 0)

    prod = ksum[...] * (1.0 / MOBA_BLOCK) * q_ref[pl.ds(b, 1), :]
    col_head = lax.broadcasted_iota(I32, (att_w, LANE), 0) >> 6
    head = lax.broadcasted_iota(I32, (att_w, LANE), 1)
    g = _dot(prod, jnp.where(col_head == head, 1.0, 0.0), precision=HI)
    blk = lax.broadcasted_iota(I32, (nblk, LANE), 0)
    for r in range(MOBA_TOPK):
        m = jnp.max(g, axis=0, keepdims=True)
        idx = jnp.min(jnp.where((g == m) & (g > -jnp.inf), blk, nblk), axis=0, keepdims=True)
        sel_ref[0, r:r + 1, :] = idx
        g = jnp.where(blk == idx, -jnp.inf, g)
    sel_ref[0, MOBA_TOPK:, :] = jnp.zeros((SUBLANE - MOBA_TOPK, LANE), I32)


def _kscan(pt_flat, q, ck, *, n_pages, n_heads, chunk):
    bs, att_w = q.shape
    nblk = n_pages * PAGE_SIZE // MOBA_BLOCK
    return pl.pallas_call(
        functools.partial(_kscan_kernel, n_pages=n_pages, chunk=chunk, n_heads=n_heads),
        grid_spec=pltpu.PrefetchScalarGridSpec(
            num_scalar_prefetch=1,
            grid=(bs,),
            in_specs=[pl.BlockSpec(q.shape, lambda b, pt: (0, 0)), pl.BlockSpec(memory_space=pl.ANY)],
            out_specs=pl.BlockSpec((1, SUBLANE, LANE), lambda b, pt: (b, 0, 0)),
            scratch_shapes=[pltpu.VMEM((2, chunk, PAGE_SIZE, att_w), F32), pltpu.VMEM((nblk, att_w), F32),
                            pltpu.SemaphoreType.DMA((2,))]),
        out_shape=jax.ShapeDtypeStruct((bs, SUBLANE, LANE), I32),
        compiler_params=_params("arbitrary"),
        name="kscan",
    )(pt_flat, q, ck)


def _sample_attn_kernel(pt_ref, sel_ref, slopes_ref, q_ref, kn_ref, vn_ref, qm_ref, mk_ref, mv_ref, ck_ref, cv_ref,
                        oa_ref, om_ref, kbuf, vbuf, sem, *, n_pages, n_heads, past_len, mem_w):
    b = pl.program_id(0)
    B = MOBA_BLOCK
    pages_per_blk = B // PAGE_SIZE
    n_keys = MOBA_TOPK * B
    blks = [[sel_ref[(b * MOBA_TOPK + s) * n_heads + h] for s in range(MOBA_TOPK)] for h in range(n_heads)]

    copies = []
    for h in range(n_heads):
        cols = pl.ds((h // 2) * LANE, LANE)
        for s in range(MOBA_TOPK):
            for half in range(pages_per_blk):
                pg = pt_ref[b * n_pages + pages_per_blk * blks[h][s] + half]
                rows = pl.ds(s * B + half * PAGE_SIZE, PAGE_SIZE)
                copies.append(pltpu.make_async_copy(ck_ref.at[pg, :, cols], kbuf.at[h, rows, :], sem.at[0]))
                copies.append(pltpu.make_async_copy(cv_ref.at[pg, :, cols], vbuf.at[h, rows, :], sem.at[1]))
    for cp in copies:
        cp.start()
    for cp in copies:
        cp.wait()

    q_row = q_ref[pl.ds(b, 1), :]
    kn = kn_ref[pl.ds(b, 1), :]
    vn = vn_ref[pl.ds(b, 1), :]
    key_lane = lax.broadcasted_iota(I32, (1, n_keys), 1)
    lane = lax.broadcasted_iota(I32, (1, LANE), 1)
    outs = []
    for h in range(n_heads):
        cs = slice((h // 2) * LANE, (h // 2 + 1) * LANE)
        qh = jnp.where(_head_mask((1, LANE), h % 2), q_row[:, cs], 0.0)
        q8 = jnp.broadcast_to(qh, (SUBLANE, LANE))
        blk_of_key = jnp.where(key_lane < B, blks[h][0], jnp.where(key_lane < 2 * B, blks[h][1], blks[h][2]))
        dist = (past_len - blk_of_key * B - (key_lane & (B - 1))).astype(F32)
        s = _dot_nt(q8, kbuf[h], precision=HI) - slopes_ref[h] * dist
        s_self = jnp.sum(qh * kn[:, cs], axis=1, keepdims=True)
        m = jnp.maximum(jnp.max(s, axis=1, keepdims=True), s_self)
        e = jnp.exp(s - m)
        e_self = jnp.exp(s_self - m)
        l = jnp.sum(e, axis=1, keepdims=True) + e_self
        o = (_dot(e, vbuf[h], precision=HI) + e_self * vn[:, cs]) / l
        outs.append(o[0:1, :])
    oa_ref[0] = jnp.concatenate(
        [jnp.where(lane < HEAD_DIM, outs[2 * pr], outs[2 * pr + 1]) for pr in range(n_heads // 2)], axis=1)

    qm_row = qm_ref[pl.ds(b, 1), :]
    outs = []
    for pr in range(mem_w // LANE):
        cs = slice(pr * LANE, (pr + 1) * LANE)
        mkp = mk_ref[0, :, cs]
        mvp = mv_ref[0, :, cs]
        pair = []
        for hh in range(2):
            qh = jnp.where(_head_mask((1, LANE), hh), qm_row[:, cs], 0.0)
            s = _dot_nt(jnp.broadcast_to(qh, (SUBLANE, LANE)), mkp, precision=HI)
            e = jnp.exp(s - jnp.max(s, axis=1, keepdims=True))
            o = _dot(e, mvp, precision=HI) / jnp.sum(e, axis=1, keepdims=True)
            pair.append(o[0:1, :])
        outs.append(jnp.where(lane < HEAD_DIM, pair[0], pair[1]))
    om_ref[0] = jnp.concatenate(outs, axis=1)


def _sample_attn(pt_flat, sel_flat, slopes, q, kn, vn, qm, mk, mv, ck, cv, *, n_pages, n_heads):
    bs, att_w = q.shape
    _, n_mem, mem_w = mk.shape
    n_keys = MOBA_TOPK * MOBA_BLOCK
    full = lambda a: pl.BlockSpec(a.shape, lambda b, pt, sel: (0,) * a.ndim)
    return pl.pallas_call(
        functools.partial(_sample_attn_kernel, n_pages=n_pages, n_heads=n_heads, past_len=n_pages * PAGE_SIZE,
                          mem_w=mem_w),
        grid_spec=pltpu.PrefetchScalarGridSpec(
            num_scalar_prefetch=2,
            grid=(bs,),
            in_specs=[pl.BlockSpec(memory_space=pltpu.SMEM), full(q), full(kn), full(vn), full(qm),
                      pl.BlockSpec((1, n_mem, mem_w), lambda b, pt, sel: (b, 0, 0)),
                      pl.BlockSpec((1, n_mem, mem_w), lambda b, pt, sel: (b, 0, 0)),
                      pl.BlockSpec(memory_space=pl.ANY), pl.BlockSpec(memory_space=pl.ANY)],
            out_specs=[pl.BlockSpec((1, 1, att_w), lambda b, pt, sel: (b, 0, 0)),
                       pl.BlockSpec((1, 1, mem_w), lambda b, pt, sel: (b, 0, 0))],
            scratch_shapes=[pltpu.VMEM((n_heads, n_keys, LANE), F32), pltpu.VMEM((n_heads, n_keys, LANE), F32),
                            pltpu.SemaphoreType.DMA((2,))]),
        out_shape=[jax.ShapeDtypeStruct((bs, 1, att_w), F32), jax.ShapeDtypeStruct((bs, 1, mem_w), F32)],
        compiler_params=_params("arbitrary"),
        name="sample_attn",
    )(pt_flat, sel_flat, slopes, q, kn, vn, qm, mk, mv, ck, cv)


def _sample_layer(x, cache_k, cache_v, mem_k, mem_v, state, page_table, wts, dims, alpha):
    bs, dec_seq, d = x.shape
    assert dec_seq == 1
    att_w, mem_w, conv_w = dims
    n_heads = att_w // HEAD_DIM
    n_pages = page_table.shape[1]
    assert (n_pages * PAGE_SIZE) % MOBA_BLOCK == 0 and n_pages * PAGE_SIZE // MOBA_BLOCK >= MOBA_TOPK
    x2d = x.reshape(bs, d)
    q, k, v, qm, oc, u = _sample_inproj(x2d, wts["w_in"], wts["w_conv"], state[:, 0, :], state[:, 1, :],
                                        att_w=att_w, mem_w=mem_w, conv_w=conv_w)
    n_phys = cache_k.shape[0]
    ck = cache_k.reshape(n_phys, PAGE_SIZE, att_w)
    cv = cache_v.reshape(n_phys, PAGE_SIZE, att_w)
    pt_flat = page_table.reshape(-1)
    sel = _kscan(pt_flat, q, ck, n_pages=n_pages, n_heads=n_heads, chunk=8)
    sel_flat = sel[:, :MOBA_TOPK, :n_heads].reshape(-1)
    oa, om = _sample_attn(pt_flat, sel_flat, wts["slopes"], q, k, v, qm, mem_k.reshape(bs, -1, mem_w),
                          mem_v.reshape(bs, -1, mem_w), ck, cv, n_pages=n_pages, n_heads=n_heads)
    y = _finish(x2d, oa.reshape(bs, att_w), oc, om.reshape(bs, mem_w), wts, tile=bs, tm=bs * TOP_K, alpha=alpha)
    conv_state = jnp.stack([state[:, 1, :], u], axis=1)
    return y.reshape(bs, 1, d), k, v, conv_state


def _pick_tile(n, pref):
    return pref if n % pref == 0 else n


def _prompt_layer(x, mem, wts, dims, alpha):
    batch, seq, d = x.shape
    att_w, mem_w, conv_w = dims
    n = batch * seq
    n_mem = mem.shape[1]
    x2d = x.reshape(n, d)
    tile = 512
    q, k, v, kmean, qm, oc, tail = _inproj(x2d, wts["w_in"], wts["w_conv"], seq=seq, tile=tile, att_w=att_w,
                                           mem_w=mem_w, conv_w=conv_w)
    oa = _moba_prompt(q, k, v, kmean.reshape(n // MOBA_BLOCK, att_w), wts["slopes"], batch=batch, seq=seq)
    mk, mv = _memkv(mem.reshape(batch * n_mem, d), wts["w_mem_kv"], tile=_pick_tile(batch * n_mem, 512), mem_w=mem_w)
    om = _memattn_prompt(qm, mk, mv, batch=batch, seq=seq, n_mem=n_mem, tq=256)
    y = _finish(x2d, oa, oc, om, wts, tile=256, tm=256, alpha=alpha)
    conv_state = tail.reshape(batch, seq // tile, SUBLANE, conv_w)[:, -1, SUBLANE - (CONV_K - 1):, :]
    return y.reshape(batch, seq, d), k, v, conv_state, mk, mv


def _prep_weights(l, w_in, w_mem_kv, w_conv, g_mix, w_out, ln1_g, ln1_b, w_router, b_router, w_gate_up, b_gate_up,
                  w_down, b_down, ln2_g, ln2_b):
    n_heads = 8
    row = lambda a: a[l][None, :]
    return {
        "w_in": w_in[l].astype(BF16), "w_mem_kv": w_mem_kv[l].astype(BF16), "w_conv": w_conv[l],
        "g_mix": row(g_mix), "w_out": w_out[l].astype(BF16), "ln1_g": row(ln1_g), "ln1_b": row(ln1_b),
        "w_rT": w_router[l].T, "b_r": b_router[l][:, None],
        "w_gu": w_gate_up[l].astype(BF16), "b_gu": b_gate_up[l][:, None, :],
        "w_d": w_down[l].astype(BF16), "b_d": b_down[l][:, None, :],
        "ln2_g": row(ln2_g), "ln2_b": row(ln2_b),
        "slopes": 2.0 ** (-8.0 * jnp.arange(1, n_heads + 1, dtype=F32) / n_heads),
    }


def kernel(x_prompt, x_sample, cache_k, cache_v, cache_mem_k, cache_mem_v, state_conv, page_table, mem_prompt, w_in, w_mem_kv, w_conv, g_mix, w_out, ln1_g, ln1_b, w_router, b_router, w_gate_up, b_gate_up, w_down, b_down, ln2_g, ln2_b):
    depth = w_in.shape[0]
    assert depth == 1
    alpha = (2 * depth) ** 0.25
    n_heads = cache_k.shape[3]
    att_w = n_heads * HEAD_DIM
    mem_w = cache_mem_k.shape[3] * HEAD_DIM
    conv_w = state_conv.shape[3]
    dims = (att_w, mem_w, conv_w)
    wts = _prep_weights(0, w_in, w_mem_kv, w_conv, g_mix, w_out, ln1_g, ln1_b, w_router, b_router, w_gate_up,
                        b_gate_up, w_down, b_down, ln2_g, ln2_b)
    bp, seq, d = x_prompt.shape
    n_mem = mem_prompt.shape[1]
    y_p, k_p, v_p, conv_p, mk_p, mv_p = _prompt_layer(x_prompt, mem_prompt, wts, dims, alpha)
    bs = x_sample.shape[0]
    y_s, k_s, v_s, conv_s = _sample_layer(x_sample, cache_k[0], cache_v[0], cache_mem_k[0], cache_mem_v[0],
                                          state_conv[0], page_table, wts, dims, alpha)
    return (y_p, y_s,
            k_p.reshape(1, bp, seq, n_heads, HEAD_DIM), v_p.reshape(1, bp, seq, n_heads, HEAD_DIM),
            conv_p[None], mk_p.reshape(1, bp, n_mem, mem_w // HEAD_DIM, HEAD_DIM),
            mv_p.reshape(1, bp, n_mem, mem_w // HEAD_DIM, HEAD_DIM),
            k_s.reshape(1, bs, 1, n_heads, HEAD_DIM), v_s.reshape(1, bs, 1, n_heads, HEAD_DIM), conv_s[None])
```

```python
import functools

import jax
import jax.numpy as jnp
from jax import lax
from jax.experimental import pallas as pl
from jax.experimental.pallas import tpu as pltpu

F32 = jnp.float32
BF16 = jnp.bfloat16
I32 = jnp.int32

HEAD_DIM = 64
MOBA_BLOCK = 256
MOBA_TOPK = 3
PAGE_SIZE = 128
TOP_K = 4
CONV_K = 3
SWIGLU_LIMIT = 7.0
SWIGLU_ALPHA = 1.702
LN_EPS = 1e-5
Q_SCALE = HEAD_DIM ** -0.5
NEG = -1e30
LANE = 128
SUBLANE = 8
VMEM_LIMIT = 56 * 1024 * 1024
HI = lax.Precision.HIGHEST


def _params(*sem):
    return pltpu.CompilerParams(dimension_semantics=sem, vmem_limit_bytes=VMEM_LIMIT)


def _dot_nt(a, b, precision=None):
    return lax.dot_general(a, b, (((1,), (1,)), ((), ())), precision=precision, preferred_element_type=F32)


def _dot(a, b, precision=None):
    return jnp.dot(a, b, precision=precision, preferred_element_type=F32)


def _layernorm(z, g, b):
    zc = z - jnp.mean(z, axis=-1, keepdims=True)
    var = jnp.mean(zc * zc, axis=-1, keepdims=True)
    return zc * lax.rsqrt(var + LN_EPS) * g + b


def _rms(a):
    return a * lax.rsqrt(jnp.mean(a * a, axis=-1, keepdims=True) + LN_EPS)


def _head_mask(shape, hh):
    lane = lax.broadcasted_iota(I32, shape, len(shape) - 1)
    return (lane >= HEAD_DIM * hh) & (lane < HEAD_DIM * (hh + 1))


def _inproj_kernel(x_ref, w_ref, wc_ref, q_ref, k_ref, v_ref, km_ref, qm_ref, oc_ref, tail_ref, ubuf,
                   *, tile, tiles_per_seq, att_w, mem_w, conv_w):
    i = pl.program_id(0)
    x = x_ref[...].astype(BF16)
    c1, c2, c3 = att_w, 2 * att_w, 3 * att_w
    c4 = c3 + mem_w
    c5, c6, c7 = c4 + conv_w, c4 + 2 * conv_w, c4 + 3 * conv_w

    def proj(lo, hi):
        return _dot(x, w_ref[:, lo:hi])

    q_ref[...] = (proj(0, c1) * Q_SCALE).astype(BF16)
    k = proj(c1, c2)
    k_ref[...] = k
    km_ref[0] = jnp.sum(k.reshape(tile // MOBA_BLOCK, MOBA_BLOCK, att_w), axis=1) * (1.0 / MOBA_BLOCK)
    v_ref[...] = proj(c2, c3)
    qm_ref[...] = (proj(c3, c4) * Q_SCALE).astype(BF16)
    gb = proj(c4, c5)
    u = proj(c5, c6) * proj(c6, c7)

    @pl.when(i % tiles_per_seq == 0)
    def _():
        ubuf[0:SUBLANE, :] = jnp.zeros((SUBLANE, conv_w), F32)

    ubuf[SUBLANE:SUBLANE + tile, :] = u
    u1 = ubuf[SUBLANE - 1:SUBLANE - 1 + tile, :]
    u2 = ubuf[SUBLANE - 2:SUBLANE - 2 + tile, :]
    cz = wc_ref[0:1, :] * u2 + wc_ref[1:2, :] * u1 + wc_ref[2:3, :] * u
    oc_ref[...] = gb * cz
    tail = ubuf[tile:tile + SUBLANE, :]
    tail_ref[0] = tail
    ubuf[0:SUBLANE, :] = tail


def _inproj(x2d, w_bf, w_conv, *, seq, tile, att_w, mem_w, conv_w):
    n, d = x2d.shape
    nt = n // tile
    kern = functools.partial(_inproj_kernel, tile=tile, tiles_per_seq=seq // tile, att_w=att_w, mem_w=mem_w,
                             conv_w=conv_w)
    row = lambda w: pl.BlockSpec((tile, w), lambda i: (i, 0))
    return pl.pallas_call(
        kern,
        grid=(nt,),
        in_specs=[row(d), pl.BlockSpec(w_bf.shape, lambda i: (0, 0)), pl.BlockSpec(w_conv.shape, lambda i: (0, 0))],
        out_specs=[row(att_w), row(att_w), row(att_w),
                   pl.BlockSpec((1, tile // MOBA_BLOCK, att_w), lambda i: (i, 0, 0)),
                   row(mem_w), row(conv_w), pl.BlockSpec((1, SUBLANE, conv_w), lambda i: (i, 0, 0))],
        out_shape=[jax.ShapeDtypeStruct((n, att_w), BF16), jax.ShapeDtypeStruct((n, att_w), F32),
                   jax.ShapeDtypeStruct((n, att_w), F32),
                   jax.ShapeDtypeStruct((nt, tile // MOBA_BLOCK, att_w), F32),
                   jax.ShapeDtypeStruct((n, mem_w), BF16), jax.ShapeDtypeStruct((n, conv_w), F32),
                   jax.ShapeDtypeStruct((nt, SUBLANE, conv_w), F32)],
        scratch_shapes=[pltpu.VMEM((tile + SUBLANE, conv_w), F32)],
        compiler_params=_params("arbitrary"),
        name="inproj",
    )(x2d, w_bf, w_conv)


def _moba_kernel(slopes_ref, q_ref, k_ref, v_ref, km_ref, o_ref, kbf, vT, selb, m_s, l_s, acc_s, *, nblk):
    p = pl.program_id(1)
    i = pl.program_id(2)
    B = MOBA_BLOCK

    @pl.when(i == 0)
    def _():
        for c in range(nblk):
            kbf[c] = k_ref[c * B:(c + 1) * B, :].astype(BF16)
            vT[c] = v_ref[c * B:(c + 1) * B, :].T.astype(BF16)

    W = 2 * B
    q = q_ref[...]
    zero = jnp.zeros_like(q)
    q2 = jnp.concatenate([jnp.where(_head_mask(q.shape, 0), q, zero), jnp.where(_head_mask(q.shape, 1), q, zero)],
                         axis=0)
    second = lax.broadcasted_iota(I32, (1, W), 1) >= B
    slope = jnp.where(second, slopes_ref[2 * p + 1], slopes_ref[2 * p])
    key_i = lax.broadcasted_iota(I32, (B, W), 0)
    qry_i = lax.broadcasted_iota(I32, (B, W), 1) & (B - 1)
    base = slope * (key_i - qry_i).astype(F32)
    blk = lax.broadcasted_iota(I32, (nblk, W), 0)
    g = _dot_nt(km_ref[...], q2.astype(F32), precision=HI)
    g = jnp.where(blk < i, g, -jnp.inf)
    sb = jnp.full((nblk, W), NEG, F32)
    for _ in range(MOBA_TOPK):
        m = jnp.max(g, axis=0, keepdims=True)
        cand = (g == m) & (g > -jnp.inf)
        idx = jnp.min(jnp.where(cand, blk, nblk), axis=0, keepdims=True)
        pick = blk == idx
        sb = jnp.where(pick, 0.0, sb)
        g = jnp.where(pick, -jnp.inf, g)
    selb[...] = sb
    s = jnp.where(key_i <= qry_i, _dot_nt(kbf[i], q2) + base, NEG)
    m0 = jnp.max(s, axis=0, keepdims=True)
    p0 = jnp.exp(s - m0)
    m_s[...] = m0
    l_s[...] = jnp.sum(p0, axis=0, keepdims=True)
    acc_s[...] = _dot(vT[i], p0.astype(BF16))

    def scores(j):
        off = slope * (B * (i - j)).astype(F32)
        return _dot_nt(kbf[j], q2) + base + (selb[pl.ds(j, 1), :] - off)

    def update(js):
        ss = [scores(j) for j in js]
        m_old = m_s[...]
        m_new = m_old
        for s in ss:
            m_new = jnp.maximum(m_new, jnp.max(s, axis=0, keepdims=True))
        a = jnp.exp(m_old - m_new)
        l_new = a * l_s[...]
        acc = a * acc_s[...]
        for j, s in zip(js, ss):
            pj = jnp.exp(s - m_new)
            l_new = l_new + jnp.sum(pj, axis=0, keepdims=True)
            acc = acc + _dot(vT[j], pj.astype(BF16))
        l_s[...] = l_new
        acc_s[...] = acc
        m_s[...] = m_new

    def pair(jp, carry):
        update([2 * jp, 2 * jp + 1])
        return carry

    lax.fori_loop(0, i // 2, pair, 0)

    @pl.when(i % 2 == 1)
    def _():
        update([i - 1])
    o = acc_s[...] / l_s[...]
    sub = lax.broadcasted_iota(I32, (2 * HEAD_DIM, B), 0)
    o_ref[...] = jnp.where(sub < HEAD_DIM, o[:, :B], o[:, B:]).T


def _moba_prompt(q, k, v, kmean, slopes, *, batch, seq):
    n, att_w = q.shape
    nblk = seq // MOBA_BLOCK
    B = MOBA_BLOCK
    kern = functools.partial(_moba_kernel, nblk=nblk)
    return pl.pallas_call(
        kern,
        grid_spec=pltpu.PrefetchScalarGridSpec(
            num_scalar_prefetch=1,
            grid=(batch, att_w // LANE, nblk),
            in_specs=[pl.BlockSpec((B, LANE), lambda b, p, i, s: (b * nblk + i, p)),
                      pl.BlockSpec((seq, LANE), lambda b, p, i, s: (b, p)),
                      pl.BlockSpec((seq, LANE), lambda b, p, i, s: (b, p)),
                      pl.BlockSpec((nblk, LANE), lambda b, p, i, s: (b, p))],
            out_specs=pl.BlockSpec((B, LANE), lambda b, p, i, s: (b * nblk + i, p)),
            scratch_shapes=[pltpu.VMEM((nblk, B, LANE), BF16), pltpu.VMEM((nblk, LANE, B), BF16),
                            pltpu.VMEM((nblk, 2 * B), F32), pltpu.VMEM((1, 2 * B), F32),
                            pltpu.VMEM((1, 2 * B), F32), pltpu.VMEM((LANE, 2 * B), F32)]),
        out_shape=jax.ShapeDtypeStruct((n, att_w), F32),
        compiler_params=_params("arbitrary", "arbitrary", "arbitrary"),
        name="moba_prompt",
    )(slopes, q, k, v, kmean)


def _memkv_kernel(x_ref, w_ref, mk_ref, mv_ref, *, mem_w):
    r = _dot(x_ref[...].astype(BF16), w_ref[...])
    mk_ref[...] = r[:, :mem_w]
    mv_ref[...] = r[:, mem_w:]


def _memkv(mem2d, w_bf, *, tile, mem_w):
    n, d = mem2d.shape
    return pl.pallas_call(
        functools.partial(_memkv_kernel, mem_w=mem_w),
        grid=(n // tile,),
        in_specs=[pl.BlockSpec((tile, d), lambda i: (i, 0)), pl.BlockSpec(w_bf.shape, lambda i: (0, 0))],
        out_specs=[pl.BlockSpec((tile, mem_w), lambda i: (i, 0))] * 2,
        out_shape=[jax.ShapeDtypeStruct((n, mem_w), F32)] * 2,
        compiler_params=_params("arbitrary"),
        name="memkv",
    )(mem2d, w_bf)


def _memattn_kernel(qm_ref, mk_ref, mv_ref, o_ref, *, tq, mem_w):
    sub = lax.broadcasted_iota(I32, (LANE, tq), 0)
    for pr in range(mem_w // LANE):
        cs = slice(LANE * pr, LANE * (pr + 1))
        qp = qm_ref[:, cs]
        mkp = mk_ref[:, cs].astype(BF16)
        mvT = mv_ref[:, cs].T.astype(BF16)
        outs = []
        for hh in range(2):
            qh = jnp.where(_head_mask(qp.shape, hh), qp, jnp.zeros_like(qp))
            s = _dot_nt(mkp, qh)
            m = jnp.max(s, axis=0, keepdims=True)
            e = jnp.exp(s - m)
            l = jnp.sum(e, axis=0, keepdims=True)
            outs.append(_dot(mvT, e.astype(BF16)) / l)
        o_ref[:, cs] = jnp.where(sub < HEAD_DIM, outs[0], outs[1]).T


def _memattn_prompt(qm, mk, mv, *, batch, seq, n_mem, tq):
    n, mem_w = qm.shape
    nq = seq // tq
    return pl.pallas_call(
        functools.partial(_memattn_kernel, tq=tq, mem_w=mem_w),
        grid=(batch, nq),
        in_specs=[pl.BlockSpec((tq, mem_w), lambda b, i: (b * nq + i, 0)),
                  pl.BlockSpec((n_mem, mem_w), lambda b, i: (b, 0)),
                  pl.BlockSpec((n_mem, mem_w), lambda b, i: (b, 0))],
        out_specs=pl.BlockSpec((tq, mem_w), lambda b, i: (b * nq + i, 0)),
        out_shape=jax.ShapeDtypeStruct((n, mem_w), F32),
        compiler_params=_params("arbitrary", "arbitrary"),
        name="memattn_prompt",
    )(qm, mk, mv)


SORT_TILE = 256


def _sort_rows(ts, n_exp):
    return TOP_K * ts + SUBLANE * n_exp


def _finish1_kernel(x_ref, oa_ref, oc_ref, om_ref, gmix_ref, wout_ref, g1_ref, b1_ref, wrT_ref, br_ref,
                    x1_ref, xs_ref, route_ref, gw_ref, cpad_ref, loff_ref, cbase_ref, tot_ref, carry,
                    *, tile, ts, n_exp, alpha):
    i = pl.program_id(0)
    lt = _sort_rows(ts, n_exp)

    @pl.when(i == 0)
    def _():
        carry[...] = jnp.zeros_like(carry)

    mix = jnp.concatenate([_rms(oa_ref[...]), _rms(oc_ref[...]), _rms(om_ref[...])], axis=-1) * gmix_ref[...]
    z = alpha * x_ref[...] + _dot(mix.astype(BF16), wout_ref[...])
    x1 = _layernorm(z, g1_ref[...], b1_ref[...])
    x1_ref[...] = x1
    x1b = x1.astype(BF16)

    g = _dot_nt(wrT_ref[...], x1, precision=HI) + br_ref[...]
    eidx = lax.broadcasted_iota(I32, (n_exp, tile), 0)
    picks, vals = [], []
    for k in range(TOP_K):
        m = jnp.max(g, axis=0, keepdims=True)
        idx = jnp.min(jnp.where(g == m, eidx, n_exp), axis=0, keepdims=True)
        pick = eidx == idx
        route_ref[k:k + 1, :] = idx
        picks.append(pick)
        vals.append(m)
        g = jnp.where(pick, -jnp.inf, g)
    ex = [jnp.exp(v - vals[0]) for v in vals]
    denom = ex[0] + ex[1] + ex[2] + ex[3]
    for k in range(TOP_K):
        gw_ref[k:k + 1, :] = ex[k] / denom
    gw_ref[TOP_K:, :] = jnp.zeros((gw_ref.shape[0] - TOP_K, tile), F32)

    t_src = lax.broadcasted_iota(I32, (ts, ts), 0)
    t_dst = lax.broadcasted_iota(I32, (ts, ts), 1)
    before = jnp.where(t_src < t_dst, 1.0, 0.0).astype(BF16)
    e_src = lax.broadcasted_iota(I32, (n_exp, n_exp), 1)
    e_dst = lax.broadcasted_iota(I32, (n_exp, n_exp), 0)
    lower = jnp.where(e_src < e_dst, 1.0, 0.0).astype(BF16)
    slot = lax.broadcasted_iota(I32, (lt, ts), 0)
    for sub in range(tile // ts):
        cs = slice(sub * ts, (sub + 1) * ts)
        pk = [p[:, cs] for p in picks]
        onehot = jnp.zeros((n_exp, ts), F32)
        for p in pk:
            onehot = onehot + jnp.where(p, 1.0, 0.0)
        cnt = jnp.sum(onehot, axis=1, keepdims=True)
        cpad = jnp.floor((cnt + (SUBLANE - 1)) * (1.0 / SUBLANE)) * SUBLANE
        cpad_l = jnp.broadcast_to(cpad, (n_exp, LANE))
        loff_l = _dot(lower, cpad_l.astype(BF16))
        pos = _dot(onehot.astype(BF16), before) + loff_l[:, 0:1]
        perm = jnp.zeros((lt, ts), F32)
        for k in range(TOP_K):
            lpos = jnp.sum(jnp.where(pk[k], pos, 0.0), axis=0, keepdims=True).astype(I32)
            route_ref[TOP_K + k:TOP_K + k + 1, cs] = lpos
            perm = perm + jnp.where(slot == lpos, 1.0, 0.0)
        perm = perm.astype(BF16)
        xs_ref[sub * lt:(sub + 1) * lt, :] = _dot(perm, x1b[cs, :])
        cpad_ref[sub] = cpad_l
        loff_ref[sub] = loff_l
        cbase_ref[sub] = jnp.broadcast_to(carry[...], (n_exp, LANE))
        carry[...] = carry[...] + cpad
    tot_ref[...] = jnp.broadcast_to(carry[...], tot_ref.shape)


def _finish1(x2d, oa, oc, om, g_mix, wout_bf, ln_g, ln_b, w_rT, b_r, *, tile, ts, alpha):
    n, d = x2d.shape
    n_exp = w_rT.shape[0]
    lt = _sort_rows(ts, n_exp)
    spt = tile // ts
    kern = functools.partial(_finish1_kernel, tile=tile, ts=ts, n_exp=n_exp, alpha=alpha)
    row = lambda w: pl.BlockSpec((tile, w), lambda i: (i, 0))
    full = lambda a: pl.BlockSpec(a.shape, lambda i: (0,) * a.ndim)
    meta = pl.BlockSpec((spt, n_exp, LANE), lambda i: (i, 0, 0))
    meta_shape = jax.ShapeDtypeStruct((n // ts, n_exp, LANE), F32)
    return pl.pallas_call(
        kern,
        grid=(n // tile,),
        in_specs=[row(d), row(oa.shape[1]), row(oc.shape[1]), row(om.shape[1]), full(g_mix), full(wout_bf),
                  full(ln_g), full(ln_b), full(w_rT), full(b_r)],
        out_specs=[row(d), pl.BlockSpec((spt * lt, d), lambda i: (i, 0)),
                   pl.BlockSpec((2 * TOP_K, tile), lambda i: (0, i)), pl.BlockSpec((2 * TOP_K, tile), lambda i: (0, i)),
                   meta, meta, meta, pl.BlockSpec((n_exp, LANE), lambda i: (0, 0))],
        out_shape=[jax.ShapeDtypeStruct((n, d), F32), jax.ShapeDtypeStruct((n // ts * lt, d), F32),
                   jax.ShapeDtypeStruct((2 * TOP_K, n), I32), jax.ShapeDtypeStruct((2 * TOP_K, n), F32),
                   meta_shape, meta_shape, meta_shape, jax.ShapeDtypeStruct((n_exp, LANE), F32)],
        scratch_shapes=[pltpu.VMEM((n_exp, 1), F32)],
        compiler_params=_params("arbitrary"),
        name="finish1",
    )(x2d, oa, oc, om, g_mix, wout_bf, ln_g, ln_b, w_rT, b_r)


def _run_sizes(ts):
    sizes, s = [], SUBLANE
    while s <= max(ts, SUBLANE):
        sizes.append(s)
        s *= 2
    return sizes


def _for_each_piece(length, sizes, fn):
    off = 0
    for sz in sizes:
        @pl.when((length & sz) != 0)
        def _(off=off, sz=sz):
            fn(off, sz)
        off = off + (length & sz)


def _dispatch_kernel(cpad_ref, loff_ref, base_ref, fill_ref, xs_ref, zero_ref, buf_ref, sem, fill_sem,
                     *, lt, n_exp, sizes, tm):
    t = pl.program_id(0)

    def run(e, wait):
        n = cpad_ref[t * n_exp + e]
        src = loff_ref[t * n_exp + e]
        dst = base_ref[t * n_exp + e]

        def piece(off, sz):
            cp = pltpu.make_async_copy(xs_ref.at[pl.ds(pl.multiple_of(src + off, SUBLANE), sz), :],
                                       buf_ref.at[pl.ds(pl.multiple_of(dst + off, SUBLANE), sz), :], sem)
            cp.wait() if wait else cp.start()

        _for_each_piece(n, sizes, piece)

    def issue(e, c):
        run(e, False)
        return c

    def drain(e, c):
        run(e, True)
        return c

    lax.fori_loop(0, n_exp, issue, 0)

    @pl.when(t == pl.num_programs(0) - 1)
    def _():
        start = fill_ref[0]
        rest = buf_ref.shape[0] - start
        n_full = rest // tm

        def chunk(c, wait):
            cp = pltpu.make_async_copy(zero_ref, buf_ref.at[pl.ds(pl.multiple_of(start + c * tm, SUBLANE), tm), :],
                                       fill_sem)
            cp.wait() if wait else cp.start()

        def piece(wait):
            def fn(off, sz):
                cp = pltpu.make_async_copy(
                    zero_ref.at[pl.ds(0, sz), :],
                    buf_ref.at[pl.ds(pl.multiple_of(start + n_full * tm + off, SUBLANE), sz), :], fill_sem)
                cp.wait() if wait else cp.start()
            return fn

        tail_sizes = [s for s in _run_sizes(tm) if s < tm]
        for wait in (False, True):
            lax.fori_loop(0, n_full, lambda c, carry, wait=wait: (chunk(c, wait), carry)[1], 0)
            _for_each_piece(rest - n_full * tm, tail_sizes, piece(wait))

    lax.fori_loop(0, n_exp, drain, 0)


def _dispatch(cpad, loff, base, fill, xs, *, ts, n_exp, m_pad, tm):
    w = xs.shape[1]
    lt = _sort_rows(ts, n_exp)
    n_sub = xs.shape[0] // lt
    zero = jnp.zeros((tm, w), F32)
    return pl.pallas_call(
        functools.partial(_dispatch_kernel, lt=lt, n_exp=n_exp, sizes=_run_sizes(ts), tm=tm),
        grid_spec=pltpu.PrefetchScalarGridSpec(
            num_scalar_prefetch=4,
            grid=(n_sub,),
            in_specs=[pl.BlockSpec((lt, w), lambda t, *_: (t, 0)), pl.BlockSpec((tm, w), lambda t, *_: (0, 0))],
            out_specs=pl.BlockSpec(memory_space=pl.ANY),
            scratch_shapes=[pltpu.SemaphoreType.DMA(()), pltpu.SemaphoreType.DMA(())]),
        out_shape=jax.ShapeDtypeStruct((m_pad, w), F32),
        compiler_params=_params("arbitrary"),
        name="dispatch",
    )(cpad, loff, base, fill, xs, zero)


def _ffn_kernel(otile_ref, tile_ref, exp_ref, lo_ref, hi_ref, lhs_ref, wgu_ref, bgu_ref, wd_ref, bd_ref, out_ref,
                *, tm, d_ff):
    g = pl.program_id(0)
    lo = lo_ref[g]
    hi = hi_ref[g]

    @pl.when(hi == lo)
    def _():
        out_ref[...] = jnp.zeros_like(out_ref)

    @pl.when(hi > lo)
    def _():
        gu = _dot(lhs_ref[...].astype(BF16), wgu_ref[0]) + bgu_ref[0]
        gate = jnp.minimum(gu[:, :d_ff], SWIGLU_LIMIT)
        up = jnp.clip(gu[:, d_ff:], -SWIGLU_LIMIT, SWIGLU_LIMIT)
        hid = (up + 1.0) * gate * jax.nn.sigmoid(SWIGLU_ALPHA * gate)
        o = _dot(hid.astype(BF16), wd_ref[0]) + bd_ref[0]

        @pl.when(lo == 0)
        def _():
            out_ref[...] = o

        @pl.when(lo > 0)
        def _():
            rows = lax.broadcasted_iota(I32, (tm, 1), 0)
            out_ref[...] = jnp.where((rows >= lo) & (rows < hi), o, out_ref[...])


def _ffn(sched, buf, wgu_bf, b_gu, wd_bf, b_d, *, tm):
    m, w = buf.shape
    n_exp, d, d_ff2 = wgu_bf.shape
    out_tiles, tiles, experts, los, his = sched
    return pl.pallas_call(
        functools.partial(_ffn_kernel, tm=tm, d_ff=d_ff2 // 2),
        grid_spec=pltpu.PrefetchScalarGridSpec(
            num_scalar_prefetch=5,
            grid=(tiles.shape[0],),
            in_specs=[pl.BlockSpec((tm, w), lambda g, ot, t, e, lo, hi: (t[g], 0)),
                      pl.BlockSpec((1, d, d_ff2), lambda g, ot, t, e, lo, hi: (e[g], 0, 0)),
                      pl.BlockSpec((1, 1, d_ff2), lambda g, ot, t, e, lo, hi: (e[g], 0, 0)),
                      pl.BlockSpec((1, d_ff2 // 2, d), lambda g, ot, t, e, lo, hi: (e[g], 0, 0)),
                      pl.BlockSpec((1, 1, d), lambda g, ot, t, e, lo, hi: (e[g], 0, 0))],
            out_specs=pl.BlockSpec((tm, d), lambda g, ot, t, e, lo, hi: (ot[g], 0))),
        out_shape=jax.ShapeDtypeStruct((m, d), F32),
        compiler_params=_params("arbitrary"),
        name="expert_ffn",
    )(out_tiles, tiles, experts, los, his, buf, wgu_bf, b_gu, wd_bf, b_d)


def _moe_schedule(counts, n_rows, tm):
    n_exp = counts.shape[0]
    n_tiles = n_rows // tm
    n_items = n_tiles + n_exp - 1
    ends = jnp.cumsum(counts)
    starts = ends - counts
    first_tile = starts // tm
    n_it = jnp.where(counts > 0, (ends - 1) // tm - first_tile + 1, 0)
    it_end = jnp.cumsum(n_it)
    it_start = it_end - n_it
    item = jnp.arange(n_items, dtype=I32)
    n_real = it_end[-1]
    g = jnp.minimum(item, n_real - 1)
    e = jnp.minimum(jnp.sum((it_end[None, :] <= g[:, None]).astype(I32), axis=1), n_exp - 1)
    tile = (first_tile[e] + g - it_start[e]).astype(I32)
    valid = item < n_real
    lo = jnp.where(valid, jnp.clip(starts[e] - tile * tm, 0, tm), 0).astype(I32)
    hi = jnp.where(valid, jnp.clip(ends[e] - tile * tm, 0, tm), 0).astype(I32)
    used = (ends[-1] + tm - 1) // tm
    out_tile = jnp.where(valid, tile, jnp.minimum(used + item - n_real, n_tiles - 1)).astype(I32)
    return (out_tile, tile, e, lo, hi), starts.astype(I32)


def _combine_kernel(cpad_ref, loff_ref, base_ref, route_ref, gw_ref, x1_ref, g2_ref, b2_ref, eo_ref, y_ref,
                    gbuf, sem, *, ts, n_exp, sizes, alpha):
    t = pl.program_id(0)
    nt = pl.num_programs(0)
    lt = gbuf.shape[1]
    d = gbuf.shape[2]

    def tile_copies(tt, slot, wait):
        def run(e, c):
            n = cpad_ref[tt * n_exp + e]
            src = base_ref[tt * n_exp + e]
            dst = loff_ref[tt * n_exp + e]

            def piece(off, sz):
                cp = pltpu.make_async_copy(eo_ref.at[pl.ds(pl.multiple_of(src + off, SUBLANE), sz), :],
                                           gbuf.at[slot, pl.ds(pl.multiple_of(dst + off, SUBLANE), sz), :],
                                           sem.at[slot])
                cp.wait() if wait else cp.start()

            _for_each_piece(n, sizes, piece)
            return c

        lax.fori_loop(0, n_exp, run, 0)

    def fetch(tt, slot):
        gbuf[slot, TOP_K * ts:, :] = jnp.zeros((lt - TOP_K * ts, d), F32)
        tile_copies(tt, slot, False)

    slot = t % 2

    @pl.when(t == 0)
    def _():
        fetch(t, slot)

    @pl.when(t + 1 < nt)
    def _():
        fetch(t + 1, 1 - slot)

    tile_copies(t, slot, True)

    r = lax.broadcasted_iota(I32, (ts, ts), 0)
    c = lax.broadcasted_iota(I32, (ts, ts), 1)
    eye = jnp.where(r == c, 1.0, 0.0)
    wcol = _dot_nt(eye, gw_ref[...], precision=HI)
    pcol = _dot_nt(eye, route_ref[...].astype(F32), precision=HI)
    rows = gbuf[slot].astype(BF16)
    slot_i = lax.broadcasted_iota(I32, (ts, lt), 1)
    unsort = jnp.zeros((ts, lt), F32)
    for k in range(TOP_K):
        lpos = pcol[:, TOP_K + k:TOP_K + k + 1].astype(I32)
        unsort = unsort + jnp.where(slot_i == lpos, wcol[:, k:k + 1], 0.0)
    hi = unsort.astype(BF16)
    lo = (unsort - hi.astype(F32)).astype(BF16)
    moe = _dot(hi, rows) + _dot(lo, rows)
    y_ref[...] = _layernorm(alpha * x1_ref[...] + moe, g2_ref[...], b2_ref[...])


def _combine(cpad, loff, base, route, gw, x1, ln_g, ln_b, eo, *, ts, n_exp, alpha):
    n, d = x1.shape
    lt = _sort_rows(ts, n_exp)
    full = lambda a: pl.BlockSpec(a.shape, lambda i, *_: (0,) * a.ndim)
    return pl.pallas_call(
        functools.partial(_combine_kernel, ts=ts, n_exp=n_exp, sizes=_run_sizes(ts), alpha=alpha),
        grid_spec=pltpu.PrefetchScalarGridSpec(
            num_scalar_prefetch=3,
            grid=(n // ts,),
            in_specs=[pl.BlockSpec((2 * TOP_K, ts), lambda i, *_: (0, i)),
                      pl.BlockSpec((2 * TOP_K, ts), lambda i, *_: (0, i)),
                      pl.BlockSpec((ts, d), lambda i, *_: (i, 0)), full(ln_g), full(ln_b),
                      pl.BlockSpec(memory_space=pl.ANY)],
            out_specs=pl.BlockSpec((ts, d), lambda i, *_: (i, 0)),
            scratch_shapes=[pltpu.VMEM((2, lt, d), F32), pltpu.SemaphoreType.DMA((2,))]),
        out_shape=jax.ShapeDtypeStruct((n, d), F32),
        compiler_params=_params("arbitrary"),
        name="combine",
    )(cpad, loff, base, route, gw, x1, ln_g, ln_b, eo)


def _finish(x2d, oa, oc, om, wts, *, tile, tm, alpha):
    n = x2d.shape[0]
    n_exp = wts["w_rT"].shape[0]
    ts = _pick_tile(n, SORT_TILE)
    n_sub = n // ts
    x1, xs, route, gw, cpad, loff, cbase, tot = _finish1(
        x2d, oa, oc, om, wts["g_mix"], wts["w_out"], wts["ln1_g"], wts["ln1_b"], wts["w_rT"], wts["b_r"],
        tile=_pick_tile(n, tile), ts=ts, alpha=alpha)
    m_pad = n_sub * _sort_rows(ts, n_exp) + tm
    counts = tot[:, 0].astype(I32)
    sched, offs = _moe_schedule(counts, m_pad, tm)
    flat = lambda a: a[:, :, 0].astype(I32).reshape(-1)
    cpad, loff = flat(cpad), flat(loff)
    base = (cbase[:, :, 0].astype(I32) + offs[None, :]).reshape(-1)
    fill = jnp.sum(counts, keepdims=True)
    buf = _dispatch(cpad, loff, base, fill, xs, ts=ts, n_exp=n_exp, m_pad=m_pad, tm=tm)
    eo = _ffn(sched, buf, wts["w_gu"], wts["b_gu"], wts["w_d"], wts["b_d"], tm=tm)
    return _combine(cpad, loff, base, route, gw, x1, wts["ln2_g"], wts["ln2_b"], eo, ts=ts, n_exp=n_exp, alpha=alpha)


def _sample_inproj_kernel(x_ref, w_ref, wc_ref, p0_ref, p1_ref, q_ref, k_ref, v_ref, qm_ref, oc_ref, u_ref,
                          *, att_w, mem_w, conv_w):
    x = x_ref[...].astype(BF16)
    c1, c2, c3 = att_w, 2 * att_w, 3 * att_w
    c4 = c3 + mem_w
    c5, c6, c7 = c4 + conv_w, c4 + 2 * conv_w, c4 + 3 * conv_w

    def proj(lo, hi):
        return _dot(x, w_ref[:, lo:hi])

    q_ref[...] = proj(0, c1) * Q_SCALE
    k_ref[...] = proj(c1, c2)
    v_ref[...] = proj(c2, c3)
    qm_ref[...] = proj(c3, c4) * Q_SCALE
    u = proj(c5, c6) * proj(c6, c7)
    cz = wc_ref[0:1, :] * p0_ref[...] + wc_ref[1:2, :] * p1_ref[...] + wc_ref[2:3, :] * u
    oc_ref[...] = proj(c4, c5) * cz
    u_ref[...] = u


def _sample_inproj(x2d, w_bf, w_conv, prev0, prev1, *, att_w, mem_w, conv_w):
    n = x2d.shape[0]
    args = (x2d, w_bf, w_conv, prev0, prev1)
    full = lambda a: pl.BlockSpec(a.shape, lambda i: (0,) * a.ndim)
    widths = (att_w, att_w, att_w, mem_w, conv_w, conv_w)
    return pl.pallas_call(
        functools.partial(_sample_inproj_kernel, att_w=att_w, mem_w=mem_w, conv_w=conv_w),
        grid=(1,),
        in_specs=[full(a) for a in args],
        out_specs=[pl.BlockSpec((n, w), lambda i: (0, 0)) for w in widths],
        out_shape=[jax.ShapeDtypeStruct((n, w), F32) for w in widths],
        compiler_params=_params("arbitrary"),
        name="sample_inproj",
    )(*args)


def _kscan_kernel(pt_ref, q_ref, ck_ref, sel_ref, kbuf, ksum, sem, *, n_pages, chunk, n_heads):
    b = pl.program_id(0)
    n_chunks = n_pages // chunk
    pages_per_blk = MOBA_BLOCK // PAGE_SIZE
    nblk = ksum.shape[0]

    def copies(c, slot):
        return [pltpu.make_async_copy(ck_ref.at[0, pt_ref[b * n_pages + c * chunk + j]], kbuf.at[slot, j], sem.at[slot])
                for j in range(chunk)]

    for cp in copies(0, 0):
        cp.start()

    def body(c, carry):
        slot = c % 2

        @pl.when(c + 1 < n_chunks)
        def _():
            for cp in copies(c + 1, 1 - slot):
                cp.start()

        for cp in copies(c, slot):
            cp.wait()
        for jb in range(chunk // pages_per_blk):
            s = jnp.sum(kbuf[slot, pages_per_blk * jb], axis=0)
            for pg in range(1, pages_per_blk):
                s = s + jnp.sum(kbuf[slot, pages_per_blk * jb + pg], axis=0)
            ksum[c * (chunk // pages_per_blk) + jb] = s
        return carry

    lax.fori_loop(0, n_chunks, body, 0)

    g = jnp.sum(ksum[...] * q_ref[...], axis=2, keepdims=True) * (1.0 / MOBA_BLOCK)
    blk = lax.broadcasted_iota(I32, g.shape, 0)
    sel_ref[...] = jnp.zeros(sel_ref.shape, I32)
    for r in range(MOBA_TOPK):
        m = jnp.max(g, axis=0, keepdims=True)
        idx = jnp.min(jnp.where((g == m) & (g > -jnp.inf), blk, nblk), axis=0, keepdims=True)
        sel_ref[0, :, r:r + 1] = idx[0]
        g = jnp.where(blk == idx, -jnp.inf, g)


def _kscan(pt_flat, q3, ck, *, n_pages, chunk):
    bs, n_heads, hd = q3.shape
    nblk = n_pages * PAGE_SIZE // MOBA_BLOCK
    return pl.pallas_call(
        functools.partial(_kscan_kernel, n_pages=n_pages, chunk=chunk, n_heads=n_heads),
        grid_spec=pltpu.PrefetchScalarGridSpec(
            num_scalar_prefetch=1,
            grid=(bs,),
            in_specs=[pl.BlockSpec((1, n_heads, hd), lambda b, pt: (b, 0, 0)), pl.BlockSpec(memory_space=pl.ANY)],
            out_specs=pl.BlockSpec((1, n_heads, LANE), lambda b, pt: (b, 0, 0)),
            scratch_shapes=[pltpu.VMEM((2, chunk, PAGE_SIZE, n_heads, hd), F32), pltpu.VMEM((nblk, n_heads, hd), F32),
                            pltpu.SemaphoreType.DMA((2,))]),
        out_shape=jax.ShapeDtypeStruct((bs, n_heads, LANE), I32),
        compiler_params=_params("arbitrary"),
        name="kscan",
    )(pt_flat, q3, ck)


def _sample_attn_kernel(pt_ref, sel_ref, slopes_ref, q_ref, kn_ref, vn_ref, qm_ref, mk_ref, mv_ref, ck_ref, cv_ref,
                        oa_ref, om_ref, kbuf, vbuf, sem, *, n_pages, n_heads, past_len, mem_w):
    b = pl.program_id(0)
    nb = pl.num_programs(0)
    B = MOBA_BLOCK
    pages_per_blk = B // PAGE_SIZE
    n_keys = MOBA_TOPK * B

    def blocks(bb, h):
        return [sel_ref[(bb * n_heads + h) * MOBA_TOPK + s] for s in range(MOBA_TOPK)]

    def copies(bb, slot):
        cps = []
        for h in range(n_heads):
            blks = blocks(bb, h)
            for s in range(MOBA_TOPK):
                for half in range(pages_per_blk):
                    pg = pt_ref[bb * n_pages + pages_per_blk * blks[s] + half]
                    rows = pl.ds(s * B + half * PAGE_SIZE, PAGE_SIZE)
                    cps.append(pltpu.make_async_copy(ck_ref.at[0, pg, :, h, :], kbuf.at[slot, h, rows, :], sem.at[0, slot]))
                    cps.append(pltpu.make_async_copy(cv_ref.at[0, pg, :, h, :], vbuf.at[slot, h, rows, :], sem.at[1, slot]))
        return cps

    slot = b % 2

    @pl.when(b == 0)
    def _():
        for cp in copies(b, slot):
            cp.start()

    @pl.when(b + 1 < nb)
    def _():
        for cp in copies(b + 1, 1 - slot):
            cp.start()

    for cp in copies(b, slot):
        cp.wait()

    key_lane = lax.broadcasted_iota(I32, (1, n_keys), 1)
    lane = lax.broadcasted_iota(I32, (1, LANE), 1)
    for h in range(n_heads):
        blks = blocks(b, h)
        qh = q_ref[0, h:h + 1, :]
        q8 = jnp.broadcast_to(qh, (SUBLANE, HEAD_DIM))
        blk_of_key = jnp.where(key_lane < B, blks[0], jnp.where(key_lane < 2 * B, blks[1], blks[2]))
        dist = (past_len - blk_of_key * B - (key_lane & (B - 1))).astype(F32)
        s = _dot_nt(q8, kbuf[slot, h], precision=HI) - slopes_ref[h] * dist
        s_self = jnp.sum(qh * kn_ref[0, h:h + 1, :], axis=1, keepdims=True)
        m = jnp.maximum(jnp.max(s, axis=1, keepdims=True), s_self)
        e = jnp.exp(s - m)
        e_self = jnp.exp(s_self - m)
        l = jnp.sum(e, axis=1, keepdims=True) + e_self
        o = (_dot(e, vbuf[slot, h], precision=HI) + e_self * vn_ref[0, h:h + 1, :]) / l
        oa_ref[0, h:h + 1, :] = o[0:1, :]

    qm_row = qm_ref[pl.ds(b, 1), :]
    outs = []
    for pr in range(mem_w // LANE):
        cs = slice(pr * LANE, (pr + 1) * LANE)
        mkp = mk_ref[0, :, cs]
        mvp = mv_ref[0, :, cs]
        pair = []
        for hh in range(2):
            qh = jnp.where(_head_mask((1, LANE), hh), qm_row[:, cs], 0.0)
            s = _dot_nt(jnp.broadcast_to(qh, (SUBLANE, LANE)), mkp, precision=HI)
            e = jnp.exp(s - jnp.max(s, axis=1, keepdims=True))
            o = _dot(e, mvp, precision=HI) / jnp.sum(e, axis=1, keepdims=True)
            pair.append(o[0:1, :])
        outs.append(jnp.where(lane < HEAD_DIM, pair[0], pair[1]))
    om_ref[0] = jnp.concatenate(outs, axis=1)


def _sample_attn(pt_flat, sel_flat, slopes, q3, kn3, vn3, qm, mk, mv, ck, cv, *, n_pages):
    bs, n_heads, hd = q3.shape
    _, n_mem, mem_w = mk.shape
    n_keys = MOBA_TOPK * MOBA_BLOCK
    per_b = lambda a: pl.BlockSpec((1,) + a.shape[1:], lambda b, pt, sel: (b,) + (0,) * (a.ndim - 1))
    return pl.pallas_call(
        functools.partial(_sample_attn_kernel, n_pages=n_pages, n_heads=n_heads, past_len=n_pages * PAGE_SIZE,
                          mem_w=mem_w),
        grid_spec=pltpu.PrefetchScalarGridSpec(
            num_scalar_prefetch=2,
            grid=(bs,),
            in_specs=[pl.BlockSpec(memory_space=pltpu.SMEM), per_b(q3), per_b(kn3), per_b(vn3),
                      pl.BlockSpec(qm.shape, lambda b, pt, sel: (0, 0)), per_b(mk), per_b(mv),
                      pl.BlockSpec(memory_space=pl.ANY), pl.BlockSpec(memory_space=pl.ANY)],
            out_specs=[pl.BlockSpec((1, n_heads, hd), lambda b, pt, sel: (b, 0, 0)),
                       pl.BlockSpec((1, 1, mem_w), lambda b, pt, sel: (b, 0, 0))],
            scratch_shapes=[pltpu.VMEM((2, n_heads, n_keys, hd), F32), pltpu.VMEM((2, n_heads, n_keys, hd), F32),
                            pltpu.SemaphoreType.DMA((2, 2))]),
        out_shape=[jax.ShapeDtypeStruct((bs, n_heads, hd), F32), jax.ShapeDtypeStruct((bs, 1, mem_w), F32)],
        compiler_params=_params("arbitrary"),
        name="sample_attn",
    )(pt_flat, sel_flat, slopes, q3, kn3, vn3, qm, mk, mv, ck, cv)


def _sample_layer(x, cache_k, cache_v, mem_k, mem_v, state, page_table, wts, dims, alpha):
    bs, dec_seq, d = x.shape
    assert dec_seq == 1
    att_w, mem_w, conv_w = dims
    n_heads = att_w // HEAD_DIM
    n_pages = page_table.shape[1]
    assert (n_pages * PAGE_SIZE) % MOBA_BLOCK == 0 and n_pages * PAGE_SIZE // MOBA_BLOCK >= MOBA_TOPK
    x2d = x.reshape(bs, d)
    q, k, v, qm, oc, u = _sample_inproj(x2d, wts["w_in"], wts["w_conv"], state[:, 0, :], state[:, 1, :],
                                        att_w=att_w, mem_w=mem_w, conv_w=conv_w)
    pt_flat = page_table.reshape(-1)
    heads = lambda a: a.reshape(bs, n_heads, HEAD_DIM)
    sel = _kscan(pt_flat, heads(q), cache_k, n_pages=n_pages, chunk=8)
    sel_flat = sel[:, :, :MOBA_TOPK].reshape(-1)
    oa, om = _sample_attn(pt_flat, sel_flat, wts["slopes"], heads(q), heads(k), heads(v), qm,
                          mem_k.reshape(bs, -1, mem_w), mem_v.reshape(bs, -1, mem_w), cache_k, cache_v,
                          n_pages=n_pages)
    y = _finish(x2d, oa.reshape(bs, att_w), oc, om.reshape(bs, mem_w), wts, tile=bs, tm=bs * TOP_K, alpha=alpha)
    conv_state = jnp.stack([state[:, 1, :], u], axis=1)
    return y.reshape(bs, 1, d), k, v, conv_state


def _pick_tile(n, pref):
    return pref if n % pref == 0 else n


def _prompt_layer(x, mem, wts, dims, alpha):
    batch, seq, d = x.shape
    att_w, mem_w, conv_w = dims
    n = batch * seq
    n_mem = mem.shape[1]
    x2d = x.reshape(n, d)
    tile = 512
    q, k, v, kmean, qm, oc, tail = _inproj(x2d, wts["w_in"], wts["w_conv"], seq=seq, tile=tile, att_w=att_w,
                                           mem_w=mem_w, conv_w=conv_w)
    oa = _moba_prompt(q, k, v, kmean.reshape(n // MOBA_BLOCK, att_w), wts["slopes"], batch=batch, seq=seq)
    mk, mv = _memkv(mem.reshape(batch * n_mem, d), wts["w_mem_kv"], tile=_pick_tile(batch * n_mem, 512), mem_w=mem_w)
    om = _memattn_prompt(qm, mk, mv, batch=batch, seq=seq, n_mem=n_mem, tq=512)
    y = _finish(x2d, oa, oc, om, wts, tile=512, tm=256, alpha=alpha)
    conv_state = tail.reshape(batch, seq // tile, SUBLANE, conv_w)[:, -1, SUBLANE - (CONV_K - 1):, :]
    return y.reshape(batch, seq, d), k, v, conv_state, mk, mv


def _prep_weights(l, w_in, w_mem_kv, w_conv, g_mix, w_out, ln1_g, ln1_b, w_router, b_router, w_gate_up, b_gate_up,
                  w_down, b_down, ln2_g, ln2_b):
    n_heads = 8
    row = lambda a: a[l][None, :]
    return {
        "w_in": w_in[l].astype(BF16), "w_mem_kv": w_mem_kv[l].astype(BF16), "w_conv": w_conv[l],
        "g_mix": row(g_mix), "w_out": w_out[l].astype(BF16), "ln1_g": row(ln1_g), "ln1_b": row(ln1_b),
        "w_rT": w_router[l].T, "b_r": b_router[l][:, None],
        "w_gu": w_gate_up[l].astype(BF16), "b_gu": b_gate_up[l][:, None, :],
        "w_d": w_down[l].astype(BF16), "b_d": b_down[l][:, None, :],
        "ln2_g": row(ln2_g), "ln2_b": row(ln2_b),
        "slopes": 2.0 ** (-8.0 * jnp.arange(1, n_heads + 1, dtype=F32) / n_heads),
    }


def kernel(x_prompt, x_sample, cache_k, cache_v, cache_mem_k, cache_mem_v, state_conv, page_table, mem_prompt, w_in, w_mem_kv, w_conv, g_mix, w_out, ln1_g, ln1_b, w_router, b_router, w_gate_up, b_gate_up, w_down, b_down, ln2_g, ln2_b):
    depth = w_in.shape[0]
    assert depth == 1
    alpha = (2 * depth) ** 0.25
    n_heads = cache_k.shape[3]
    att_w = n_heads * HEAD_DIM
    mem_w = cache_mem_k.shape[3] * HEAD_DIM
    conv_w = state_conv.shape[3]
    dims = (att_w, mem_w, conv_w)
    wts = _prep_weights(0, w_in, w_mem_kv, w_conv, g_mix, w_out, ln1_g, ln1_b, w_router, b_router, w_gate_up,
                        b_gate_up, w_down, b_down, ln2_g, ln2_b)
    bp, seq, d = x_prompt.shape
    n_mem = mem_prompt.shape[1]
    y_p, k_p, v_p, conv_p, mk_p, mv_p = _prompt_layer(x_prompt, mem_prompt, wts, dims, alpha)
    bs = x_sample.shape[0]
    y_s, k_s, v_s, conv_s = _sample_layer(x_sample, cache_k, cache_v, cache_mem_k[0], cache_mem_v[0],
                                          state_conv[0], page_table, wts, dims, alpha)
    return (y_p, y_s,
            k_p.reshape(1, bp, seq, n_heads, HEAD_DIM), v_p.reshape(1, bp, seq, n_heads, HEAD_DIM),
            conv_p[None], mk_p.reshape(1, bp, n_mem, mem_w // HEAD_DIM, HEAD_DIM),
            mv_p.reshape(1, bp, n_mem, mem_w // HEAD_DIM, HEAD_DIM),
            k_s.reshape(1, bs, 1, n_heads, HEAD_DIM), v_s.reshape(1, bs, 1, n_heads, HEAD_DIM), conv_s[None])
```

```python
import functools

import jax
import jax.numpy as jnp
from jax import lax
from jax.experimental import pallas as pl
from jax.experimental.pallas import tpu as pltpu

F32 = jnp.float32
BF16 = jnp.bfloat16
I32 = jnp.int32

HEAD_DIM = 64
MOBA_BLOCK = 256
MOBA_TOPK = 3
PAGE_SIZE = 128
TOP_K = 4
CONV_K = 3
SWIGLU_LIMIT = 7.0
SWIGLU_ALPHA = 1.702
LN_EPS = 1e-5
Q_SCALE = HEAD_DIM ** -0.5
NEG = -1e30
LANE = 128
SUBLANE = 8
VMEM_LIMIT = 56 * 1024 * 1024
HI = lax.Precision.HIGHEST


def _params(*sem):
    return pltpu.CompilerParams(dimension_semantics=sem, vmem_limit_bytes=VMEM_LIMIT)


def _dot_nt(a, b, precision=None):
    return lax.dot_general(a, b, (((1,), (1,)), ((), ())), precision=precision, preferred_element_type=F32)


def _dot(a, b, precision=None):
    return jnp.dot(a, b, precision=precision, preferred_element_type=F32)


def _layernorm(z, g, b):
    zc = z - jnp.mean(z, axis=-1, keepdims=True)
    var = jnp.mean(zc * zc, axis=-1, keepdims=True)
    return zc * lax.rsqrt(var + LN_EPS) * g + b


def _rms(a):
    return a * lax.rsqrt(jnp.mean(a * a, axis=-1, keepdims=True) + LN_EPS)


def _head_mask(shape, hh):
    lane = lax.broadcasted_iota(I32, shape, len(shape) - 1)
    return (lane >= HEAD_DIM * hh) & (lane < HEAD_DIM * (hh + 1))


def _inproj_kernel(x_ref, w_ref, wc_ref, q_ref, k_ref, v_ref, kt_ref, vt_ref, km_ref, qm_ref, oc_ref, tail_ref, ubuf,
                   *, tile, tiles_per_seq, att_w, mem_w, conv_w):
    i = pl.program_id(0)
    x = x_ref[...].astype(BF16)
    c1, c2, c3 = att_w, 2 * att_w, 3 * att_w
    c4 = c3 + mem_w
    c5, c6, c7 = c4 + conv_w, c4 + 2 * conv_w, c4 + 3 * conv_w
    n_heads = att_w // HEAD_DIM

    def proj(lo, hi):
        return _dot(x, w_ref[:, lo:hi])

    q_ref[...] = (proj(0, c1) * Q_SCALE).astype(BF16)
    k = proj(c1, c2)
    k_ref[...] = k
    kt_ref[0] = k.T.reshape(n_heads, HEAD_DIM, tile)
    km_ref[0] = jnp.sum(k.reshape(tile // MOBA_BLOCK, MOBA_BLOCK, att_w), axis=1) * (1.0 / MOBA_BLOCK)
    v = proj(c2, c3)
    v_ref[...] = v
    vt_ref[0] = v.T.reshape(n_heads, HEAD_DIM, tile)
    qm_ref[...] = (proj(c3, c4) * Q_SCALE).astype(BF16)
    gb = proj(c4, c5)
    u = proj(c5, c6) * proj(c6, c7)

    @pl.when(i % tiles_per_seq == 0)
    def _():
        ubuf[0:SUBLANE, :] = jnp.zeros((SUBLANE, conv_w), F32)

    ubuf[SUBLANE:SUBLANE + tile, :] = u
    u1 = ubuf[SUBLANE - 1:SUBLANE - 1 + tile, :]
    u2 = ubuf[SUBLANE - 2:SUBLANE - 2 + tile, :]
    cz = wc_ref[0:1, :] * u2 + wc_ref[1:2, :] * u1 + wc_ref[2:3, :] * u
    oc_ref[...] = gb * cz
    tail = ubuf[tile:tile + SUBLANE, :]
    tail_ref[0] = tail
    ubuf[0:SUBLANE, :] = tail


def _inproj(x2d, w_bf, w_conv, *, seq, tile, att_w, mem_w, conv_w):
    n, d = x2d.shape
    nt = n // tile
    kern = functools.partial(_inproj_kernel, tile=tile, tiles_per_seq=seq // tile, att_w=att_w, mem_w=mem_w,
                             conv_w=conv_w)
    row = lambda w: pl.BlockSpec((tile, w), lambda i: (i, 0))
    tps = seq // tile
    n_heads = att_w // HEAD_DIM
    headsT = pl.BlockSpec((1, n_heads, HEAD_DIM, tile), lambda i: (i // tps, 0, 0, i % tps))
    headsT_shape = jax.ShapeDtypeStruct((n // seq, n_heads, HEAD_DIM, seq), F32)
    return pl.pallas_call(
        kern,
        grid=(nt,),
        in_specs=[row(d), pl.BlockSpec(w_bf.shape, lambda i: (0, 0)), pl.BlockSpec(w_conv.shape, lambda i: (0, 0))],
        out_specs=[row(att_w), row(att_w), row(att_w), headsT, headsT,
                   pl.BlockSpec((1, tile // MOBA_BLOCK, att_w), lambda i: (i, 0, 0)),
                   row(mem_w), row(conv_w), pl.BlockSpec((1, SUBLANE, conv_w), lambda i: (i, 0, 0))],
        out_shape=[jax.ShapeDtypeStruct((n, att_w), BF16), jax.ShapeDtypeStruct((n, att_w), F32),
                   jax.ShapeDtypeStruct((n, att_w), F32), headsT_shape, headsT_shape,
                   jax.ShapeDtypeStruct((nt, tile // MOBA_BLOCK, att_w), F32),
                   jax.ShapeDtypeStruct((n, mem_w), BF16), jax.ShapeDtypeStruct((n, conv_w), F32),
                   jax.ShapeDtypeStruct((nt, SUBLANE, conv_w), F32)],
        scratch_shapes=[pltpu.VMEM((tile + SUBLANE, conv_w), F32)],
        compiler_params=_params("arbitrary"),
        name="inproj",
    )(x2d, w_bf, w_conv)


def _moba_kernel(slopes_ref, q_ref, k_ref, v_ref, km_ref, o_ref, kbf, vT, selb, m_s, l_s, acc_s, *, nblk):
    p = pl.program_id(1)
    i = pl.program_id(2)
    B = MOBA_BLOCK

    @pl.when(i == 0)
    def _():
        for c in range(nblk):
            kbf[c] = k_ref[c * B:(c + 1) * B, :].astype(BF16)
            vT[c] = v_ref[c * B:(c + 1) * B, :].T.astype(BF16)

    W = 2 * B
    q = q_ref[...]
    zero = jnp.zeros_like(q)
    q2 = jnp.concatenate([jnp.where(_head_mask(q.shape, 0), q, zero), jnp.where(_head_mask(q.shape, 1), q, zero)],
                         axis=0)
    second = lax.broadcasted_iota(I32, (1, W), 1) >= B
    slope = jnp.where(second, slopes_ref[2 * p + 1], slopes_ref[2 * p])
    key_i = lax.broadcasted_iota(I32, (B, W), 0)
    qry_i = lax.broadcasted_iota(I32, (B, W), 1) & (B - 1)
    base = slope * (key_i - qry_i).astype(F32)
    blk = lax.broadcasted_iota(I32, (nblk, W), 0)
    g = _dot_nt(km_ref[...], q2.astype(F32), precision=HI)
    g = jnp.where(blk < i, g, -jnp.inf)
    sb = jnp.full((nblk, W), NEG, F32)
    for _ in range(MOBA_TOPK):
        m = jnp.max(g, axis=0, keepdims=True)
        cand = (g == m) & (g > -jnp.inf)
        idx = jnp.min(jnp.where(cand, blk, nblk), axis=0, keepdims=True)
        pick = blk == idx
        sb = jnp.where(pick, 0.0, sb)
        g = jnp.where(pick, -jnp.inf, g)
    selb[...] = sb
    s = jnp.where(key_i <= qry_i, _dot_nt(kbf[i], q2) + base, NEG)
    m0 = jnp.max(s, axis=0, keepdims=True)
    p0 = jnp.exp(s - m0)
    m_s[...] = m0
    l_s[...] = jnp.sum(p0, axis=0, keepdims=True)
    acc_s[...] = _dot(vT[i], p0.astype(BF16))

    def scores(j):
        off = slope * (B * (i - j)).astype(F32)
        return _dot_nt(kbf[j], q2) + base + (selb[pl.ds(j, 1), :] - off)

    def update(js):
        ss = [scores(j) for j in js]
        m_old = m_s[...]
        m_new = m_old
        for s in ss:
            m_new = jnp.maximum(m_new, jnp.max(s, axis=0, keepdims=True))
        a = jnp.exp(m_old - m_new)
        l_new = a * l_s[...]
        acc = a * acc_s[...]
        for j, s in zip(js, ss):
            pj = jnp.exp(s - m_new)
            l_new = l_new + jnp.sum(pj, axis=0, keepdims=True)
            acc = acc + _dot(vT[j], pj.astype(BF16))
        l_s[...] = l_new
        acc_s[...] = acc
        m_s[...] = m_new

    def pair(jp, carry):
        update([2 * jp, 2 * jp + 1])
        return carry

    lax.fori_loop(0, i // 2, pair, 0)

    @pl.when(i % 2 == 1)
    def _():
        update([i - 1])
    o = acc_s[...] / l_s[...]
    sub = lax.broadcasted_iota(I32, (2 * HEAD_DIM, B), 0)
    o_ref[...] = jnp.where(sub < HEAD_DIM, o[:, :B], o[:, B:]).T


def _moba_prompt(q, k, v, kmean, slopes, *, batch, seq):
    n, att_w = q.shape
    nblk = seq // MOBA_BLOCK
    B = MOBA_BLOCK
    kern = functools.partial(_moba_kernel, nblk=nblk)
    return pl.pallas_call(
        kern,
        grid_spec=pltpu.PrefetchScalarGridSpec(
            num_scalar_prefetch=1,
            grid=(batch, att_w // LANE, nblk),
            in_specs=[pl.BlockSpec((B, LANE), lambda b, p, i, s: (b * nblk + i, p)),
                      pl.BlockSpec((seq, LANE), lambda b, p, i, s: (b, p)),
                      pl.BlockSpec((seq, LANE), lambda b, p, i, s: (b, p)),
                      pl.BlockSpec((nblk, LANE), lambda b, p, i, s: (b, p))],
            out_specs=pl.BlockSpec((B, LANE), lambda b, p, i, s: (b * nblk + i, p)),
            scratch_shapes=[pltpu.VMEM((nblk, B, LANE), BF16), pltpu.VMEM((nblk, LANE, B), BF16),
                            pltpu.VMEM((nblk, 2 * B), F32), pltpu.VMEM((1, 2 * B), F32),
                            pltpu.VMEM((1, 2 * B), F32), pltpu.VMEM((LANE, 2 * B), F32)]),
        out_shape=jax.ShapeDtypeStruct((n, att_w), F32),
        compiler_params=_params("arbitrary", "arbitrary", "arbitrary"),
        name="moba_prompt",
    )(slopes, q, k, v, kmean)


def _memkv_kernel(x_ref, w_ref, mk_ref, mv_ref, *, mem_w):
    r = _dot(x_ref[...].astype(BF16), w_ref[...])
    mk_ref[...] = r[:, :mem_w]
    mv_ref[...] = r[:, mem_w:]


def _memkv(mem2d, w_bf, *, tile, mem_w):
    n, d = mem2d.shape
    return pl.pallas_call(
        functools.partial(_memkv_kernel, mem_w=mem_w),
        grid=(n // tile,),
        in_specs=[pl.BlockSpec((tile, d), lambda i: (i, 0)), pl.BlockSpec(w_bf.shape, lambda i: (0, 0))],
        out_specs=[pl.BlockSpec((tile, mem_w), lambda i: (i, 0))] * 2,
        out_shape=[jax.ShapeDtypeStruct((n, mem_w), F32)] * 2,
        compiler_params=_params("arbitrary"),
        name="memkv",
    )(mem2d, w_bf)


def _memattn_kernel(qm_ref, mk_ref, mv_ref, o_ref, *, tq, mem_w):
    sub = lax.broadcasted_iota(I32, (LANE, tq), 0)
    for pr in range(mem_w // LANE):
        cs = slice(LANE * pr, LANE * (pr + 1))
        qp = qm_ref[:, cs]
        mkp = mk_ref[:, cs].astype(BF16)
        mvT = mv_ref[:, cs].T.astype(BF16)
        outs = []
        for hh in range(2):
            qh = jnp.where(_head_mask(qp.shape, hh), qp, jnp.zeros_like(qp))
            s = _dot_nt(mkp, qh)
            m = jnp.max(s, axis=0, keepdims=True)
            e = jnp.exp(s - m)
            l = jnp.sum(e, axis=0, keepdims=True)
            outs.append(_dot(mvT, e.astype(BF16)) / l)
        o_ref[:, cs] = jnp.where(sub < HEAD_DIM, outs[0], outs[1]).T


def _memattn_prompt(qm, mk, mv, *, batch, seq, n_mem, tq):
    n, mem_w = qm.shape
    nq = seq // tq
    return pl.pallas_call(
        functools.partial(_memattn_kernel, tq=tq, mem_w=mem_w),
        grid=(batch, nq),
        in_specs=[pl.BlockSpec((tq, mem_w), lambda b, i: (b * nq + i, 0)),
                  pl.BlockSpec((n_mem, mem_w), lambda b, i: (b, 0)),
                  pl.BlockSpec((n_mem, mem_w), lambda b, i: (b, 0))],
        out_specs=pl.BlockSpec((tq, mem_w), lambda b, i: (b * nq + i, 0)),
        out_shape=jax.ShapeDtypeStruct((n, mem_w), F32),
        compiler_params=_params("arbitrary", "arbitrary"),
        name="memattn_prompt",
    )(qm, mk, mv)


SORT_TILE = 256


def _sort_rows(ts, n_exp):
    return TOP_K * ts + SUBLANE * n_exp


def _finish1_kernel(x_ref, oa_ref, oc_ref, om_ref, gmix_ref, wout_ref, g1_ref, b1_ref, wrT_ref, br_ref,
                    x1_ref, xs_ref, route_ref, gw_ref, cpad_ref, loff_ref, cbase_ref, tot_ref, carry,
                    *, tile, ts, n_exp, alpha):
    i = pl.program_id(0)
    lt = _sort_rows(ts, n_exp)

    @pl.when(i == 0)
    def _():
        carry[...] = jnp.zeros_like(carry)

    mix = jnp.concatenate([_rms(oa_ref[...]), _rms(oc_ref[...]), _rms(om_ref[...])], axis=-1) * gmix_ref[...]
    if wout_ref.dtype == BF16:
        z = alpha * x_ref[...] + _dot(mix.astype(BF16), wout_ref[...])
    else:
        z = alpha * x_ref[...] + _dot(mix, wout_ref[...], precision=HI)
    x1 = _layernorm(z, g1_ref[...], b1_ref[...])
    x1_ref[...] = x1
    x1b = x1.astype(BF16)

    g = _dot_nt(wrT_ref[...], x1, precision=HI) + br_ref[...]
    eidx = lax.broadcasted_iota(I32, (n_exp, tile), 0)
    picks, vals = [], []
    for k in range(TOP_K):
        m = jnp.max(g, axis=0, keepdims=True)
        idx = jnp.min(jnp.where(g == m, eidx, n_exp), axis=0, keepdims=True)
        pick = eidx == idx
        route_ref[k:k + 1, :] = idx
        picks.append(pick)
        vals.append(m)
        g = jnp.where(pick, -jnp.inf, g)
    ex = [jnp.exp(v - vals[0]) for v in vals]
    denom = ex[0] + ex[1] + ex[2] + ex[3]
    for k in range(TOP_K):
        gw_ref[k:k + 1, :] = ex[k] / denom
    gw_ref[TOP_K:, :] = jnp.zeros((gw_ref.shape[0] - TOP_K, tile), F32)

    t_src = lax.broadcasted_iota(I32, (ts, ts), 0)
    t_dst = lax.broadcasted_iota(I32, (ts, ts), 1)
    before = jnp.where(t_src < t_dst, 1.0, 0.0).astype(BF16)
    e_src = lax.broadcasted_iota(I32, (n_exp, n_exp), 1)
    e_dst = lax.broadcasted_iota(I32, (n_exp, n_exp), 0)
    lower = jnp.where(e_src < e_dst, 1.0, 0.0).astype(BF16)
    slot = lax.broadcasted_iota(I32, (lt, ts), 0)
    for sub in range(tile // ts):
        cs = slice(sub * ts, (sub + 1) * ts)
        pk = [p[:, cs] for p in picks]
        onehot = jnp.zeros((n_exp, ts), F32)
        for p in pk:
            onehot = onehot + jnp.where(p, 1.0, 0.0)
        cnt = jnp.sum(onehot, axis=1, keepdims=True)
        cpad = jnp.floor((cnt + (SUBLANE - 1)) * (1.0 / SUBLANE)) * SUBLANE
        cpad_l = jnp.broadcast_to(cpad, (n_exp, LANE))
        loff_l = _dot(lower, cpad_l.astype(BF16))
        pos = _dot(onehot.astype(BF16), before) + loff_l[:, 0:1]
        perm = jnp.zeros((lt, ts), F32)
        for k in range(TOP_K):
            lpos = jnp.sum(jnp.where(pk[k], pos, 0.0), axis=0, keepdims=True).astype(I32)
            route_ref[TOP_K + k:TOP_K + k + 1, cs] = lpos
            perm = perm + jnp.where(slot == lpos, 1.0, 0.0)
        perm = perm.astype(BF16)
        xs_ref[sub * lt:(sub + 1) * lt, :] = _dot(perm, x1b[cs, :])
        cpad_ref[sub] = cpad_l
        loff_ref[sub] = loff_l
        cbase_ref[sub] = jnp.broadcast_to(carry[...], (n_exp, LANE))
        carry[...] = carry[...] + cpad
    tot_ref[...] = jnp.broadcast_to(carry[...], tot_ref.shape)


def _finish1(x2d, oa, oc, om, g_mix, wout_bf, ln_g, ln_b, w_rT, b_r, *, tile, ts, alpha):
    n, d = x2d.shape
    n_exp = w_rT.shape[0]
    lt = _sort_rows(ts, n_exp)
    spt = tile // ts
    kern = functools.partial(_finish1_kernel, tile=tile, ts=ts, n_exp=n_exp, alpha=alpha)
    row = lambda w: pl.BlockSpec((tile, w), lambda i: (i, 0))
    full = lambda a: pl.BlockSpec(a.shape, lambda i: (0,) * a.ndim)
    meta = pl.BlockSpec((spt, n_exp, LANE), lambda i: (i, 0, 0))
    meta_shape = jax.ShapeDtypeStruct((n // ts, n_exp, LANE), F32)
    return pl.pallas_call(
        kern,
        grid=(n // tile,),
        in_specs=[row(d), row(oa.shape[1]), row(oc.shape[1]), row(om.shape[1]), full(g_mix), full(wout_bf),
                  full(ln_g), full(ln_b), full(w_rT), full(b_r)],
        out_specs=[row(d), pl.BlockSpec((spt * lt, d), lambda i: (i, 0)),
                   pl.BlockSpec((2 * TOP_K, tile), lambda i: (0, i)), pl.BlockSpec((2 * TOP_K, tile), lambda i: (0, i)),
                   meta, meta, meta, pl.BlockSpec((n_exp, LANE), lambda i: (0, 0))],
        out_shape=[jax.ShapeDtypeStruct((n, d), F32), jax.ShapeDtypeStruct((n // ts * lt, d), F32),
                   jax.ShapeDtypeStruct((2 * TOP_K, n), I32), jax.ShapeDtypeStruct((2 * TOP_K, n), F32),
                   meta_shape, meta_shape, meta_shape, jax.ShapeDtypeStruct((n_exp, LANE), F32)],
        scratch_shapes=[pltpu.VMEM((n_exp, 1), F32)],
        compiler_params=_params("arbitrary"),
        name="finish1",
    )(x2d, oa, oc, om, g_mix, wout_bf, ln_g, ln_b, w_rT, b_r)


def _run_sizes(ts):
    sizes, s = [], SUBLANE
    while s <= max(ts, SUBLANE):
        sizes.append(s)
        s *= 2
    return sizes


def _for_each_piece(length, sizes, fn):
    off = 0
    for sz in sizes:
        @pl.when((length & sz) != 0)
        def _(off=off, sz=sz):
            fn(off, sz)
        off = off + (length & sz)


def _dispatch_kernel(cpad_ref, loff_ref, base_ref, fill_ref, xs_ref, zero_ref, buf_ref, sem, fill_sem,
                     *, lt, n_exp, sizes, tm):
    t = pl.program_id(0)

    def run(e, wait):
        n = cpad_ref[t * n_exp + e]
        src = loff_ref[t * n_exp + e]
        dst = base_ref[t * n_exp + e]

        def piece(off, sz):
            cp = pltpu.make_async_copy(xs_ref.at[pl.ds(pl.multiple_of(src + off, SUBLANE), sz), :],
                                       buf_ref.at[pl.ds(pl.multiple_of(dst + off, SUBLANE), sz), :], sem)
            cp.wait() if wait else cp.start()

        _for_each_piece(n, sizes, piece)

    def issue(e, c):
        run(e, False)
        return c

    def drain(e, c):
        run(e, True)
        return c

    lax.fori_loop(0, n_exp, issue, 0)

    @pl.when(t == pl.num_programs(0) - 1)
    def _():
        start = fill_ref[0]
        rest = buf_ref.shape[0] - start
        n_full = rest // tm

        def chunk(c, wait):
            cp = pltpu.make_async_copy(zero_ref, buf_ref.at[pl.ds(pl.multiple_of(start + c * tm, SUBLANE), tm), :],
                                       fill_sem)
            cp.wait() if wait else cp.start()

        def piece(wait):
            def fn(off, sz):
                cp = pltpu.make_async_copy(
                    zero_ref.at[pl.ds(0, sz), :],
                    buf_ref.at[pl.ds(pl.multiple_of(start + n_full * tm + off, SUBLANE), sz), :], fill_sem)
                cp.wait() if wait else cp.start()
            return fn

        tail_sizes = [s for s in _run_sizes(tm) if s < tm]
        for wait in (False, True):
            lax.fori_loop(0, n_full, lambda c, carry, wait=wait: (chunk(c, wait), carry)[1], 0)
            _for_each_piece(rest - n_full * tm, tail_sizes, piece(wait))

    lax.fori_loop(0, n_exp, drain, 0)


def _dispatch(cpad, loff, base, fill, xs, *, ts, n_exp, m_pad, tm):
    w = xs.shape[1]
    lt = _sort_rows(ts, n_exp)
    n_sub = xs.shape[0] // lt
    zero = jnp.zeros((tm, w), F32)
    return pl.pallas_call(
        functools.partial(_dispatch_kernel, lt=lt, n_exp=n_exp, sizes=_run_sizes(ts), tm=tm),
        grid_spec=pltpu.PrefetchScalarGridSpec(
            num_scalar_prefetch=4,
            grid=(n_sub,),
            in_specs=[pl.BlockSpec((lt, w), lambda t, *_: (t, 0)), pl.BlockSpec((tm, w), lambda t, *_: (0, 0))],
            out_specs=pl.BlockSpec(memory_space=pl.ANY),
            scratch_shapes=[pltpu.SemaphoreType.DMA(()), pltpu.SemaphoreType.DMA(())]),
        out_shape=jax.ShapeDtypeStruct((m_pad, w), F32),
        compiler_params=_params("arbitrary"),
        name="dispatch",
    )(cpad, loff, base, fill, xs, zero)


def _ffn_kernel(otile_ref, tile_ref, exp_ref, lo_ref, hi_ref, lhs_ref, wgu_ref, bgu_ref, wd_ref, bd_ref, out_ref,
                *, tm, d_ff):
    g = pl.program_id(0)
    lo = lo_ref[g]
    hi = hi_ref[g]

    @pl.when(hi == lo)
    def _():
        out_ref[...] = jnp.zeros_like(out_ref)

    @pl.when(hi > lo)
    def _():
        gu = _dot(lhs_ref[...].astype(BF16), wgu_ref[0]) + bgu_ref[0]
        gate = jnp.minimum(gu[:, :d_ff], SWIGLU_LIMIT)
        up = jnp.clip(gu[:, d_ff:], -SWIGLU_LIMIT, SWIGLU_LIMIT)
        hid = (up + 1.0) * gate * jax.nn.sigmoid(SWIGLU_ALPHA * gate)
        o = _dot(hid.astype(BF16), wd_ref[0]) + bd_ref[0]

        @pl.when(lo == 0)
        def _():
            out_ref[...] = o

        @pl.when(lo > 0)
        def _():
            rows = lax.broadcasted_iota(I32, (tm, 1), 0)
            out_ref[...] = jnp.where((rows >= lo) & (rows < hi), o, out_ref[...])


def _ffn(sched, buf, wgu_bf, b_gu, wd_bf, b_d, *, tm):
    m, w = buf.shape
    n_exp, d, d_ff2 = wgu_bf.shape
    out_tiles, tiles, experts, los, his = sched
    return pl.pallas_call(
        functools.partial(_ffn_kernel, tm=tm, d_ff=d_ff2 // 2),
        grid_spec=pltpu.PrefetchScalarGridSpec(
            num_scalar_prefetch=5,
            grid=(tiles.shape[0],),
            in_specs=[pl.BlockSpec((tm, w), lambda g, ot, t, e, lo, hi: (t[g], 0)),
                      pl.BlockSpec((1, d, d_ff2), lambda g, ot, t, e, lo, hi: (e[g], 0, 0)),
                      pl.BlockSpec((1, 1, d_ff2), lambda g, ot, t, e, lo, hi: (e[g], 0, 0)),
                      pl.BlockSpec((1, d_ff2 // 2, d), lambda g, ot, t, e, lo, hi: (e[g], 0, 0)),
                      pl.BlockSpec((1, 1, d), lambda g, ot, t, e, lo, hi: (e[g], 0, 0))],
            out_specs=pl.BlockSpec((tm, d), lambda g, ot, t, e, lo, hi: (ot[g], 0))),
        out_shape=jax.ShapeDtypeStruct((m, d), F32),
        compiler_params=_params("arbitrary"),
        name="expert_ffn",
    )(out_tiles, tiles, experts, los, his, buf, wgu_bf, b_gu, wd_bf, b_d)


def _moe_schedule(counts, n_rows, tm):
    n_exp = counts.shape[0]
    n_tiles = n_rows // tm
    n_items = n_tiles + n_exp - 1
    ends = jnp.cumsum(counts)
    starts = ends - counts
    first_tile = starts // tm
    n_it = jnp.where(counts > 0, (ends - 1) // tm - first_tile + 1, 0)
    it_end = jnp.cumsum(n_it)
    it_start = it_end - n_it
    item = jnp.arange(n_items, dtype=I32)
    n_real = it_end[-1]
    g = jnp.minimum(item, n_real - 1)
    e = jnp.minimum(jnp.sum((it_end[None, :] <= g[:, None]).astype(I32), axis=1), n_exp - 1)
    tile = (first_tile[e] + g - it_start[e]).astype(I32)
    valid = item < n_real
    lo = jnp.where(valid, jnp.clip(starts[e] - tile * tm, 0, tm), 0).astype(I32)
    hi = jnp.where(valid, jnp.clip(ends[e] - tile * tm, 0, tm), 0).astype(I32)
    used = (ends[-1] + tm - 1) // tm
    out_tile = jnp.where(valid, tile, jnp.minimum(used + item - n_real, n_tiles - 1)).astype(I32)
    return (out_tile, tile, e, lo, hi), starts.astype(I32)


def _combine_kernel(cpad_ref, loff_ref, base_ref, route_ref, gw_ref, x1_ref, g2_ref, b2_ref, eo_ref, y_ref,
                    gbuf, sem, *, ts, n_exp, sizes, alpha):
    t = pl.program_id(0)
    nt = pl.num_programs(0)
    lt = gbuf.shape[1]
    d = gbuf.shape[2]

    def tile_copies(tt, slot, wait):
        def run(e, c):
            n = cpad_ref[tt * n_exp + e]
            src = base_ref[tt * n_exp + e]
            dst = loff_ref[tt * n_exp + e]

            def piece(off, sz):
                cp = pltpu.make_async_copy(eo_ref.at[pl.ds(pl.multiple_of(src + off, SUBLANE), sz), :],
                                           gbuf.at[slot, pl.ds(pl.multiple_of(dst + off, SUBLANE), sz), :],
                                           sem.at[slot])
                cp.wait() if wait else cp.start()

            _for_each_piece(n, sizes, piece)
            return c

        lax.fori_loop(0, n_exp, run, 0)

    def fetch(tt, slot):
        gbuf[slot, TOP_K * ts:, :] = jnp.zeros((lt - TOP_K * ts, d), F32)
        tile_copies(tt, slot, False)

    slot = t % 2

    @pl.when(t == 0)
    def _():
        fetch(t, slot)

    @pl.when(t + 1 < nt)
    def _():
        fetch(t + 1, 1 - slot)

    tile_copies(t, slot, True)

    r = lax.broadcasted_iota(I32, (ts, ts), 0)
    c = lax.broadcasted_iota(I32, (ts, ts), 1)
    eye = jnp.where(r == c, 1.0, 0.0)
    wcol = _dot_nt(eye, gw_ref[...], precision=HI)
    pcol = _dot_nt(eye, route_ref[...].astype(F32), precision=HI)
    rows = gbuf[slot].astype(BF16)
    slot_i = lax.broadcasted_iota(I32, (ts, lt), 1)
    unsort = jnp.zeros((ts, lt), F32)
    for k in range(TOP_K):
        lpos = pcol[:, TOP_K + k:TOP_K + k + 1].astype(I32)
        unsort = unsort + jnp.where(slot_i == lpos, wcol[:, k:k + 1], 0.0)
    hi = unsort.astype(BF16)
    lo = (unsort - hi.astype(F32)).astype(BF16)
    moe = _dot(hi, rows) + _dot(lo, rows)
    y_ref[...] = _layernorm(alpha * x1_ref[...] + moe, g2_ref[...], b2_ref[...])


def _combine(cpad, loff, base, route, gw, x1, ln_g, ln_b, eo, *, ts, n_exp, alpha):
    n, d = x1.shape
    lt = _sort_rows(ts, n_exp)
    full = lambda a: pl.BlockSpec(a.shape, lambda i, *_: (0,) * a.ndim)
    return pl.pallas_call(
        functools.partial(_combine_kernel, ts=ts, n_exp=n_exp, sizes=_run_sizes(ts), alpha=alpha),
        grid_spec=pltpu.PrefetchScalarGridSpec(
            num_scalar_prefetch=3,
            grid=(n // ts,),
            in_specs=[pl.BlockSpec((2 * TOP_K, ts), lambda i, *_: (0, i)),
                      pl.BlockSpec((2 * TOP_K, ts), lambda i, *_: (0, i)),
                      pl.BlockSpec((ts, d), lambda i, *_: (i, 0)), full(ln_g), full(ln_b),
                      pl.BlockSpec(memory_space=pl.ANY)],
            out_specs=pl.BlockSpec((ts, d), lambda i, *_: (i, 0)),
            scratch_shapes=[pltpu.VMEM((2, lt, d), F32), pltpu.SemaphoreType.DMA((2,))]),
        out_shape=jax.ShapeDtypeStruct((n, d), F32),
        compiler_params=_params("arbitrary"),
        name="combine",
    )(cpad, loff, base, route, gw, x1, ln_g, ln_b, eo)


def _finish(x2d, oa, oc, om, wts, w_out, *, tile, tm, alpha):
    n = x2d.shape[0]
    n_exp = wts["w_rT"].shape[0]
    ts = _pick_tile(n, SORT_TILE)
    n_sub = n // ts
    x1, xs, route, gw, cpad, loff, cbase, tot = _finish1(
        x2d, oa, oc, om, wts["g_mix"], w_out, wts["ln1_g"], wts["ln1_b"], wts["w_rT"], wts["b_r"],
        tile=_pick_tile(n, tile), ts=ts, alpha=alpha)
    m_pad = n_sub * _sort_rows(ts, n_exp) + tm
    counts = tot[:, 0].astype(I32)
    sched, offs = _moe_schedule(counts, m_pad, tm)
    flat = lambda a: a[:, :, 0].astype(I32).reshape(-1)
    cpad, loff = flat(cpad), flat(loff)
    base = (cbase[:, :, 0].astype(I32) + offs[None, :]).reshape(-1)
    fill = jnp.sum(counts, keepdims=True)
    buf = _dispatch(cpad, loff, base, fill, xs, ts=ts, n_exp=n_exp, m_pad=m_pad, tm=tm)
    eo = _ffn(sched, buf, wts["w_gu"], wts["b_gu"], wts["w_d"], wts["b_d"], tm=tm)
    return _combine(cpad, loff, base, route, gw, x1, wts["ln2_g"], wts["ln2_b"], eo, ts=ts, n_exp=n_exp, alpha=alpha)


def _sample_inproj_kernel(x_ref, w_ref, wc_ref, p0_ref, p1_ref, q_ref, k_ref, v_ref, qm_ref, oc_ref, u_ref,
                          *, att_w, mem_w, conv_w):
    x = x_ref[...]
    c1, c2, c3 = att_w, 2 * att_w, 3 * att_w
    c4 = c3 + mem_w
    c5, c6, c7 = c4 + conv_w, c4 + 2 * conv_w, c4 + 3 * conv_w

    def proj(lo, hi):
        return _dot(x, w_ref[:, lo:hi], precision=HI)

    q_ref[...] = proj(0, c1) * Q_SCALE
    k_ref[...] = proj(c1, c2)
    v_ref[...] = proj(c2, c3)
    qm_ref[...] = proj(c3, c4) * Q_SCALE
    u = proj(c5, c6) * proj(c6, c7)
    cz = wc_ref[0:1, :] * p0_ref[...] + wc_ref[1:2, :] * p1_ref[...] + wc_ref[2:3, :] * u
    oc_ref[...] = proj(c4, c5) * cz
    u_ref[...] = u


def _sample_inproj(x2d, w_bf, w_conv, prev0, prev1, *, att_w, mem_w, conv_w):
    n = x2d.shape[0]
    args = (x2d, w_bf, w_conv, prev0, prev1)
    full = lambda a: pl.BlockSpec(a.shape, lambda i: (0,) * a.ndim)
    widths = (att_w, att_w, att_w, mem_w, conv_w, conv_w)
    return pl.pallas_call(
        functools.partial(_sample_inproj_kernel, att_w=att_w, mem_w=mem_w, conv_w=conv_w),
        grid=(1,),
        in_specs=[full(a) for a in args],
        out_specs=[pl.BlockSpec((n, w), lambda i: (0, 0)) for w in widths],
        out_shape=[jax.ShapeDtypeStruct((n, w), F32) for w in widths],
        compiler_params=_params("arbitrary"),
        name="sample_inproj",
    )(*args)


def _kscan_kernel(pt_ref, q_ref, ck_ref, sel_ref, kbuf, ksum, sem, *, n_pages, chunk, n_heads):
    b = pl.program_id(0)
    n_chunks = n_pages // chunk
    pages_per_blk = MOBA_BLOCK // PAGE_SIZE
    nblk = ksum.shape[0]
    q_col = q_ref[0]

    def copies(c, slot):
        return [pltpu.make_async_copy(ck_ref.at[0, pt_ref[b * n_pages + c * chunk + j]], kbuf.at[slot, j], sem.at[slot])
                for j in range(chunk)]

    for cp in copies(0, 0):
        cp.start()

    def body(c, carry):
        slot = c % 2

        @pl.when(c + 1 < n_chunks)
        def _():
            for cp in copies(c + 1, 1 - slot):
                cp.start()

        for cp in copies(c, slot):
            cp.wait()
        for jb in range(chunk // pages_per_blk):
            s = kbuf[slot, pages_per_blk * jb]
            for pg in range(1, pages_per_blk):
                s = s + kbuf[slot, pages_per_blk * jb + pg]
            ksum[c * (chunk // pages_per_blk) + jb] = jnp.sum(s * q_col, axis=1)
        return carry

    lax.fori_loop(0, n_chunks, body, 0)

    g = jnp.sum(ksum[...], axis=2, keepdims=True) * (1.0 / MOBA_BLOCK)
    blk = lax.broadcasted_iota(I32, g.shape, 0)
    sel_ref[...] = jnp.zeros(sel_ref.shape, I32)
    for r in range(MOBA_TOPK):
        m = jnp.max(g, axis=0, keepdims=True)
        idx = jnp.min(jnp.where((g == m) & (g > -jnp.inf), blk, nblk), axis=0, keepdims=True)
        sel_ref[0, :, r:r + 1] = idx[0]
        g = jnp.where(blk == idx, -jnp.inf, g)


def _kscan(pt_flat, q_col, ckT, *, n_pages, chunk):
    bs, n_heads, hd, _ = q_col.shape
    nblk = n_pages * PAGE_SIZE // MOBA_BLOCK
    return pl.pallas_call(
        functools.partial(_kscan_kernel, n_pages=n_pages, chunk=chunk, n_heads=n_heads),
        grid_spec=pltpu.PrefetchScalarGridSpec(
            num_scalar_prefetch=1,
            grid=(bs,),
            in_specs=[pl.BlockSpec((1, n_heads, hd, 1), lambda b, pt: (b, 0, 0, 0)), pl.BlockSpec(memory_space=pl.ANY)],
            out_specs=pl.BlockSpec((1, n_heads, LANE), lambda b, pt: (b, 0, 0)),
            scratch_shapes=[pltpu.VMEM((2, chunk, n_heads, hd, PAGE_SIZE), F32),
                            pltpu.VMEM((nblk, n_heads, PAGE_SIZE), F32), pltpu.SemaphoreType.DMA((2,))]),
        out_shape=jax.ShapeDtypeStruct((bs, n_heads, LANE), I32),
        compiler_params=_params("arbitrary"),
        name="kscan",
    )(pt_flat, q_col, ckT)


def _sample_attn_kernel(pt_ref, sel_ref, slopes_ref, q_ref, kn_ref, vn_ref, qm_ref, mk_ref, mv_ref, ck_ref, cv_ref,
                        oa_ref, om_ref, kbuf, vbuf, sem, *, n_pages, n_heads, past_len, mem_w):
    b = pl.program_id(0)
    nb = pl.num_programs(0)
    B = MOBA_BLOCK
    pages_per_blk = B // PAGE_SIZE
    n_keys = MOBA_TOPK * B

    def blocks(bb, h):
        return [sel_ref[(bb * n_heads + h) * MOBA_TOPK + s] for s in range(MOBA_TOPK)]

    n_sel_pages = MOBA_TOPK * pages_per_blk

    def copies(bb, slot):
        cps = []
        for h in range(n_heads):
            blks = blocks(bb, h)
            for s in range(MOBA_TOPK):
                for half in range(pages_per_blk):
                    pg = pt_ref[bb * n_pages + pages_per_blk * blks[s] + half]
                    pp = s * pages_per_blk + half
                    cps.append(pltpu.make_async_copy(ck_ref.at[0, pg, h], kbuf.at[slot, h, pp], sem.at[0, slot]))
                    cps.append(pltpu.make_async_copy(cv_ref.at[0, pg, h], vbuf.at[slot, h, pp], sem.at[1, slot]))
        return cps

    slot = b % 2

    @pl.when(b == 0)
    def _():
        for cp in copies(b, slot):
            cp.start()

    @pl.when(b + 1 < nb)
    def _():
        for cp in copies(b + 1, 1 - slot):
            cp.start()

    for cp in copies(b, slot):
        cp.wait()

    key_lane = lax.broadcasted_iota(I32, (1, n_keys), 1)
    lane = lax.broadcasted_iota(I32, (1, LANE), 1)
    for h in range(n_heads):
        blks = blocks(b, h)
        qh = q_ref[0, h:h + 1, :]
        q8 = jnp.broadcast_to(qh, (SUBLANE, HEAD_DIM))
        blk_of_key = jnp.where(key_lane < B, blks[0], jnp.where(key_lane < 2 * B, blks[1], blks[2]))
        dist = (past_len - blk_of_key * B - (key_lane & (B - 1))).astype(F32)
        s = jnp.concatenate([_dot(q8, kbuf[slot, h, pp], precision=HI) for pp in range(n_sel_pages)], axis=1)
        s = s - slopes_ref[h] * dist
        s_self = jnp.sum(qh * kn_ref[0, h:h + 1, :], axis=1, keepdims=True)
        m = jnp.maximum(jnp.max(s, axis=1, keepdims=True), s_self)
        e = jnp.exp(s - m)
        e_self = jnp.exp(s_self - m)
        l = jnp.sum(e, axis=1, keepdims=True) + e_self
        o = e_self * vn_ref[0, h:h + 1, :]
        for pp in range(n_sel_pages):
            o = o + _dot_nt(e[:, pp * PAGE_SIZE:(pp + 1) * PAGE_SIZE], vbuf[slot, h, pp], precision=HI)
        oa_ref[0, h:h + 1, :] = (o / l)[0:1, :]

    qm_row = qm_ref[pl.ds(b, 1), :]
    outs = []
    for pr in range(mem_w // LANE):
        cs = slice(pr * LANE, (pr + 1) * LANE)
        mkp = mk_ref[0, :, cs]
        mvp = mv_ref[0, :, cs]
        pair = []
        for hh in range(2):
            qh = jnp.where(_head_mask((1, LANE), hh), qm_row[:, cs], 0.0)
            s = _dot_nt(jnp.broadcast_to(qh, (SUBLANE, LANE)), mkp, precision=HI)
            e = jnp.exp(s - jnp.max(s, axis=1, keepdims=True))
            o = _dot(e, mvp, precision=HI) / jnp.sum(e, axis=1, keepdims=True)
            pair.append(o[0:1, :])
        outs.append(jnp.where(lane < HEAD_DIM, pair[0], pair[1]))
    om_ref[0] = jnp.concatenate(outs, axis=1)


def _sample_attn(pt_flat, sel_flat, slopes, q3, kn3, vn3, qm, mk, mv, ck, cv, *, n_pages):
    bs, n_heads, hd = q3.shape
    _, n_mem, mem_w = mk.shape
    n_keys = MOBA_TOPK * MOBA_BLOCK
    per_b = lambda a: pl.BlockSpec((1,) + a.shape[1:], lambda b, pt, sel: (b,) + (0,) * (a.ndim - 1))
    return pl.pallas_call(
        functools.partial(_sample_attn_kernel, n_pages=n_pages, n_heads=n_heads, past_len=n_pages * PAGE_SIZE,
                          mem_w=mem_w),
        grid_spec=pltpu.PrefetchScalarGridSpec(
            num_scalar_prefetch=2,
            grid=(bs,),
            in_specs=[pl.BlockSpec(memory_space=pltpu.SMEM), per_b(q3), per_b(kn3), per_b(vn3),
                      pl.BlockSpec(qm.shape, lambda b, pt, sel: (0, 0)), per_b(mk), per_b(mv),
                      pl.BlockSpec(memory_space=pl.ANY), pl.BlockSpec(memory_space=pl.ANY)],
            out_specs=[pl.BlockSpec((1, n_heads, hd), lambda b, pt, sel: (b, 0, 0)),
                       pl.BlockSpec((1, 1, mem_w), lambda b, pt, sel: (b, 0, 0))],
            scratch_shapes=[pltpu.VMEM((2, n_heads, n_keys // PAGE_SIZE, hd, PAGE_SIZE), F32),
                            pltpu.VMEM((2, n_heads, n_keys // PAGE_SIZE, hd, PAGE_SIZE), F32),
                            pltpu.SemaphoreType.DMA((2, 2))]),
        out_shape=[jax.ShapeDtypeStruct((bs, n_heads, hd), F32), jax.ShapeDtypeStruct((bs, 1, mem_w), F32)],
        compiler_params=_params("arbitrary"),
        name="sample_attn",
    )(pt_flat, sel_flat, slopes, q3, kn3, vn3, qm, mk, mv, ck, cv)


def _sample_layer(x, cache_k, cache_v, mem_k, mem_v, state, page_table, wts, dims, alpha):
    bs, dec_seq, d = x.shape
    assert dec_seq == 1
    att_w, mem_w, conv_w = dims
    n_heads = att_w // HEAD_DIM
    n_pages = page_table.shape[1]
    assert (n_pages * PAGE_SIZE) % MOBA_BLOCK == 0 and n_pages * PAGE_SIZE // MOBA_BLOCK >= MOBA_TOPK
    x2d = x.reshape(bs, d)
    q, k, v, qm, oc, u = _sample_inproj(x2d, wts["w_in_f32"], wts["w_conv"], state[:, 0, :], state[:, 1, :],
                                        att_w=att_w, mem_w=mem_w, conv_w=conv_w)
    pt_flat = page_table.reshape(-1)
    heads = lambda a: a.reshape(bs, n_heads, HEAD_DIM)
    pagesT = lambda c: jnp.transpose(c, (0, 1, 3, 4, 2))
    sel = _kscan(pt_flat, q.reshape(bs, n_heads, HEAD_DIM, 1), pagesT(cache_k), n_pages=n_pages, chunk=8)
    sel_flat = sel[:, :, :MOBA_TOPK].reshape(-1)
    oa, om = _sample_attn(pt_flat, sel_flat, wts["slopes"], heads(q), heads(k), heads(v), qm,
                          mem_k.reshape(bs, -1, mem_w), mem_v.reshape(bs, -1, mem_w), pagesT(cache_k),
                          pagesT(cache_v), n_pages=n_pages)
    y = _finish(x2d, oa.reshape(bs, att_w), oc, om.reshape(bs, mem_w), wts, wts["w_out_f32"], tile=bs,
                tm=bs * TOP_K, alpha=alpha)
    conv_state = jnp.stack([state[:, 1, :], u], axis=1)
    return y.reshape(bs, 1, d), k, v, conv_state


def _pick_tile(n, pref):
    return pref if n % pref == 0 else n


def _prompt_layer(x, mem, wts, dims, alpha):
    batch, seq, d = x.shape
    att_w, mem_w, conv_w = dims
    n = batch * seq
    n_mem = mem.shape[1]
    x2d = x.reshape(n, d)
    tile = 512
    q, k, v, kT, vT, kmean, qm, oc, tail = _inproj(x2d, wts["w_in"], wts["w_conv"], seq=seq, tile=tile, att_w=att_w,
                                                   mem_w=mem_w, conv_w=conv_w)
    oa = _moba_prompt(q, k, v, kmean.reshape(n // MOBA_BLOCK, att_w), wts["slopes"], batch=batch, seq=seq)
    mk, mv = _memkv(mem.reshape(batch * n_mem, d), wts["w_mem_kv"], tile=_pick_tile(batch * n_mem, 512), mem_w=mem_w)
    om = _memattn_prompt(qm, mk, mv, batch=batch, seq=seq, n_mem=n_mem, tq=512)
    y = _finish(x2d, oa, oc, om, wts, wts["w_out"], tile=512, tm=256, alpha=alpha)
    conv_state = tail.reshape(batch, seq // tile, SUBLANE, conv_w)[:, -1, SUBLANE - (CONV_K - 1):, :]
    rows = lambda t: jnp.transpose(t, (0, 3, 1, 2))
    return y.reshape(batch, seq, d), rows(kT), rows(vT), conv_state, mk, mv


def _prep_weights(l, w_in, w_mem_kv, w_conv, g_mix, w_out, ln1_g, ln1_b, w_router, b_router, w_gate_up, b_gate_up,
                  w_down, b_down, ln2_g, ln2_b):
    n_heads = 8
    row = lambda a: a[l][None, :]
    return {
        "w_in": w_in[l].astype(BF16), "w_in_f32": w_in[l], "w_mem_kv": w_mem_kv[l].astype(BF16), "w_conv": w_conv[l],
        "g_mix": row(g_mix), "w_out": w_out[l].astype(BF16), "w_out_f32": w_out[l], "ln1_g": row(ln1_g),
        "ln1_b": row(ln1_b),
        "w_rT": w_router[l].T, "b_r": b_router[l][:, None],
        "w_gu": w_gate_up[l].astype(BF16), "b_gu": b_gate_up[l][:, None, :],
        "w_d": w_down[l].astype(BF16), "b_d": b_down[l][:, None, :],
        "ln2_g": row(ln2_g), "ln2_b": row(ln2_b),
        "slopes": 2.0 ** (-8.0 * jnp.arange(1, n_heads + 1, dtype=F32) / n_heads),
    }


def kernel(x_prompt, x_sample, cache_k, cache_v, cache_mem_k, cache_mem_v, state_conv, page_table, mem_prompt, w_in, w_mem_kv, w_conv, g_mix, w_out, ln1_g, ln1_b, w_router, b_router, w_gate_up, b_gate_up, w_down, b_down, ln2_g, ln2_b):
    depth = w_in.shape[0]
    assert depth == 1
    alpha = (2 * depth) ** 0.25
    n_heads = cache_k.shape[3]
    att_w = n_heads * HEAD_DIM
    mem_w = cache_mem_k.shape[3] * HEAD_DIM
    conv_w = state_conv.shape[3]
    dims = (att_w, mem_w, conv_w)
    wts = _prep_weights(0, w_in, w_mem_kv, w_conv, g_mix, w_out, ln1_g, ln1_b, w_router, b_router, w_gate_up,
                        b_gate_up, w_down, b_down, ln2_g, ln2_b)
    bp, seq, d = x_prompt.shape
    n_mem = mem_prompt.shape[1]
    y_p, k_p, v_p, conv_p, mk_p, mv_p = _prompt_layer(x_prompt, mem_prompt, wts, dims, alpha)
    bs = x_sample.shape[0]
    y_s, k_s, v_s, conv_s = _sample_layer(x_sample, cache_k, cache_v, cache_mem_k[0], cache_mem_v[0],
                                          state_conv[0], page_table, wts, dims, alpha)
    return (y_p, y_s,
            k_p[None], v_p[None], conv_p[None], mk_p.reshape(1, bp, n_mem, mem_w // HEAD_DIM, HEAD_DIM),
            mv_p.reshape(1, bp, n_mem, mem_w // HEAD_DIM, HEAD_DIM),
            k_s.reshape(1, bs, 1, n_heads, HEAD_DIM), v_s.reshape(1, bs, 1, n_heads, HEAD_DIM), conv_s[None])
```

```python
import functools

import jax
import jax.numpy as jnp
from jax import lax
from jax.experimental import pallas as pl
from jax.experimental.pallas import tpu as pltpu

F32 = jnp.float32
BF16 = jnp.bfloat16
I32 = jnp.int32

HEAD_DIM = 64
MOBA_BLOCK = 256
MOBA_TOPK = 3
PAGE_SIZE = 128
TOP_K = 4
CONV_K = 3
SWIGLU_LIMIT = 7.0
SWIGLU_ALPHA = 1.702
LN_EPS = 1e-5
Q_SCALE = HEAD_DIM ** -0.5
NEG = -1e30
LANE = 128
SUBLANE = 8
VMEM_LIMIT = 56 * 1024 * 1024
HI = lax.Precision.HIGHEST


def _params(*sem):
    return pltpu.CompilerParams(dimension_semantics=sem, vmem_limit_bytes=VMEM_LIMIT)


def _dot_nt(a, b, precision=None):
    return lax.dot_general(a, b, (((1,), (1,)), ((), ())), precision=precision, preferred_element_type=F32)


def _dot(a, b, precision=None):
    return jnp.dot(a, b, precision=precision, preferred_element_type=F32)


def _layernorm(z, g, b):
    zc = z - jnp.mean(z, axis=-1, keepdims=True)
    var = jnp.mean(zc * zc, axis=-1, keepdims=True)
    return zc * lax.rsqrt(var + LN_EPS) * g + b


def _rms(a):
    return a * lax.rsqrt(jnp.mean(a * a, axis=-1, keepdims=True) + LN_EPS)


def _head_mask(shape, hh):
    lane = lax.broadcasted_iota(I32, shape, len(shape) - 1)
    return (lane >= HEAD_DIM * hh) & (lane < HEAD_DIM * (hh + 1))


def _inproj_kernel(x_ref, w_ref, wc_ref, q_ref, k_ref, v_ref, kt_ref, vt_ref, km_ref, qm_ref, oc_ref, tail_ref, ubuf,
                   *, tile, tiles_per_seq, att_w, mem_w, conv_w):
    i = pl.program_id(0)
    x = x_ref[...].astype(BF16)
    c1, c2, c3 = att_w, 2 * att_w, 3 * att_w
    c4 = c3 + mem_w
    c5, c6, c7 = c4 + conv_w, c4 + 2 * conv_w, c4 + 3 * conv_w
    n_heads = att_w // HEAD_DIM

    def proj(lo, hi):
        return _dot(x, w_ref[:, lo:hi])

    q_ref[...] = (proj(0, c1) * Q_SCALE).astype(BF16)
    k = proj(c1, c2)
    k_ref[...] = k
    kt_ref[0] = k.T.reshape(n_heads, HEAD_DIM, tile)
    km_ref[0] = jnp.sum(k.reshape(tile // MOBA_BLOCK, MOBA_BLOCK, att_w), axis=1) * (1.0 / MOBA_BLOCK)
    v = proj(c2, c3)
    v_ref[...] = v
    vt_ref[0] = v.T.reshape(n_heads, HEAD_DIM, tile)
    qm_ref[...] = (proj(c3, c4) * Q_SCALE).astype(BF16)
    gb = proj(c4, c5)
    u = proj(c5, c6) * proj(c6, c7)

    @pl.when(i % tiles_per_seq == 0)
    def _():
        ubuf[0:SUBLANE, :] = jnp.zeros((SUBLANE, conv_w), F32)

    ubuf[SUBLANE:SUBLANE + tile, :] = u
    u1 = ubuf[SUBLANE - 1:SUBLANE - 1 + tile, :]
    u2 = ubuf[SUBLANE - 2:SUBLANE - 2 + tile, :]
    cz = wc_ref[0:1, :] * u2 + wc_ref[1:2, :] * u1 + wc_ref[2:3, :] * u
    oc_ref[...] = gb * cz
    tail = ubuf[tile:tile + SUBLANE, :]
    tail_ref[0] = tail
    ubuf[0:SUBLANE, :] = tail


def _inproj(x2d, w_bf, w_conv, *, seq, tile, att_w, mem_w, conv_w):
    n, d = x2d.shape
    nt = n // tile
    kern = functools.partial(_inproj_kernel, tile=tile, tiles_per_seq=seq // tile, att_w=att_w, mem_w=mem_w,
                             conv_w=conv_w)
    row = lambda w: pl.BlockSpec((tile, w), lambda i: (i, 0))
    tps = seq // tile
    n_heads = att_w // HEAD_DIM
    headsT = pl.BlockSpec((1, n_heads, HEAD_DIM, tile), lambda i: (i // tps, 0, 0, i % tps))
    headsT_shape = jax.ShapeDtypeStruct((n // seq, n_heads, HEAD_DIM, seq), F32)
    return pl.pallas_call(
        kern,
        grid=(nt,),
        in_specs=[row(d), pl.BlockSpec(w_bf.shape, lambda i: (0, 0)), pl.BlockSpec(w_conv.shape, lambda i: (0, 0))],
        out_specs=[row(att_w), row(att_w), row(att_w), headsT, headsT,
                   pl.BlockSpec((1, tile // MOBA_BLOCK, att_w), lambda i: (i, 0, 0)),
                   row(mem_w), row(conv_w), pl.BlockSpec((1, SUBLANE, conv_w), lambda i: (i, 0, 0))],
        out_shape=[jax.ShapeDtypeStruct((n, att_w), BF16), jax.ShapeDtypeStruct((n, att_w), F32),
                   jax.ShapeDtypeStruct((n, att_w), F32), headsT_shape, headsT_shape,
                   jax.ShapeDtypeStruct((nt, tile // MOBA_BLOCK, att_w), F32),
                   jax.ShapeDtypeStruct((n, mem_w), BF16), jax.ShapeDtypeStruct((n, conv_w), F32),
                   jax.ShapeDtypeStruct((nt, SUBLANE, conv_w), F32)],
        scratch_shapes=[pltpu.VMEM((tile + SUBLANE, conv_w), F32)],
        compiler_params=_params("arbitrary"),
        name="inproj",
    )(x2d, w_bf, w_conv)


def _moba_kernel(slopes_ref, q_ref, k_ref, v_ref, km_ref, o_ref, kbf, vT, selb, m_s, l_s, acc_s, *, nblk):
    p = pl.program_id(1)
    B = MOBA_BLOCK

    for c in range(nblk):
        kbf[c] = k_ref[c * B:(c + 1) * B, :].astype(BF16)
        vT[c] = v_ref[c * B:(c + 1) * B, :].T.astype(BF16)

    W = 2 * B
    second = lax.broadcasted_iota(I32, (1, W), 1) >= B
    slope = jnp.where(second, slopes_ref[2 * p + 1], slopes_ref[2 * p])
    key_i = lax.broadcasted_iota(I32, (B, W), 0)
    qry_i = lax.broadcasted_iota(I32, (B, W), 1) & (B - 1)
    base = slope * (key_i - qry_i).astype(F32)
    blk = lax.broadcasted_iota(I32, (nblk, W), 0)
    sub = lax.broadcasted_iota(I32, (2 * HEAD_DIM, B), 0)
    km = km_ref[...]

    def query_tile(i, carry):
        rows = pl.ds(i * B if isinstance(i, int) else pl.multiple_of(i * B, B), B)
        q = q_ref[rows, :]
        zero = jnp.zeros_like(q)
        q2 = jnp.concatenate([jnp.where(_head_mask(q.shape, 0), q, zero),
                              jnp.where(_head_mask(q.shape, 1), q, zero)], axis=0)
        g = _dot_nt(km, q2.astype(F32), precision=HI)
        g = jnp.where(blk < i, g, -jnp.inf)
        sb = jnp.full((nblk, W), NEG, F32)
        for _ in range(MOBA_TOPK):
            m = jnp.max(g, axis=0, keepdims=True)
            cand = (g == m) & (g > -jnp.inf)
            idx = jnp.min(jnp.where(cand, blk, nblk), axis=0, keepdims=True)
            pick = blk == idx
            sb = jnp.where(pick, 0.0, sb)
            g = jnp.where(pick, -jnp.inf, g)
        selb[...] = sb
        s = jnp.where(key_i <= qry_i, _dot_nt(kbf[i], q2) + base, NEG)
        m0 = jnp.max(s, axis=0, keepdims=True)
        p0 = jnp.exp(s - m0)
        m_s[...] = m0
        l_s[...] = jnp.sum(p0, axis=0, keepdims=True)
        acc_s[...] = _dot(vT[i], p0.astype(BF16))

        def scores(j):
            off = slope * jnp.asarray(B * (i - j), F32)
            return _dot_nt(kbf[j], q2) + base + (selb[pl.ds(j, 1), :] - off)

        def update(js):
            ss = [scores(j) for j in js]
            m_old = m_s[...]
            m_new = m_old
            for s in ss:
                m_new = jnp.maximum(m_new, jnp.max(s, axis=0, keepdims=True))
            a = jnp.exp(m_old - m_new)
            l_new = a * l_s[...]
            acc = a * acc_s[...]
            for j, s in zip(js, ss):
                pj = jnp.exp(s - m_new)
                l_new = l_new + jnp.sum(pj, axis=0, keepdims=True)
                acc = acc + _dot(vT[j], pj.astype(BF16))
            l_s[...] = l_new
            acc_s[...] = acc
            m_s[...] = m_new

        def pair(jp, c2):
            update([2 * jp, 2 * jp + 1])
            return c2

        lax.fori_loop(0, i // 2, pair, 0)

        @pl.when(i % 2 == 1)
        def _():
            update([i - 1])

        o = acc_s[...] / l_s[...]
        o_ref[rows, :] = jnp.where(sub < HEAD_DIM, o[:, :B], o[:, B:]).T
        return carry

    lax.fori_loop(0, nblk, query_tile, 0)


def _moba_prompt(q, k, v, kmean, slopes, *, batch, seq):
    n, att_w = q.shape
    nblk = seq // MOBA_BLOCK
    B = MOBA_BLOCK
    kern = functools.partial(_moba_kernel, nblk=nblk)
    return pl.pallas_call(
        kern,
        grid_spec=pltpu.PrefetchScalarGridSpec(
            num_scalar_prefetch=1,
            grid=(batch, att_w // LANE),
            in_specs=[pl.BlockSpec((seq, LANE), lambda b, p, s: (b, p)),
                      pl.BlockSpec((seq, LANE), lambda b, p, s: (b, p)),
                      pl.BlockSpec((seq, LANE), lambda b, p, s: (b, p)),
                      pl.BlockSpec((nblk, LANE), lambda b, p, s: (b, p))],
            out_specs=pl.BlockSpec((seq, LANE), lambda b, p, s: (b, p)),
            scratch_shapes=[pltpu.VMEM((nblk, B, LANE), BF16), pltpu.VMEM((nblk, LANE, B), BF16),
                            pltpu.VMEM((nblk, 2 * B), F32), pltpu.VMEM((1, 2 * B), F32),
                            pltpu.VMEM((1, 2 * B), F32), pltpu.VMEM((LANE, 2 * B), F32)]),
        out_shape=jax.ShapeDtypeStruct((n, att_w), F32),
        compiler_params=_params("arbitrary", "arbitrary"),
        name="moba_prompt",
    )(slopes, q, k, v, kmean)


def _memkv_kernel(x_ref, w_ref, mk_ref, mv_ref, *, mem_w):
    r = _dot(x_ref[...].astype(BF16), w_ref[...])
    mk_ref[...] = r[:, :mem_w]
    mv_ref[...] = r[:, mem_w:]


def _memkv(mem2d, w_bf, *, tile, mem_w):
    n, d = mem2d.shape
    return pl.pallas_call(
        functools.partial(_memkv_kernel, mem_w=mem_w),
        grid=(n // tile,),
        in_specs=[pl.BlockSpec((tile, d), lambda i: (i, 0)), pl.BlockSpec(w_bf.shape, lambda i: (0, 0))],
        out_specs=[pl.BlockSpec((tile, mem_w), lambda i: (i, 0))] * 2,
        out_shape=[jax.ShapeDtypeStruct((n, mem_w), F32)] * 2,
        compiler_params=_params("arbitrary"),
        name="memkv",
    )(mem2d, w_bf)


def _memattn_kernel(qm_ref, mk_ref, mv_ref, o_ref, *, tq, mem_w):
    sub = lax.broadcasted_iota(I32, (LANE, tq), 0)
    for pr in range(mem_w // LANE):
        cs = slice(LANE * pr, LANE * (pr + 1))
        qp = qm_ref[:, cs]
        mkp = mk_ref[:, cs].astype(BF16)
        mvT = mv_ref[:, cs].T.astype(BF16)
        outs = []
        for hh in range(2):
            qh = jnp.where(_head_mask(qp.shape, hh), qp, jnp.zeros_like(qp))
            s = _dot_nt(mkp, qh)
            m = jnp.max(s, axis=0, keepdims=True)
            e = jnp.exp(s - m)
            l = jnp.sum(e, axis=0, keepdims=True)
            outs.append(_dot(mvT, e.astype(BF16)) / l)
        o_ref[:, cs] = jnp.where(sub < HEAD_DIM, outs[0], outs[1]).T


def _memattn_prompt(qm, mk, mv, *, batch, seq, n_mem, tq):
    n, mem_w = qm.shape
    nq = seq // tq
    return pl.pallas_call(
        functools.partial(_memattn_kernel, tq=tq, mem_w=mem_w),
        grid=(batch, nq),
        in_specs=[pl.BlockSpec((tq, mem_w), lambda b, i: (b * nq + i, 0)),
                  pl.BlockSpec((n_mem, mem_w), lambda b, i: (b, 0)),
                  pl.BlockSpec((n_mem, mem_w), lambda b, i: (b, 0))],
        out_specs=pl.BlockSpec((tq, mem_w), lambda b, i: (b * nq + i, 0)),
        out_shape=jax.ShapeDtypeStruct((n, mem_w), F32),
        compiler_params=_params("arbitrary", "arbitrary"),
        name="memattn_prompt",
    )(qm, mk, mv)


SORT_TILE = 256


def _sort_rows(ts, n_exp):
    return TOP_K * ts + SUBLANE * n_exp


def _finish1_kernel(x_ref, oa_ref, oc_ref, om_ref, gmix_ref, wout_ref, g1_ref, b1_ref, wrT_ref, br_ref,
                    x1_ref, xs_ref, route_ref, gw_ref, cpad_ref, loff_ref, cbase_ref, tot_ref, carry,
                    *, tile, ts, n_exp, alpha):
    i = pl.program_id(0)
    lt = _sort_rows(ts, n_exp)

    @pl.when(i == 0)
    def _():
        carry[...] = jnp.zeros_like(carry)

    mix = jnp.concatenate([_rms(oa_ref[...]), _rms(oc_ref[...]), _rms(om_ref[...])], axis=-1) * gmix_ref[...]
    if wout_ref.dtype == BF16:
        z = alpha * x_ref[...] + _dot(mix.astype(BF16), wout_ref[...])
    else:
        z = alpha * x_ref[...] + _dot(mix, wout_ref[...], precision=HI)
    x1 = _layernorm(z, g1_ref[...], b1_ref[...])
    x1_ref[...] = x1
    x1b = x1.astype(BF16)

    g = _dot_nt(wrT_ref[...], x1, precision=HI) + br_ref[...]
    eidx = lax.broadcasted_iota(I32, (n_exp, tile), 0)
    picks, vals = [], []
    for k in range(TOP_K):
        m = jnp.max(g, axis=0, keepdims=True)
        idx = jnp.min(jnp.where(g == m, eidx, n_exp), axis=0, keepdims=True)
        pick = eidx == idx
        route_ref[k:k + 1, :] = idx
        picks.append(pick)
        vals.append(m)
        g = jnp.where(pick, -jnp.inf, g)
    ex = [jnp.exp(v - vals[0]) for v in vals]
    denom = ex[0] + ex[1] + ex[2] + ex[3]
    for k in range(TOP_K):
        gw_ref[k:k + 1, :] = ex[k] / denom
    gw_ref[TOP_K:, :] = jnp.zeros((gw_ref.shape[0] - TOP_K, tile), F32)

    t_src = lax.broadcasted_iota(I32, (ts, ts), 0)
    t_dst = lax.broadcasted_iota(I32, (ts, ts), 1)
    before = jnp.where(t_src < t_dst, 1.0, 0.0).astype(BF16)
    e_src = lax.broadcasted_iota(I32, (n_exp, n_exp), 1)
    e_dst = lax.broadcasted_iota(I32, (n_exp, n_exp), 0)
    lower = jnp.where(e_src < e_dst, 1.0, 0.0).astype(BF16)
    slot = lax.broadcasted_iota(I32, (lt, ts), 0)
    for sub in range(tile // ts):
        cs = slice(sub * ts, (sub + 1) * ts)
        pk = [p[:, cs] for p in picks]
        onehot = jnp.zeros((n_exp, ts), F32)
        for p in pk:
            onehot = onehot + jnp.where(p, 1.0, 0.0)
        cnt = jnp.sum(onehot, axis=1, keepdims=True)
        cpad = jnp.floor((cnt + (SUBLANE - 1)) * (1.0 / SUBLANE)) * SUBLANE
        cpad_l = jnp.broadcast_to(cpad, (n_exp, LANE))
        loff_l = _dot(lower, cpad_l.astype(BF16))
        pos = _dot(onehot.astype(BF16), before) + loff_l[:, 0:1]
        perm = jnp.zeros((lt, ts), F32)
        for k in range(TOP_K):
            lpos = jnp.sum(jnp.where(pk[k], pos, 0.0), axis=0, keepdims=True).astype(I32)
            route_ref[TOP_K + k:TOP_K + k + 1, cs] = lpos
            perm = perm + jnp.where(slot == lpos, 1.0, 0.0)
        perm = perm.astype(BF16)
        xs_ref[sub * lt:(sub + 1) * lt, :] = _dot(perm, x1b[cs, :])
        cpad_ref[sub] = cpad_l
        loff_ref[sub] = loff_l
        cbase_ref[sub] = jnp.broadcast_to(carry[...], (n_exp, LANE))
        carry[...] = carry[...] + cpad
    tot_ref[...] = jnp.broadcast_to(carry[...], tot_ref.shape)


def _finish1(x2d, oa, oc, om, g_mix, wout_bf, ln_g, ln_b, w_rT, b_r, *, tile, ts, alpha):
    n, d = x2d.shape
    n_exp = w_rT.shape[0]
    lt = _sort_rows(ts, n_exp)
    spt = tile // ts
    kern = functools.partial(_finish1_kernel, tile=tile, ts=ts, n_exp=n_exp, alpha=alpha)
    row = lambda w: pl.BlockSpec((tile, w), lambda i: (i, 0))
    full = lambda a: pl.BlockSpec(a.shape, lambda i: (0,) * a.ndim)
    meta = pl.BlockSpec((spt, n_exp, LANE), lambda i: (i, 0, 0))
    meta_shape = jax.ShapeDtypeStruct((n // ts, n_exp, LANE), F32)
    return pl.pallas_call(
        kern,
        grid=(n // tile,),
        in_specs=[row(d), row(oa.shape[1]), row(oc.shape[1]), row(om.shape[1]), full(g_mix), full(wout_bf),
                  full(ln_g), full(ln_b), full(w_rT), full(b_r)],
        out_specs=[row(d), pl.BlockSpec((spt * lt, d), lambda i: (i, 0)),
                   pl.BlockSpec((2 * TOP_K, tile), lambda i: (0, i)), pl.BlockSpec((2 * TOP_K, tile), lambda i: (0, i)),
                   meta, meta, meta, pl.BlockSpec((n_exp, LANE), lambda i: (0, 0))],
        out_shape=[jax.ShapeDtypeStruct((n, d), F32), jax.ShapeDtypeStruct((n // ts * lt, d), F32),
                   jax.ShapeDtypeStruct((2 * TOP_K, n), I32), jax.ShapeDtypeStruct((2 * TOP_K, n), F32),
                   meta_shape, meta_shape, meta_shape, jax.ShapeDtypeStruct((n_exp, LANE), F32)],
        scratch_shapes=[pltpu.VMEM((n_exp, 1), F32)],
        compiler_params=_params("arbitrary"),
        name="finish1",
    )(x2d, oa, oc, om, g_mix, wout_bf, ln_g, ln_b, w_rT, b_r)


def _run_sizes(ts):
    sizes, s = [], SUBLANE
    while s <= max(ts, SUBLANE):
        sizes.append(s)
        s *= 2
    return sizes


def _for_each_piece(length, sizes, fn):
    off = 0
    for sz in sizes:
        @pl.when((length & sz) != 0)
        def _(off=off, sz=sz):
            fn(off, sz)
        off = off + (length & sz)


def _dispatch_kernel(cpad_ref, loff_ref, base_ref, fill_ref, xs_ref, zero_ref, buf_ref, sem, fill_sem,
                     *, lt, n_exp, sizes, tm):
    t = pl.program_id(0)

    def run(e, wait):
        n = cpad_ref[t * n_exp + e]
        src = loff_ref[t * n_exp + e]
        dst = base_ref[t * n_exp + e]

        def piece(off, sz):
            cp = pltpu.make_async_copy(xs_ref.at[pl.ds(pl.multiple_of(src + off, SUBLANE), sz), :],
                                       buf_ref.at[pl.ds(pl.multiple_of(dst + off, SUBLANE), sz), :], sem)
            cp.wait() if wait else cp.start()

        _for_each_piece(n, sizes, piece)

    def issue(e, c):
        run(e, False)
        return c

    def drain(e, c):
        run(e, True)
        return c

    lax.fori_loop(0, n_exp, issue, 0)

    @pl.when(t == pl.num_programs(0) - 1)
    def _():
        start = fill_ref[0]
        rest = buf_ref.shape[0] - start
        n_full = rest // tm

        def chunk(c, wait):
            cp = pltpu.make_async_copy(zero_ref, buf_ref.at[pl.ds(pl.multiple_of(start + c * tm, SUBLANE), tm), :],
                                       fill_sem)
            cp.wait() if wait else cp.start()

        def piece(wait):
            def fn(off, sz):
                cp = pltpu.make_async_copy(
                    zero_ref.at[pl.ds(0, sz), :],
                    buf_ref.at[pl.ds(pl.multiple_of(start + n_full * tm + off, SUBLANE), sz), :], fill_sem)
                cp.wait() if wait else cp.start()
            return fn

        tail_sizes = [s for s in _run_sizes(tm) if s < tm]
        for wait in (False, True):
            lax.fori_loop(0, n_full, lambda c, carry, wait=wait: (chunk(c, wait), carry)[1], 0)
            _for_each_piece(rest - n_full * tm, tail_sizes, piece(wait))

    lax.fori_loop(0, n_exp, drain, 0)


def _dispatch(cpad, loff, base, fill, xs, *, ts, n_exp, m_pad, tm):
    w = xs.shape[1]
    lt = _sort_rows(ts, n_exp)
    n_sub = xs.shape[0] // lt
    zero = jnp.zeros((tm, w), F32)
    return pl.pallas_call(
        functools.partial(_dispatch_kernel, lt=lt, n_exp=n_exp, sizes=_run_sizes(ts), tm=tm),
        grid_spec=pltpu.PrefetchScalarGridSpec(
            num_scalar_prefetch=4,
            grid=(n_sub,),
            in_specs=[pl.BlockSpec((lt, w), lambda t, *_: (t, 0)), pl.BlockSpec((tm, w), lambda t, *_: (0, 0))],
            out_specs=pl.BlockSpec(memory_space=pl.ANY),
            scratch_shapes=[pltpu.SemaphoreType.DMA(()), pltpu.SemaphoreType.DMA(())]),
        out_shape=jax.ShapeDtypeStruct((m_pad, w), F32),
        compiler_params=_params("arbitrary"),
        name="dispatch",
    )(cpad, loff, base, fill, xs, zero)


def _ffn_kernel(otile_ref, tile_ref, exp_ref, lo_ref, hi_ref, lhs_ref, wgu_ref, bgu_ref, wd_ref, bd_ref, out_ref,
                *, tm, d_ff):
    g = pl.program_id(0)
    lo = lo_ref[g]
    hi = hi_ref[g]

    @pl.when(hi == lo)
    def _():
        out_ref[...] = jnp.zeros_like(out_ref)

    @pl.when(hi > lo)
    def _():
        gu = _dot(lhs_ref[...].astype(BF16), wgu_ref[0]) + bgu_ref[0]
        gate = jnp.minimum(gu[:, :d_ff], SWIGLU_LIMIT)
        up = jnp.clip(gu[:, d_ff:], -SWIGLU_LIMIT, SWIGLU_LIMIT)
        hid = (up + 1.0) * gate * jax.nn.sigmoid(SWIGLU_ALPHA * gate)
        o = _dot(hid.astype(BF16), wd_ref[0]) + bd_ref[0]

        @pl.when(lo == 0)
        def _():
            out_ref[...] = o

        @pl.when(lo > 0)
        def _():
            rows = lax.broadcasted_iota(I32, (tm, 1), 0)
            out_ref[...] = jnp.where((rows >= lo) & (rows < hi), o, out_ref[...])


def _ffn(sched, buf, wgu_bf, b_gu, wd_bf, b_d, *, tm):
    m, w = buf.shape
    n_exp, d, d_ff2 = wgu_bf.shape
    out_tiles, tiles, experts, los, his = sched
    return pl.pallas_call(
        functools.partial(_ffn_kernel, tm=tm, d_ff=d_ff2 // 2),
        grid_spec=pltpu.PrefetchScalarGridSpec(
            num_scalar_prefetch=5,
            grid=(tiles.shape[0],),
            in_specs=[pl.BlockSpec((tm, w), lambda g, ot, t, e, lo, hi: (t[g], 0)),
                      pl.BlockSpec((1, d, d_ff2), lambda g, ot, t, e, lo, hi: (e[g], 0, 0)),
                      pl.BlockSpec((1, 1, d_ff2), lambda g, ot, t, e, lo, hi: (e[g], 0, 0)),
                      pl.BlockSpec((1, d_ff2 // 2, d), lambda g, ot, t, e, lo, hi: (e[g], 0, 0)),
                      pl.BlockSpec((1, 1, d), lambda g, ot, t, e, lo, hi: (e[g], 0, 0))],
            out_specs=pl.BlockSpec((tm, d), lambda g, ot, t, e, lo, hi: (ot[g], 0))),
        out_shape=jax.ShapeDtypeStruct((m, d), F32),
        compiler_params=_params("arbitrary"),
        name="expert_ffn",
    )(out_tiles, tiles, experts, los, his, buf, wgu_bf, b_gu, wd_bf, b_d)


def _moe_schedule(counts, n_rows, tm):
    n_exp = counts.shape[0]
    n_tiles = n_rows // tm
    n_items = n_tiles + n_exp - 1
    ends = jnp.cumsum(counts)
    starts = ends - counts
    first_tile = starts // tm
    n_it = jnp.where(counts > 0, (ends - 1) // tm - first_tile + 1, 0)
    it_end = jnp.cumsum(n_it)
    it_start = it_end - n_it
    item = jnp.arange(n_items, dtype=I32)
    n_real = it_end[-1]
    g = jnp.minimum(item, n_real - 1)
    e = jnp.minimum(jnp.sum((it_end[None, :] <= g[:, None]).astype(I32), axis=1), n_exp - 1)
    tile = (first_tile[e] + g - it_start[e]).astype(I32)
    valid = item < n_real
    lo = jnp.where(valid, jnp.clip(starts[e] - tile * tm, 0, tm), 0).astype(I32)
    hi = jnp.where(valid, jnp.clip(ends[e] - tile * tm, 0, tm), 0).astype(I32)
    used = (ends[-1] + tm - 1) // tm
    out_tile = jnp.where(valid, tile, jnp.minimum(used + item - n_real, n_tiles - 1)).astype(I32)
    return (out_tile, tile, e, lo, hi), starts.astype(I32)


def _combine_kernel(cpad_ref, loff_ref, base_ref, route_ref, gw_ref, x1_ref, g2_ref, b2_ref, eo_ref, y_ref,
                    gbuf, sem, *, ts, n_exp, sizes, alpha):
    t = pl.program_id(0)
    nt = pl.num_programs(0)
    lt = gbuf.shape[1]
    d = gbuf.shape[2]

    def tile_copies(tt, slot, wait):
        def run(e, c):
            n = cpad_ref[tt * n_exp + e]
            src = base_ref[tt * n_exp + e]
            dst = loff_ref[tt * n_exp + e]

            def piece(off, sz):
                cp = pltpu.make_async_copy(eo_ref.at[pl.ds(pl.multiple_of(src + off, SUBLANE), sz), :],
                                           gbuf.at[slot, pl.ds(pl.multiple_of(dst + off, SUBLANE), sz), :],
                                           sem.at[slot])
                cp.wait() if wait else cp.start()

            _for_each_piece(n, sizes, piece)
            return c

        lax.fori_loop(0, n_exp, run, 0)

    def fetch(tt, slot):
        gbuf[slot, TOP_K * ts:, :] = jnp.zeros((lt - TOP_K * ts, d), F32)
        tile_copies(tt, slot, False)

    slot = t % 2

    @pl.when(t == 0)
    def _():
        fetch(t, slot)

    @pl.when(t + 1 < nt)
    def _():
        fetch(t + 1, 1 - slot)

    tile_copies(t, slot, True)

    r = lax.broadcasted_iota(I32, (ts, ts), 0)
    c = lax.broadcasted_iota(I32, (ts, ts), 1)
    eye = jnp.where(r == c, 1.0, 0.0)
    wcol = _dot_nt(eye, gw_ref[...], precision=HI)
    pcol = _dot_nt(eye, route_ref[...].astype(F32), precision=HI)
    rows = gbuf[slot].astype(BF16)
    slot_i = lax.broadcasted_iota(I32, (ts, lt), 1)
    unsort = jnp.zeros((ts, lt), F32)
    for k in range(TOP_K):
        lpos = pcol[:, TOP_K + k:TOP_K + k + 1].astype(I32)
        unsort = unsort + jnp.where(slot_i == lpos, wcol[:, k:k + 1], 0.0)
    hi = unsort.astype(BF16)
    lo = (unsort - hi.astype(F32)).astype(BF16)
    moe = _dot(hi, rows) + _dot(lo, rows)
    y_ref[...] = _layernorm(alpha * x1_ref[...] + moe, g2_ref[...], b2_ref[...])


def _combine(cpad, loff, base, route, gw, x1, ln_g, ln_b, eo, *, ts, n_exp, alpha):
    n, d = x1.shape
    lt = _sort_rows(ts, n_exp)
    full = lambda a: pl.BlockSpec(a.shape, lambda i, *_: (0,) * a.ndim)
    return pl.pallas_call(
        functools.partial(_combine_kernel, ts=ts, n_exp=n_exp, sizes=_run_sizes(ts), alpha=alpha),
        grid_spec=pltpu.PrefetchScalarGridSpec(
            num_scalar_prefetch=3,
            grid=(n // ts,),
            in_specs=[pl.BlockSpec((2 * TOP_K, ts), lambda i, *_: (0, i)),
                      pl.BlockSpec((2 * TOP_K, ts), lambda i, *_: (0, i)),
                      pl.BlockSpec((ts, d), lambda i, *_: (i, 0)), full(ln_g), full(ln_b),
                      pl.BlockSpec(memory_space=pl.ANY)],
            out_specs=pl.BlockSpec((ts, d), lambda i, *_: (i, 0)),
            scratch_shapes=[pltpu.VMEM((2, lt, d), F32), pltpu.SemaphoreType.DMA((2,))]),
        out_shape=jax.ShapeDtypeStruct((n, d), F32),
        compiler_params=_params("arbitrary"),
        name="combine",
    )(cpad, loff, base, route, gw, x1, ln_g, ln_b, eo)


def _finish(x2d, oa, oc, om, wts, w_out, *, tile, tm, alpha):
    n = x2d.shape[0]
    n_exp = wts["w_rT"].shape[0]
    ts = _pick_tile(n, SORT_TILE)
    n_sub = n // ts
    x1, xs, route, gw, cpad, loff, cbase, tot = _finish1(
        x2d, oa, oc, om, wts["g_mix"], w_out, wts["ln1_g"], wts["ln1_b"], wts["w_rT"], wts["b_r"],
        tile=_pick_tile(n, tile), ts=ts, alpha=alpha)
    m_pad = n_sub * _sort_rows(ts, n_exp) + tm
    counts = tot[:, 0].astype(I32)
    sched, offs = _moe_schedule(counts, m_pad, tm)
    flat = lambda a: a[:, :, 0].astype(I32).reshape(-1)
    cpad, loff = flat(cpad), flat(loff)
    base = (cbase[:, :, 0].astype(I32) + offs[None, :]).reshape(-1)
    fill = jnp.sum(counts, keepdims=True)
    buf = _dispatch(cpad, loff, base, fill, xs, ts=ts, n_exp=n_exp, m_pad=m_pad, tm=tm)
    eo = _ffn(sched, buf, wts["w_gu"], wts["b_gu"], wts["w_d"], wts["b_d"], tm=tm)
    return _combine(cpad, loff, base, route, gw, x1, wts["ln2_g"], wts["ln2_b"], eo, ts=ts, n_exp=n_exp, alpha=alpha)


def _sample_inproj_kernel(x_ref, w_ref, wc_ref, p0_ref, p1_ref, q_ref, k_ref, v_ref, qm_ref, oc_ref, u_ref,
                          *, att_w, mem_w, conv_w):
    x = x_ref[...]
    c1, c2, c3 = att_w, 2 * att_w, 3 * att_w
    c4 = c3 + mem_w
    c5, c6, c7 = c4 + conv_w, c4 + 2 * conv_w, c4 + 3 * conv_w

    def proj(lo, hi):
        return _dot(x, w_ref[:, lo:hi], precision=HI)

    q_ref[...] = proj(0, c1) * Q_SCALE
    k_ref[...] = proj(c1, c2)
    v_ref[...] = proj(c2, c3)
    qm_ref[...] = proj(c3, c4) * Q_SCALE
    u = proj(c5, c6) * proj(c6, c7)
    cz = wc_ref[0:1, :] * p0_ref[...] + wc_ref[1:2, :] * p1_ref[...] + wc_ref[2:3, :] * u
    oc_ref[...] = proj(c4, c5) * cz
    u_ref[...] = u


def _sample_inproj(x2d, w_bf, w_conv, prev0, prev1, *, att_w, mem_w, conv_w):
    n = x2d.shape[0]
    args = (x2d, w_bf, w_conv, prev0, prev1)
    full = lambda a: pl.BlockSpec(a.shape, lambda i: (0,) * a.ndim)
    widths = (att_w, att_w, att_w, mem_w, conv_w, conv_w)
    return pl.pallas_call(
        functools.partial(_sample_inproj_kernel, att_w=att_w, mem_w=mem_w, conv_w=conv_w),
        grid=(1,),
        in_specs=[full(a) for a in args],
        out_specs=[pl.BlockSpec((n, w), lambda i: (0, 0)) for w in widths],
        out_shape=[jax.ShapeDtypeStruct((n, w), F32) for w in widths],
        compiler_params=_params("arbitrary"),
        name="sample_inproj",
    )(*args)


def _kscan_kernel(pt_ref, q_ref, ck_ref, sel_ref, kbuf, ksum, sem, *, n_pages, chunk, n_heads):
    b = pl.program_id(0)
    n_chunks = n_pages // chunk
    pages_per_blk = MOBA_BLOCK // PAGE_SIZE
    nblk = ksum.shape[0]
    q_col = q_ref[0]

    def copies(bb, c, slot):
        return [pltpu.make_async_copy(ck_ref.at[0, pt_ref[bb * n_pages + c * chunk + j]], kbuf.at[slot, j],
                                      sem.at[slot]) for j in range(chunk)]

    @pl.when(b == 0)
    def _():
        for cp in copies(b, 0, 0):
            cp.start()

    def body(c, carry):
        slot = c % 2
        last = c + 1 == n_chunks

        @pl.when(jnp.logical_or(jnp.logical_not(last), b + 1 < pl.num_programs(0)))
        def _():
            for cp in copies(jnp.where(last, b + 1, b), jnp.where(last, 0, c + 1), 1 - slot):
                cp.start()

        for cp in copies(b, c, slot):
            cp.wait()
        for jb in range(chunk // pages_per_blk):
            s = kbuf[slot, pages_per_blk * jb]
            for pg in range(1, pages_per_blk):
                s = s + kbuf[slot, pages_per_blk * jb + pg]
            ksum[c * (chunk // pages_per_blk) + jb] = jnp.sum(s * q_col, axis=1)
        return carry

    lax.fori_loop(0, n_chunks, body, 0)

    g = jnp.sum(ksum[...], axis=2, keepdims=True) * (1.0 / MOBA_BLOCK)
    blk = lax.broadcasted_iota(I32, g.shape, 0)
    sel_ref[...] = jnp.zeros(sel_ref.shape, I32)
    for r in range(MOBA_TOPK):
        m = jnp.max(g, axis=0, keepdims=True)
        idx = jnp.min(jnp.where((g == m) & (g > -jnp.inf), blk, nblk), axis=0, keepdims=True)
        sel_ref[0, :, r:r + 1] = idx[0]
        g = jnp.where(blk == idx, -jnp.inf, g)


def _kscan(pt_flat, q_col, ckT, *, n_pages, chunk):
    bs, n_heads, hd, _ = q_col.shape
    nblk = n_pages * PAGE_SIZE // MOBA_BLOCK
    return pl.pallas_call(
        functools.partial(_kscan_kernel, n_pages=n_pages, chunk=chunk, n_heads=n_heads),
        grid_spec=pltpu.PrefetchScalarGridSpec(
            num_scalar_prefetch=1,
            grid=(bs,),
            in_specs=[pl.BlockSpec((1, n_heads, hd, 1), lambda b, pt: (b, 0, 0, 0)), pl.BlockSpec(memory_space=pl.ANY)],
            out_specs=pl.BlockSpec((1, n_heads, LANE), lambda b, pt: (b, 0, 0)),
            scratch_shapes=[pltpu.VMEM((2, chunk, n_heads, hd, PAGE_SIZE), F32),
                            pltpu.VMEM((nblk, n_heads, PAGE_SIZE), F32), pltpu.SemaphoreType.DMA((2,))]),
        out_shape=jax.ShapeDtypeStruct((bs, n_heads, LANE), I32),
        compiler_params=_params("arbitrary"),
        name="kscan",
    )(pt_flat, q_col, ckT)


def _sample_attn_kernel(pt_ref, sel_ref, slopes_ref, q_ref, kn_ref, vn_ref, qm_ref, mk_ref, mv_ref, ck_ref, cv_ref,
                        oa_ref, om_ref, kbuf, vbuf, sem, *, n_pages, n_heads, past_len, mem_w):
    b = pl.program_id(0)
    nb = pl.num_programs(0)
    B = MOBA_BLOCK
    pages_per_blk = B // PAGE_SIZE
    n_keys = MOBA_TOPK * B

    def blocks(bb, h):
        return [sel_ref[(bb * n_heads + h) * MOBA_TOPK + s] for s in range(MOBA_TOPK)]

    n_sel_pages = MOBA_TOPK * pages_per_blk

    def copies(bb, slot):
        cps = []
        for h in range(n_heads):
            blks = blocks(bb, h)
            for s in range(MOBA_TOPK):
                for half in range(pages_per_blk):
                    pg = pt_ref[bb * n_pages + pages_per_blk * blks[s] + half]
                    pp = s * pages_per_blk + half
                    cps.append(pltpu.make_async_copy(ck_ref.at[0, pg, h], kbuf.at[slot, h, pp], sem.at[0, slot]))
                    cps.append(pltpu.make_async_copy(cv_ref.at[0, pg, h], vbuf.at[slot, h, pp], sem.at[1, slot]))
        return cps

    slot = b % 2

    @pl.when(b == 0)
    def _():
        for cp in copies(b, slot):
            cp.start()

    @pl.when(b + 1 < nb)
    def _():
        for cp in copies(b + 1, 1 - slot):
            cp.start()

    for cp in copies(b, slot):
        cp.wait()

    key_lane = lax.broadcasted_iota(I32, (1, n_keys), 1)
    lane = lax.broadcasted_iota(I32, (1, LANE), 1)
    for h in range(n_heads):
        blks = blocks(b, h)
        qh = q_ref[0, h:h + 1, :]
        q8 = jnp.broadcast_to(qh, (SUBLANE, HEAD_DIM))
        blk_of_key = jnp.where(key_lane < B, blks[0], jnp.where(key_lane < 2 * B, blks[1], blks[2]))
        dist = (past_len - blk_of_key * B - (key_lane & (B - 1))).astype(F32)
        s = jnp.concatenate([_dot(q8.astype(BF16), kbuf[slot, h, pp].astype(BF16)) for pp in range(n_sel_pages)],
                            axis=1)
        s = s - slopes_ref[h] * dist
        s_self = jnp.sum(qh * kn_ref[0, h:h + 1, :], axis=1, keepdims=True)
        m = jnp.maximum(jnp.max(s, axis=1, keepdims=True), s_self)
        e = jnp.exp(s - m)
        e_self = jnp.exp(s_self - m)
        l = jnp.sum(e, axis=1, keepdims=True) + e_self
        o = e_self * vn_ref[0, h:h + 1, :]
        for pp in range(n_sel_pages):
            o = o + _dot_nt(e[:, pp * PAGE_SIZE:(pp + 1) * PAGE_SIZE].astype(BF16), vbuf[slot, h, pp].astype(BF16))
        oa_ref[0, h:h + 1, :] = (o / l)[0:1, :]

    qm_row = qm_ref[pl.ds(b, 1), :]
    outs = []
    for pr in range(mem_w // LANE):
        cs = slice(pr * LANE, (pr + 1) * LANE)
        mkp = mk_ref[0, :, cs]
        mvp = mv_ref[0, :, cs]
        pair = []
        for hh in range(2):
            qh = jnp.where(_head_mask((1, LANE), hh), qm_row[:, cs], 0.0)
            s = _dot_nt(jnp.broadcast_to(qh, (SUBLANE, LANE)), mkp, precision=HI)
            e = jnp.exp(s - jnp.max(s, axis=1, keepdims=True))
            o = _dot(e, mvp, precision=HI) / jnp.sum(e, axis=1, keepdims=True)
            pair.append(o[0:1, :])
        outs.append(jnp.where(lane < HEAD_DIM, pair[0], pair[1]))
    om_ref[0] = jnp.concatenate(outs, axis=1)


def _sample_attn(pt_flat, sel_flat, slopes, q3, kn3, vn3, qm, mk, mv, ck, cv, *, n_pages):
    bs, n_heads, hd = q3.shape
    _, n_mem, mem_w = mk.shape
    n_keys = MOBA_TOPK * MOBA_BLOCK
    per_b = lambda a: pl.BlockSpec((1,) + a.shape[1:], lambda b, pt, sel: (b,) + (0,) * (a.ndim - 1))
    return pl.pallas_call(
        functools.partial(_sample_attn_kernel, n_pages=n_pages, n_heads=n_heads, past_len=n_pages * PAGE_SIZE,
                          mem_w=mem_w),
        grid_spec=pltpu.PrefetchScalarGridSpec(
            num_scalar_prefetch=2,
            grid=(bs,),
            in_specs=[pl.BlockSpec(memory_space=pltpu.SMEM), per_b(q3), per_b(kn3), per_b(vn3),
                      pl.BlockSpec(qm.shape, lambda b, pt, sel: (0, 0)), per_b(mk), per_b(mv),
                      pl.BlockSpec(memory_space=pl.ANY), pl.BlockSpec(memory_space=pl.ANY)],
            out_specs=[pl.BlockSpec((1, n_heads, hd), lambda b, pt, sel: (b, 0, 0)),
                       pl.BlockSpec((1, 1, mem_w), lambda b, pt, sel: (b, 0, 0))],
            scratch_shapes=[pltpu.VMEM((2, n_heads, n_keys // PAGE_SIZE, hd, PAGE_SIZE), F32),
                            pltpu.VMEM((2, n_heads, n_keys // PAGE_SIZE, hd, PAGE_SIZE), F32),
                            pltpu.SemaphoreType.DMA((2, 2))]),
        out_shape=[jax.ShapeDtypeStruct((bs, n_heads, hd), F32), jax.ShapeDtypeStruct((bs, 1, mem_w), F32)],
        compiler_params=_params("arbitrary"),
        name="sample_attn",
    )(pt_flat, sel_flat, slopes, q3, kn3, vn3, qm, mk, mv, ck, cv)


def _sample_layer(x, cache_k, cache_v, mem_k, mem_v, state, page_table, wts, dims, alpha):
    bs, dec_seq, d = x.shape
    assert dec_seq == 1
    att_w, mem_w, conv_w = dims
    n_heads = att_w // HEAD_DIM
    n_pages = page_table.shape[1]
    assert (n_pages * PAGE_SIZE) % MOBA_BLOCK == 0 and n_pages * PAGE_SIZE // MOBA_BLOCK >= MOBA_TOPK
    x2d = x.reshape(bs, d)
    q, k, v, qm, oc, u = _sample_inproj(x2d, wts["w_in_f32"], wts["w_conv"], state[:, 0, :], state[:, 1, :],
                                        att_w=att_w, mem_w=mem_w, conv_w=conv_w)
    pt_flat = page_table.reshape(-1)
    heads = lambda a: a.reshape(bs, n_heads, HEAD_DIM)
    pagesT = lambda c: jnp.transpose(c, (0, 1, 3, 4, 2))
    chunk = 16 if n_pages % 32 == 0 else 8
    assert n_pages % (2 * chunk) == 0
    sel = _kscan(pt_flat, q.reshape(bs, n_heads, HEAD_DIM, 1), pagesT(cache_k), n_pages=n_pages, chunk=chunk)
    sel_flat = sel[:, :, :MOBA_TOPK].reshape(-1)
    oa, om = _sample_attn(pt_flat, sel_flat, wts["slopes"], heads(q), heads(k), heads(v), qm,
                          mem_k.reshape(bs, -1, mem_w), mem_v.reshape(bs, -1, mem_w), pagesT(cache_k),
                          pagesT(cache_v), n_pages=n_pages)
    y = _finish(x2d, oa.reshape(bs, att_w), oc, om.reshape(bs, mem_w), wts, wts["w_out_f32"], tile=bs,
                tm=bs * TOP_K, alpha=alpha)
    conv_state = jnp.stack([state[:, 1, :], u], axis=1)
    return y.reshape(bs, 1, d), k, v, conv_state


def _pick_tile(n, pref):
    return pref if n % pref == 0 else n


def _prompt_layer(x, mem, wts, dims, alpha):
    batch, seq, d = x.shape
    att_w, mem_w, conv_w = dims
    n = batch * seq
    n_mem = mem.shape[1]
    x2d = x.reshape(n, d)
    tile = 512
    q, k, v, kT, vT, kmean, qm, oc, tail = _inproj(x2d, wts["w_in"], wts["w_conv"], seq=seq, tile=tile, att_w=att_w,
                                                   mem_w=mem_w, conv_w=conv_w)
    oa = _moba_prompt(q, k, v, kmean.reshape(n // MOBA_BLOCK, att_w), wts["slopes"], batch=batch, seq=seq)
    mk, mv = _memkv(mem.reshape(batch * n_mem, d), wts["w_mem_kv"], tile=_pick_tile(batch * n_mem, 512), mem_w=mem_w)
    om = _memattn_prompt(qm, mk, mv, batch=batch, seq=seq, n_mem=n_mem, tq=512)
    y = _finish(x2d, oa, oc, om, wts, wts["w_out"], tile=512, tm=256, alpha=alpha)
    conv_state = tail.reshape(batch, seq // tile, SUBLANE, conv_w)[:, -1, SUBLANE - (CONV_K - 1):, :]
    rows = lambda t: jnp.transpose(t, (0, 3, 1, 2))
    return y.reshape(batch, seq, d), rows(kT), rows(vT), conv_state, mk, mv


def _prep_weights(l, w_in, w_mem_kv, w_conv, g_mix, w_out, ln1_g, ln1_b, w_router, b_router, w_gate_up, b_gate_up,
                  w_down, b_down, ln2_g, ln2_b):
    n_heads = 8
    row = lambda a: a[l][None, :]
    return {
        "w_in": w_in[l].astype(BF16), "w_in_f32": w_in[l], "w_mem_kv": w_mem_kv[l].astype(BF16), "w_conv": w_conv[l],
        "g_mix": row(g_mix), "w_out": w_out[l].astype(BF16), "w_out_f32": w_out[l], "ln1_g": row(ln1_g),
        "ln1_b": row(ln1_b),
        "w_rT": w_router[l].T, "b_r": b_router[l][:, None],
        "w_gu": w_gate_up[l].astype(BF16), "b_gu": b_gate_up[l][:, None, :],
        "w_d": w_down[l].astype(BF16), "b_d": b_down[l][:, None, :],
        "ln2_g": row(ln2_g), "ln2_b": row(ln2_b),
        "slopes": 2.0 ** (-8.0 * jnp.arange(1, n_heads + 1, dtype=F32) / n_heads),
    }


def kernel(x_prompt, x_sample, cache_k, cache_v, cache_mem_k, cache_mem_v, state_conv, page_table, mem_prompt, w_in, w_mem_kv, w_conv, g_mix, w_out, ln1_g, ln1_b, w_router, b_router, w_gate_up, b_gate_up, w_down, b_down, ln2_g, ln2_b):
    depth = w_in.shape[0]
    assert depth == 1
    alpha = (2 * depth) ** 0.25
    n_heads = cache_k.shape[3]
    att_w = n_heads * HEAD_DIM
    mem_w = cache_mem_k.shape[3] * HEAD_DIM
    conv_w = state_conv.shape[3]
    dims = (att_w, mem_w, conv_w)
    wts = _prep_weights(0, w_in, w_mem_kv, w_conv, g_mix, w_out, ln1_g, ln1_b, w_router, b_router, w_gate_up,
                        b_gate_up, w_down, b_down, ln2_g, ln2_b)
    bp, seq, d = x_prompt.shape
    n_mem = mem_prompt.shape[1]
    y_p, k_p, v_p, conv_p, mk_p, mv_p = _prompt_layer(x_prompt, mem_prompt, wts, dims, alpha)
    bs = x_sample.shape[0]
    y_s, k_s, v_s, conv_s = _sample_layer(x_sample, cache_k, cache_v, cache_mem_k[0], cache_mem_v[0],
                                          state_conv[0], page_table, wts, dims, alpha)
    return (y_p, y_s,
            k_p[None], v_p[None], conv_p[None], mk_p.reshape(1, bp, n_mem, mem_w // HEAD_DIM, HEAD_DIM),
            mv_p.reshape(1, bp, n_mem, mem_w // HEAD_DIM, HEAD_DIM),
            k_s.reshape(1, bs, 1, n_heads, HEAD_DIM), v_s.reshape(1, bs, 1, n_heads, HEAD_DIM), conv_s[None])
```

```python
import functools

import jax
import jax.numpy as jnp
from jax import lax
from jax.experimental import pallas as pl
from jax.experimental.pallas import tpu as pltpu

F32 = jnp.float32
BF16 = jnp.bfloat16
I32 = jnp.int32

HEAD_DIM = 64
MOBA_BLOCK = 256
MOBA_TOPK = 3
PAGE_SIZE = 128
TOP_K = 4
CONV_K = 3
SWIGLU_LIMIT = 7.0
SWIGLU_ALPHA = 1.702
LN_EPS = 1e-5
Q_SCALE = HEAD_DIM ** -0.5
NEG = -1e30
LANE = 128
SUBLANE = 8
VMEM_LIMIT = 56 * 1024 * 1024
HI = lax.Precision.HIGHEST


def _params(*sem):
    return pltpu.CompilerParams(dimension_semantics=sem, vmem_limit_bytes=VMEM_LIMIT)


def _dot_nt(a, b, precision=None):
    return lax.dot_general(a, b, (((1,), (1,)), ((), ())), precision=precision, preferred_element_type=F32)


def _dot(a, b, precision=None):
    return jnp.dot(a, b, precision=precision, preferred_element_type=F32)


def _layernorm(z, g, b):
    zc = z - jnp.mean(z, axis=-1, keepdims=True)
    var = jnp.mean(zc * zc, axis=-1, keepdims=True)
    return zc * lax.rsqrt(var + LN_EPS) * g + b


def _rms(a):
    return a * lax.rsqrt(jnp.mean(a * a, axis=-1, keepdims=True) + LN_EPS)


def _head_mask(shape, hh):
    lane = lax.broadcasted_iota(I32, shape, len(shape) - 1)
    return (lane >= HEAD_DIM * hh) & (lane < HEAD_DIM * (hh + 1))


def _inproj_kernel(x_ref, w_ref, wc_ref, q_ref, k_ref, v_ref, kt_ref, vt_ref, km_ref, qm_ref, oc_ref, tail_ref, ubuf,
                   *, tile, tiles_per_seq, att_w, mem_w, conv_w):
    i = pl.program_id(0)
    x = x_ref[...].astype(BF16)
    c1, c2, c3 = att_w, 2 * att_w, 3 * att_w
    c4 = c3 + mem_w
    c5, c6, c7 = c4 + conv_w, c4 + 2 * conv_w, c4 + 3 * conv_w
    n_heads = att_w // HEAD_DIM

    def proj(lo, hi):
        return _dot(x, w_ref[:, lo:hi])

    q_ref[...] = (proj(0, c1) * Q_SCALE).astype(BF16)
    k = proj(c1, c2)
    k_ref[...] = k
    kt_ref[0] = k.T.reshape(n_heads, HEAD_DIM, tile)
    km_ref[0] = jnp.sum(k.reshape(tile // MOBA_BLOCK, MOBA_BLOCK, att_w), axis=1) * (1.0 / MOBA_BLOCK)
    v = proj(c2, c3)
    v_ref[...] = v
    vt_ref[0] = v.T.reshape(n_heads, HEAD_DIM, tile)
    qm_ref[...] = (proj(c3, c4) * Q_SCALE).astype(BF16)
    gb = proj(c4, c5)
    u = proj(c5, c6) * proj(c6, c7)

    @pl.when(i % tiles_per_seq == 0)
    def _():
        ubuf[0:SUBLANE, :] = jnp.zeros((SUBLANE, conv_w), F32)

    ubuf[SUBLANE:SUBLANE + tile, :] = u
    u1 = ubuf[SUBLANE - 1:SUBLANE - 1 + tile, :]
    u2 = ubuf[SUBLANE - 2:SUBLANE - 2 + tile, :]
    cz = wc_ref[0:1, :] * u2 + wc_ref[1:2, :] * u1 + wc_ref[2:3, :] * u
    oc_ref[...] = gb * cz
    tail = ubuf[tile:tile + SUBLANE, :]
    tail_ref[0] = tail
    ubuf[0:SUBLANE, :] = tail


def _inproj(x2d, w_bf, w_conv, *, seq, tile, att_w, mem_w, conv_w):
    n, d = x2d.shape
    nt = n // tile
    kern = functools.partial(_inproj_kernel, tile=tile, tiles_per_seq=seq // tile, att_w=att_w, mem_w=mem_w,
                             conv_w=conv_w)
    row = lambda w: pl.BlockSpec((tile, w), lambda i: (i, 0))
    tps = seq // tile
    n_heads = att_w // HEAD_DIM
    headsT = pl.BlockSpec((1, n_heads, HEAD_DIM, tile), lambda i: (i // tps, 0, 0, i % tps))
    headsT_shape = jax.ShapeDtypeStruct((n // seq, n_heads, HEAD_DIM, seq), F32)
    return pl.pallas_call(
        kern,
        grid=(nt,),
        in_specs=[row(d), pl.BlockSpec(w_bf.shape, lambda i: (0, 0)), pl.BlockSpec(w_conv.shape, lambda i: (0, 0))],
        out_specs=[row(att_w), row(att_w), row(att_w), headsT, headsT,
                   pl.BlockSpec((1, tile // MOBA_BLOCK, att_w), lambda i: (i, 0, 0)),
                   row(mem_w), row(conv_w), pl.BlockSpec((1, SUBLANE, conv_w), lambda i: (i, 0, 0))],
        out_shape=[jax.ShapeDtypeStruct((n, att_w), BF16), jax.ShapeDtypeStruct((n, att_w), F32),
                   jax.ShapeDtypeStruct((n, att_w), F32), headsT_shape, headsT_shape,
                   jax.ShapeDtypeStruct((nt, tile // MOBA_BLOCK, att_w), F32),
                   jax.ShapeDtypeStruct((n, mem_w), BF16), jax.ShapeDtypeStruct((n, conv_w), F32),
                   jax.ShapeDtypeStruct((nt, SUBLANE, conv_w), F32)],
        scratch_shapes=[pltpu.VMEM((tile + SUBLANE, conv_w), F32)],
        compiler_params=_params("arbitrary"),
        name="inproj",
    )(x2d, w_bf, w_conv)


def _moba_kernel(slopes_ref, q_ref, k_ref, v_ref, km_ref, o_ref, kbf, vT, q2_s, m_s, l_s, acc_s, *, nblk):
    p = pl.program_id(1)
    B = MOBA_BLOCK
    W = 2 * B

    for c in range(nblk):
        kbf[c] = k_ref[c * B:(c + 1) * B, :].astype(BF16)
        vT[c] = v_ref[c * B:(c + 1) * B, :].T.astype(BF16)
        q = q_ref[c * B:(c + 1) * B, :]
        zero = jnp.zeros_like(q)
        q2_s[c * W:c * W + B, :] = jnp.where(_head_mask(q.shape, 0), q, zero)
        q2_s[c * W + B:(c + 1) * W, :] = jnp.where(_head_mask(q.shape, 1), q, zero)

    second = lax.broadcasted_iota(I32, (1, W), 1) >= B
    slope = jnp.where(second, slopes_ref[2 * p + 1], slopes_ref[2 * p])
    key_i = lax.broadcasted_iota(I32, (B, W), 0)
    qry_i = lax.broadcasted_iota(I32, (B, W), 1) & (B - 1)
    base = slope * (key_i - qry_i).astype(F32)
    causal = key_i <= qry_i

    blk = lax.broadcasted_iota(I32, (nblk, nblk * W), 0)
    tile_of_lane = lax.broadcasted_iota(I32, (nblk, nblk * W), 1) >> (W.bit_length() - 1)
    g = _dot_nt(km_ref[...], q2_s[...].astype(F32), precision=HI)
    g = jnp.where(blk < tile_of_lane, g, -jnp.inf)
    sel_bias = jnp.full((nblk, nblk * W), NEG, F32)
    for _ in range(MOBA_TOPK):
        m = jnp.max(g, axis=0, keepdims=True)
        cand = (g == m) & (g > -jnp.inf)
        idx = jnp.min(jnp.where(cand, blk, nblk), axis=0, keepdims=True)
        pick = blk == idx
        sel_bias = jnp.where(pick, 0.0, sel_bias)
        g = jnp.where(pick, -jnp.inf, g)

    for j in reversed(range(nblk)):
        s_all = _dot_nt(kbf[j], q2_s[j * W:, :])
        probs, scales = [], []
        for i in range(j, nblk):
            here = slice((i - j) * W, (i - j + 1) * W)
            lanes = slice(i * W, (i + 1) * W)
            if i == j:
                s = jnp.where(causal, s_all[:, here] + base, NEG)
                m_new = jnp.max(s, axis=0, keepdims=True)
                pj = jnp.exp(s - m_new)
                l_s[:, lanes] = jnp.sum(pj, axis=0, keepdims=True)
                scales.append(None)
            else:
                s = s_all[:, here] + base + (sel_bias[j:j + 1, lanes] - slope * float(B * (i - j)))
                m_old = m_s[:, lanes]
                m_new = jnp.maximum(m_old, jnp.max(s, axis=0, keepdims=True))
                a = jnp.exp(m_old - m_new)
                pj = jnp.exp(s - m_new)
                l_s[:, lanes] = a * l_s[:, lanes] + jnp.sum(pj, axis=0, keepdims=True)
                scales.append(a)
            m_s[:, lanes] = m_new
            probs.append(pj.astype(BF16))
        pv = _dot(vT[j], jnp.concatenate(probs, axis=1))
        for i in range(j, nblk):
            here = slice((i - j) * W, (i - j + 1) * W)
            lanes = slice(i * W, (i + 1) * W)
            a = scales[i - j]
            acc_s[:, lanes] = pv[:, here] if a is None else a * acc_s[:, lanes] + pv[:, here]

    sub = lax.broadcasted_iota(I32, (2 * HEAD_DIM, B), 0)
    for i in range(nblk):
        o = acc_s[:, i * W:(i + 1) * W] / l_s[:, i * W:(i + 1) * W]
        o_ref[i * B:(i + 1) * B, :] = jnp.where(sub < HEAD_DIM, o[:, :B], o[:, B:]).T


def _moba_prompt(q, k, v, kmean, slopes, *, batch, seq):
    n, att_w = q.shape
    nblk = seq // MOBA_BLOCK
    B = MOBA_BLOCK
    kern = functools.partial(_moba_kernel, nblk=nblk)
    return pl.pallas_call(
        kern,
        grid_spec=pltpu.PrefetchScalarGridSpec(
            num_scalar_prefetch=1,
            grid=(batch, att_w // LANE),
            in_specs=[pl.BlockSpec((seq, LANE), lambda b, p, s: (b, p)),
                      pl.BlockSpec((seq, LANE), lambda b, p, s: (b, p)),
                      pl.BlockSpec((seq, LANE), lambda b, p, s: (b, p)),
                      pl.BlockSpec((nblk, LANE), lambda b, p, s: (b, p))],
            out_specs=pl.BlockSpec((seq, LANE), lambda b, p, s: (b, p)),
            scratch_shapes=[pltpu.VMEM((nblk, B, LANE), BF16), pltpu.VMEM((nblk, LANE, B), BF16),
                            pltpu.VMEM((nblk * 2 * B, LANE), BF16), pltpu.VMEM((1, nblk * 2 * B), F32),
                            pltpu.VMEM((1, nblk * 2 * B), F32), pltpu.VMEM((LANE, nblk * 2 * B), F32)]),
        out_shape=jax.ShapeDtypeStruct((n, att_w), F32),
        compiler_params=_params("arbitrary", "arbitrary"),
        name="moba_prompt",
    )(slopes, q, k, v, kmean)


def _memkv_kernel(x_ref, w_ref, mk_ref, mv_ref, *, mem_w):
    r = _dot(x_ref[...].astype(BF16), w_ref[...])
    mk_ref[...] = r[:, :mem_w]
    mv_ref[...] = r[:, mem_w:]


def _memkv(mem2d, w_bf, *, tile, mem_w):
    n, d = mem2d.shape
    return pl.pallas_call(
        functools.partial(_memkv_kernel, mem_w=mem_w),
        grid=(n // tile,),
        in_specs=[pl.BlockSpec((tile, d), lambda i: (i, 0)), pl.BlockSpec(w_bf.shape, lambda i: (0, 0))],
        out_specs=[pl.BlockSpec((tile, mem_w), lambda i: (i, 0))] * 2,
        out_shape=[jax.ShapeDtypeStruct((n, mem_w), F32)] * 2,
        compiler_params=_params("arbitrary"),
        name="memkv",
    )(mem2d, w_bf)


def _memattn_kernel(qm_ref, mk_ref, mv_ref, o_ref, *, tq, mem_w):
    sub = lax.broadcasted_iota(I32, (LANE, tq), 0)
    for pr in range(mem_w // LANE):
        cs = slice(LANE * pr, LANE * (pr + 1))
        qp = qm_ref[:, cs]
        mkp = mk_ref[:, cs].astype(BF16)
        mvT = mv_ref[:, cs].T.astype(BF16)
        outs = []
        for hh in range(2):
            qh = jnp.where(_head_mask(qp.shape, hh), qp, jnp.zeros_like(qp))
            s = _dot_nt(mkp, qh)
            m = jnp.max(s, axis=0, keepdims=True)
            e = jnp.exp(s - m)
            l = jnp.sum(e, axis=0, keepdims=True)
            outs.append(_dot(mvT, e.astype(BF16)) / l)
        o_ref[:, cs] = jnp.where(sub < HEAD_DIM, outs[0], outs[1]).T


def _memattn_prompt(qm, mk, mv, *, batch, seq, n_mem, tq):
    n, mem_w = qm.shape
    nq = seq // tq
    return pl.pallas_call(
        functools.partial(_memattn_kernel, tq=tq, mem_w=mem_w),
        grid=(batch, nq),
        in_specs=[pl.BlockSpec((tq, mem_w), lambda b, i: (b * nq + i, 0)),
                  pl.BlockSpec((n_mem, mem_w), lambda b, i: (b, 0)),
                  pl.BlockSpec((n_mem, mem_w), lambda b, i: (b, 0))],
        out_specs=pl.BlockSpec((tq, mem_w), lambda b, i: (b * nq + i, 0)),
        out_shape=jax.ShapeDtypeStruct((n, mem_w), F32),
        compiler_params=_params("arbitrary", "arbitrary"),
        name="memattn_prompt",
    )(qm, mk, mv)


SORT_TILE = 256


def _sort_rows(ts, n_exp):
    return TOP_K * ts + SUBLANE * n_exp


def _finish1_kernel(x_ref, oa_ref, oc_ref, om_ref, gmix_ref, wout_ref, g1_ref, b1_ref, wrT_ref, br_ref,
                    x1_ref, xs_ref, route_ref, gw_ref, cpad_ref, loff_ref, cbase_ref, tot_ref, carry,
                    *, tile, ts, n_exp, alpha):
    i = pl.program_id(0)
    lt = _sort_rows(ts, n_exp)

    @pl.when(i == 0)
    def _():
        carry[...] = jnp.zeros_like(carry)

    mix = jnp.concatenate([_rms(oa_ref[...]), _rms(oc_ref[...]), _rms(om_ref[...])], axis=-1) * gmix_ref[...]
    if wout_ref.dtype == BF16:
        z = alpha * x_ref[...] + _dot(mix.astype(BF16), wout_ref[...])
    else:
        z = alpha * x_ref[...] + _dot(mix, wout_ref[...], precision=HI)
    x1 = _layernorm(z, g1_ref[...], b1_ref[...])
    x1_ref[...] = x1
    x1b = x1.astype(BF16)

    g = _dot_nt(wrT_ref[...], x1, precision=HI) + br_ref[...]
    eidx = lax.broadcasted_iota(I32, (n_exp, tile), 0)
    picks, vals = [], []
    for k in range(TOP_K):
        m = jnp.max(g, axis=0, keepdims=True)
        idx = jnp.min(jnp.where(g == m, eidx, n_exp), axis=0, keepdims=True)
        pick = eidx == idx
        route_ref[k:k + 1, :] = idx
        picks.append(pick)
        vals.append(m)
        g = jnp.where(pick, -jnp.inf, g)
    ex = [jnp.exp(v - vals[0]) for v in vals]
    denom = ex[0] + ex[1] + ex[2] + ex[3]
    for k in range(TOP_K):
        gw_ref[k:k + 1, :] = ex[k] / denom
    gw_ref[TOP_K:, :] = jnp.zeros((gw_ref.shape[0] - TOP_K, tile), F32)

    t_src = lax.broadcasted_iota(I32, (ts, ts), 0)
    t_dst = lax.broadcasted_iota(I32, (ts, ts), 1)
    before = jnp.where(t_src < t_dst, 1.0, 0.0).astype(BF16)
    e_src = lax.broadcasted_iota(I32, (n_exp, n_exp), 1)
    e_dst = lax.broadcasted_iota(I32, (n_exp, n_exp), 0)
    lower = jnp.where(e_src < e_dst, 1.0, 0.0).astype(BF16)
    slot = lax.broadcasted_iota(I32, (lt, ts), 0)
    for sub in range(tile // ts):
        cs = slice(sub * ts, (sub + 1) * ts)
        pk = [p[:, cs] for p in picks]
        onehot = jnp.zeros((n_exp, ts), F32)
        for p in pk:
            onehot = onehot + jnp.where(p, 1.0, 0.0)
        cnt = jnp.sum(onehot, axis=1, keepdims=True)
        cpad = jnp.floor((cnt + (SUBLANE - 1)) * (1.0 / SUBLANE)) * SUBLANE
        cpad_l = jnp.broadcast_to(cpad, (n_exp, LANE))
        loff_l = _dot(lower, cpad_l.astype(BF16))
        pos = _dot(onehot.astype(BF16), before) + loff_l[:, 0:1]
        perm = jnp.zeros((lt, ts), F32)
        for k in range(TOP_K):
            lpos = jnp.sum(jnp.where(pk[k], pos, 0.0), axis=0, keepdims=True).astype(I32)
            route_ref[TOP_K + k:TOP_K + k + 1, cs] = lpos
            perm = perm + jnp.where(slot == lpos, 1.0, 0.0)
        perm = perm.astype(BF16)
        xs_ref[sub * lt:(sub + 1) * lt, :] = _dot(perm, x1b[cs, :])
        cpad_ref[sub] = cpad_l
        loff_ref[sub] = loff_l
        cbase_ref[sub] = jnp.broadcast_to(carry[...], (n_exp, LANE))
        carry[...] = carry[...] + cpad
    tot_ref[...] = jnp.broadcast_to(carry[...], tot_ref.shape)


def _finish1(x2d, oa, oc, om, g_mix, wout_bf, ln_g, ln_b, w_rT, b_r, *, tile, ts, alpha):
    n, d = x2d.shape
    n_exp = w_rT.shape[0]
    lt = _sort_rows(ts, n_exp)
    spt = tile // ts
    kern = functools.partial(_finish1_kernel, tile=tile, ts=ts, n_exp=n_exp, alpha=alpha)
    row = lambda w: pl.BlockSpec((tile, w), lambda i: (i, 0))
    full = lambda a: pl.BlockSpec(a.shape, lambda i: (0,) * a.ndim)
    meta = pl.BlockSpec((spt, n_exp, LANE), lambda i: (i, 0, 0))
    meta_shape = jax.ShapeDtypeStruct((n // ts, n_exp, LANE), F32)
    return pl.pallas_call(
        kern,
        grid=(n // tile,),
        in_specs=[row(d), row(oa.shape[1]), row(oc.shape[1]), row(om.shape[1]), full(g_mix), full(wout_bf),
                  full(ln_g), full(ln_b), full(w_rT), full(b_r)],
        out_specs=[row(d), pl.BlockSpec((spt * lt, d), lambda i: (i, 0)),
                   pl.BlockSpec((2 * TOP_K, tile), lambda i: (0, i)), pl.BlockSpec((2 * TOP_K, tile), lambda i: (0, i)),
                   meta, meta, meta, pl.BlockSpec((n_exp, LANE), lambda i: (0, 0))],
        out_shape=[jax.ShapeDtypeStruct((n, d), F32), jax.ShapeDtypeStruct((n // ts * lt, d), F32),
                   jax.ShapeDtypeStruct((2 * TOP_K, n), I32), jax.ShapeDtypeStruct((2 * TOP_K, n), F32),
                   meta_shape, meta_shape, meta_shape, jax.ShapeDtypeStruct((n_exp, LANE), F32)],
        scratch_shapes=[pltpu.VMEM((n_exp, 1), F32)],
        compiler_params=_params("arbitrary"),
        name="finish1",
    )(x2d, oa, oc, om, g_mix, wout_bf, ln_g, ln_b, w_rT, b_r)


def _run_sizes(ts):
    sizes, s = [], SUBLANE
    while s <= max(ts, SUBLANE):
        sizes.append(s)
        s *= 2
    return sizes


def _for_each_piece(length, sizes, fn):
    off = 0
    for sz in sizes:
        @pl.when((length & sz) != 0)
        def _(off=off, sz=sz):
            fn(off, sz)
        off = off + (length & sz)


def _dispatch_kernel(cpad_ref, loff_ref, base_ref, fill_ref, xs_ref, zero_ref, buf_ref, sem, fill_sem,
                     *, lt, n_exp, sizes, tm):
    t = pl.program_id(0)

    def run(e, wait):
        n = cpad_ref[t * n_exp + e]
        src = loff_ref[t * n_exp + e]
        dst = base_ref[t * n_exp + e]

        def piece(off, sz):
            cp = pltpu.make_async_copy(xs_ref.at[pl.ds(pl.multiple_of(src + off, SUBLANE), sz), :],
                                       buf_ref.at[pl.ds(pl.multiple_of(dst + off, SUBLANE), sz), :], sem)
            cp.wait() if wait else cp.start()

        _for_each_piece(n, sizes, piece)

    def issue(e, c):
        run(e, False)
        return c

    def drain(e, c):
        run(e, True)
        return c

    lax.fori_loop(0, n_exp, issue, 0)

    @pl.when(t == pl.num_programs(0) - 1)
    def _():
        start = fill_ref[0]
        rest = buf_ref.shape[0] - start
        n_full = rest // tm

        def chunk(c, wait):
            cp = pltpu.make_async_copy(zero_ref, buf_ref.at[pl.ds(pl.multiple_of(start + c * tm, SUBLANE), tm), :],
                                       fill_sem)
            cp.wait() if wait else cp.start()

        def piece(wait):
            def fn(off, sz):
                cp = pltpu.make_async_copy(
                    zero_ref.at[pl.ds(0, sz), :],
                    buf_ref.at[pl.ds(pl.multiple_of(start + n_full * tm + off, SUBLANE), sz), :], fill_sem)
                cp.wait() if wait else cp.start()
            return fn

        tail_sizes = [s for s in _run_sizes(tm) if s < tm]
        for wait in (False, True):
            lax.fori_loop(0, n_full, lambda c, carry, wait=wait: (chunk(c, wait), carry)[1], 0)
            _for_each_piece(rest - n_full * tm, tail_sizes, piece(wait))

    lax.fori_loop(0, n_exp, drain, 0)


def _dispatch(cpad, loff, base, fill, xs, *, ts, n_exp, m_pad, tm):
    w = xs.shape[1]
    lt = _sort_rows(ts, n_exp)
    n_sub = xs.shape[0] // lt
    zero = jnp.zeros((tm, w), F32)
    return pl.pallas_call(
        functools.partial(_dispatch_kernel, lt=lt, n_exp=n_exp, sizes=_run_sizes(ts), tm=tm),
        grid_spec=pltpu.PrefetchScalarGridSpec(
            num_scalar_prefetch=4,
            grid=(n_sub,),
            in_specs=[pl.BlockSpec((lt, w), lambda t, *_: (t, 0)), pl.BlockSpec((tm, w), lambda t, *_: (0, 0))],
            out_specs=pl.BlockSpec(memory_space=pl.ANY),
            scratch_shapes=[pltpu.SemaphoreType.DMA(()), pltpu.SemaphoreType.DMA(())]),
        out_shape=jax.ShapeDtypeStruct((m_pad, w), F32),
        compiler_params=_params("arbitrary"),
        name="dispatch",
    )(cpad, loff, base, fill, xs, zero)


def _ffn_kernel(otile_ref, tile_ref, exp_ref, lo_ref, hi_ref, lhs_ref, wgu_ref, bgu_ref, wd_ref, bd_ref, out_ref,
                *, tm, d_ff):
    g = pl.program_id(0)
    lo = lo_ref[g]
    hi = hi_ref[g]

    @pl.when(hi == lo)
    def _():
        out_ref[...] = jnp.zeros_like(out_ref)

    @pl.when(hi > lo)
    def _():
        gu = _dot(lhs_ref[...].astype(BF16), wgu_ref[0]) + bgu_ref[0]
        gate = jnp.minimum(gu[:, :d_ff], SWIGLU_LIMIT)
        up = jnp.clip(gu[:, d_ff:], -SWIGLU_LIMIT, SWIGLU_LIMIT)
        hid = (up + 1.0) * gate * jax.nn.sigmoid(SWIGLU_ALPHA * gate)
        o = _dot(hid.astype(BF16), wd_ref[0]) + bd_ref[0]

        @pl.when(lo == 0)
        def _():
            out_ref[...] = o

        @pl.when(lo > 0)
        def _():
            rows = lax.broadcasted_iota(I32, (tm, 1), 0)
            out_ref[...] = jnp.where((rows >= lo) & (rows < hi), o, out_ref[...])


def _ffn(sched, buf, wgu_bf, b_gu, wd_bf, b_d, *, tm):
    m, w = buf.shape
    n_exp, d, d_ff2 = wgu_bf.shape
    out_tiles, tiles, experts, los, his = sched
    return pl.pallas_call(
        functools.partial(_ffn_kernel, tm=tm, d_ff=d_ff2 // 2),
        grid_spec=pltpu.PrefetchScalarGridSpec(
            num_scalar_prefetch=5,
            grid=(tiles.shape[0],),
            in_specs=[pl.BlockSpec((tm, w), lambda g, ot, t, e, lo, hi: (t[g], 0)),
                      pl.BlockSpec((1, d, d_ff2), lambda g, ot, t, e, lo, hi: (e[g], 0, 0)),
                      pl.BlockSpec((1, 1, d_ff2), lambda g, ot, t, e, lo, hi: (e[g], 0, 0)),
                      pl.BlockSpec((1, d_ff2 // 2, d), lambda g, ot, t, e, lo, hi: (e[g], 0, 0)),
                      pl.BlockSpec((1, 1, d), lambda g, ot, t, e, lo, hi: (e[g], 0, 0))],
            out_specs=pl.BlockSpec((tm, d), lambda g, ot, t, e, lo, hi: (ot[g], 0))),
        out_shape=jax.ShapeDtypeStruct((m, d), F32),
        compiler_params=_params("arbitrary"),
        name="expert_ffn",
    )(out_tiles, tiles, experts, los, his, buf, wgu_bf, b_gu, wd_bf, b_d)


def _moe_schedule(counts, n_rows, tm):
    n_exp = counts.shape[0]
    n_tiles = n_rows // tm
    n_items = n_tiles + n_exp - 1
    ends = jnp.cumsum(counts)
    starts = ends - counts
    first_tile = starts // tm
    n_it = jnp.where(counts > 0, (ends - 1) // tm - first_tile + 1, 0)
    it_end = jnp.cumsum(n_it)
    it_start = it_end - n_it
    item = jnp.arange(n_items, dtype=I32)
    n_real = it_end[-1]
    g = jnp.minimum(item, n_real - 1)
    e = jnp.minimum(jnp.sum((it_end[None, :] <= g[:, None]).astype(I32), axis=1), n_exp - 1)
    of_e = e[:, None] == jnp.arange(n_exp, dtype=I32)[None, :]
    at_e = lambda table: jnp.sum(jnp.where(of_e, table[None, :], 0), axis=1)
    tile = (at_e(first_tile) + g - at_e(it_start)).astype(I32)
    valid = item < n_real
    lo = jnp.where(valid, jnp.clip(at_e(starts) - tile * tm, 0, tm), 0).astype(I32)
    hi = jnp.where(valid, jnp.clip(at_e(ends) - tile * tm, 0, tm), 0).astype(I32)
    used = (ends[-1] + tm - 1) // tm
    out_tile = jnp.where(valid, tile, jnp.minimum(used + item - n_real, n_tiles - 1)).astype(I32)
    return (out_tile, tile, e, lo, hi), starts.astype(I32)


def _combine_kernel(cpad_ref, loff_ref, base_ref, route_ref, gw_ref, x1_ref, g2_ref, b2_ref, eo_ref, y_ref,
                    gbuf, sem, *, ts, n_exp, sizes, alpha):
    t = pl.program_id(0)
    nt = pl.num_programs(0)
    lt = gbuf.shape[1]
    d = gbuf.shape[2]

    def tile_copies(tt, slot, wait):
        def run(e, c):
            n = cpad_ref[tt * n_exp + e]
            src = base_ref[tt * n_exp + e]
            dst = loff_ref[tt * n_exp + e]

            def piece(off, sz):
                cp = pltpu.make_async_copy(eo_ref.at[pl.ds(pl.multiple_of(src + off, SUBLANE), sz), :],
                                           gbuf.at[slot, pl.ds(pl.multiple_of(dst + off, SUBLANE), sz), :],
                                           sem.at[slot])
                cp.wait() if wait else cp.start()

            _for_each_piece(n, sizes, piece)
            return c

        lax.fori_loop(0, n_exp, run, 0)

    def fetch(tt, slot):
        gbuf[slot, TOP_K * ts:, :] = jnp.zeros((lt - TOP_K * ts, d), F32)
        tile_copies(tt, slot, False)

    slot = t % 2

    @pl.when(t == 0)
    def _():
        fetch(t, slot)

    @pl.when(t + 1 < nt)
    def _():
        fetch(t + 1, 1 - slot)

    tile_copies(t, slot, True)

    r = lax.broadcasted_iota(I32, (ts, ts), 0)
    c = lax.broadcasted_iota(I32, (ts, ts), 1)
    eye = jnp.where(r == c, 1.0, 0.0)
    wcol = _dot_nt(eye, gw_ref[...], precision=HI)
    pcol = _dot_nt(eye, route_ref[...].astype(F32), precision=HI)
    rows = gbuf[slot].astype(BF16)
    slot_i = lax.broadcasted_iota(I32, (ts, lt), 1)
    unsort = jnp.zeros((ts, lt), F32)
    for k in range(TOP_K):
        lpos = pcol[:, TOP_K + k:TOP_K + k + 1].astype(I32)
        unsort = unsort + jnp.where(slot_i == lpos, wcol[:, k:k + 1], 0.0)
    hi = unsort.astype(BF16)
    lo = (unsort - hi.astype(F32)).astype(BF16)
    moe = _dot(hi, rows) + _dot(lo, rows)
    y_ref[...] = _layernorm(alpha * x1_ref[...] + moe, g2_ref[...], b2_ref[...])


def _combine(cpad, loff, base, route, gw, x1, ln_g, ln_b, eo, *, ts, n_exp, alpha):
    n, d = x1.shape
    lt = _sort_rows(ts, n_exp)
    full = lambda a: pl.BlockSpec(a.shape, lambda i, *_: (0,) * a.ndim)
    return pl.pallas_call(
        functools.partial(_combine_kernel, ts=ts, n_exp=n_exp, sizes=_run_sizes(ts), alpha=alpha),
        grid_spec=pltpu.PrefetchScalarGridSpec(
            num_scalar_prefetch=3,
            grid=(n // ts,),
            in_specs=[pl.BlockSpec((2 * TOP_K, ts), lambda i, *_: (0, i)),
                      pl.BlockSpec((2 * TOP_K, ts), lambda i, *_: (0, i)),
                      pl.BlockSpec((ts, d), lambda i, *_: (i, 0)), full(ln_g), full(ln_b),
                      pl.BlockSpec(memory_space=pl.ANY)],
            out_specs=pl.BlockSpec((ts, d), lambda i, *_: (i, 0)),
            scratch_shapes=[pltpu.VMEM((2, lt, d), F32), pltpu.SemaphoreType.DMA((2,))]),
        out_shape=jax.ShapeDtypeStruct((n, d), F32),
        compiler_params=_params("arbitrary"),
        name="combine",
    )(cpad, loff, base, route, gw, x1, ln_g, ln_b, eo)


def _finish(x2d, oa, oc, om, wts, w_out, *, tile, tm, alpha):
    n = x2d.shape[0]
    n_exp = wts["w_rT"].shape[0]
    ts = _pick_tile(n, SORT_TILE)
    n_sub = n // ts
    x1, xs, route, gw, cpad, loff, cbase, tot = _finish1(
        x2d, oa, oc, om, wts["g_mix"], w_out, wts["ln1_g"], wts["ln1_b"], wts["w_rT"], wts["b_r"],
        tile=_pick_tile(n, tile), ts=ts, alpha=alpha)
    m_pad = n_sub * _sort_rows(ts, n_exp) + tm
    counts = tot[:, 0].astype(I32)
    sched, offs = _moe_schedule(counts, m_pad, tm)
    flat = lambda a: a[:, :, 0].astype(I32).reshape(-1)
    cpad, loff = flat(cpad), flat(loff)
    base = (cbase[:, :, 0].astype(I32) + offs[None, :]).reshape(-1)
    fill = jnp.sum(counts, keepdims=True)
    buf = _dispatch(cpad, loff, base, fill, xs, ts=ts, n_exp=n_exp, m_pad=m_pad, tm=tm)
    eo = _ffn(sched, buf, wts["w_gu"], wts["b_gu"], wts["w_d"], wts["b_d"], tm=tm)
    return _combine(cpad, loff, base, route, gw, x1, wts["ln2_g"], wts["ln2_b"], eo, ts=ts, n_exp=n_exp, alpha=alpha)


def _sample_inproj_kernel(x_ref, w_ref, wc_ref, p0_ref, p1_ref, q_ref, k_ref, v_ref, qm_ref, oc_ref, u_ref,
                          *, att_w, mem_w, conv_w):
    x = x_ref[...]
    c1, c2, c3 = att_w, 2 * att_w, 3 * att_w
    c4 = c3 + mem_w
    c5, c6, c7 = c4 + conv_w, c4 + 2 * conv_w, c4 + 3 * conv_w

    def proj(lo, hi):
        return _dot(x, w_ref[:, lo:hi], precision=HI)

    q_ref[...] = proj(0, c1) * Q_SCALE
    k_ref[...] = proj(c1, c2)
    v_ref[...] = proj(c2, c3)
    qm_ref[...] = proj(c3, c4) * Q_SCALE
    u = proj(c5, c6) * proj(c6, c7)
    cz = wc_ref[0:1, :] * p0_ref[...] + wc_ref[1:2, :] * p1_ref[...] + wc_ref[2:3, :] * u
    oc_ref[...] = proj(c4, c5) * cz
    u_ref[...] = u


def _sample_inproj(x2d, w_bf, w_conv, prev0, prev1, *, att_w, mem_w, conv_w):
    n = x2d.shape[0]
    args = (x2d, w_bf, w_conv, prev0, prev1)
    full = lambda a: pl.BlockSpec(a.shape, lambda i: (0,) * a.ndim)
    widths = (att_w, att_w, att_w, mem_w, conv_w, conv_w)
    return pl.pallas_call(
        functools.partial(_sample_inproj_kernel, att_w=att_w, mem_w=mem_w, conv_w=conv_w),
        grid=(1,),
        in_specs=[full(a) for a in args],
        out_specs=[pl.BlockSpec((n, w), lambda i: (0, 0)) for w in widths],
        out_shape=[jax.ShapeDtypeStruct((n, w), F32) for w in widths],
        compiler_params=_params("arbitrary"),
        name="sample_inproj",
    )(*args)


def _kscan_kernel(pt_ref, q_ref, ck_ref, sel_ref, kbuf, ksum, sem, *, n_pages, chunk, n_heads):
    b = pl.program_id(0)
    n_chunks = n_pages // chunk
    pages_per_blk = MOBA_BLOCK // PAGE_SIZE
    nblk = ksum.shape[0]
    q_col = q_ref[0]

    def copies(bb, c, slot):
        return [pltpu.make_async_copy(ck_ref.at[0, pt_ref[bb * n_pages + c * chunk + j]], kbuf.at[slot, j],
                                      sem.at[slot]) for j in range(chunk)]

    @pl.when(b == 0)
    def _():
        for cp in copies(b, 0, 0):
            cp.start()

    def body(c, carry):
        slot = c % 2
        last = c + 1 == n_chunks

        @pl.when(jnp.logical_or(jnp.logical_not(last), b + 1 < pl.num_programs(0)))
        def _():
            for cp in copies(jnp.where(last, b + 1, b), jnp.where(last, 0, c + 1), 1 - slot):
                cp.start()

        for cp in copies(b, c, slot):
            cp.wait()
        for jb in range(chunk // pages_per_blk):
            s = kbuf[slot, pages_per_blk * jb]
            for pg in range(1, pages_per_blk):
                s = s + kbuf[slot, pages_per_blk * jb + pg]
            ksum[c * (chunk // pages_per_blk) + jb] = jnp.sum(s * q_col, axis=1)
        return carry

    lax.fori_loop(0, n_chunks, body, 0)

    g = jnp.sum(ksum[...], axis=2, keepdims=True) * (1.0 / MOBA_BLOCK)
    blk = lax.broadcasted_iota(I32, g.shape, 0)
    sel_ref[...] = jnp.zeros(sel_ref.shape, I32)
    for r in range(MOBA_TOPK):
        m = jnp.max(g, axis=0, keepdims=True)
        idx = jnp.min(jnp.where((g == m) & (g > -jnp.inf), blk, nblk), axis=0, keepdims=True)
        sel_ref[0, :, r:r + 1] = idx[0]
        g = jnp.where(blk == idx, -jnp.inf, g)


def _kscan(pt_flat, q_col, ckT, *, n_pages, chunk):
    bs, n_heads, hd, _ = q_col.shape
    nblk = n_pages * PAGE_SIZE // MOBA_BLOCK
    return pl.pallas_call(
        functools.partial(_kscan_kernel, n_pages=n_pages, chunk=chunk, n_heads=n_heads),
        grid_spec=pltpu.PrefetchScalarGridSpec(
            num_scalar_prefetch=1,
            grid=(bs,),
            in_specs=[pl.BlockSpec((1, n_heads, hd, 1), lambda b, pt: (b, 0, 0, 0)), pl.BlockSpec(memory_space=pl.ANY)],
            out_specs=pl.BlockSpec((1, n_heads, LANE), lambda b, pt: (b, 0, 0)),
            scratch_shapes=[pltpu.VMEM((2, chunk, n_heads, hd, PAGE_SIZE), F32),
                            pltpu.VMEM((nblk, n_heads, PAGE_SIZE), F32), pltpu.SemaphoreType.DMA((2,))]),
        out_shape=jax.ShapeDtypeStruct((bs, n_heads, LANE), I32),
        compiler_params=_params("arbitrary"),
        name="kscan",
    )(pt_flat, q_col, ckT)


def _sample_attn_kernel(pt_ref, sel_ref, slopes_ref, q_ref, kn_ref, vn_ref, qm_ref, mk_ref, mv_ref, ck_ref, cv_ref,
                        oa_ref, om_ref, kbuf, vbuf, sem, *, n_pages, n_heads, past_len, mem_w):
    b = pl.program_id(0)
    nb = pl.num_programs(0)
    B = MOBA_BLOCK
    pages_per_blk = B // PAGE_SIZE
    n_keys = MOBA_TOPK * B

    def blocks(bb, h):
        return [sel_ref[(bb * n_heads + h) * MOBA_TOPK + s] for s in range(MOBA_TOPK)]

    n_sel_pages = MOBA_TOPK * pages_per_blk

    def copies(bb, slot):
        cps = []
        for h in range(n_heads):
            blks = blocks(bb, h)
            for s in range(MOBA_TOPK):
                for half in range(pages_per_blk):
                    pg = pt_ref[bb * n_pages + pages_per_blk * blks[s] + half]
                    pp = s * pages_per_blk + half
                    cps.append(pltpu.make_async_copy(ck_ref.at[0, pg, h], kbuf.at[slot, h, pp], sem.at[0, slot]))
                    cps.append(pltpu.make_async_copy(cv_ref.at[0, pg, h], vbuf.at[slot, h, pp], sem.at[1, slot]))
        return cps

    slot = b % 2

    @pl.when(b == 0)
    def _():
        for cp in copies(b, slot):
            cp.start()

    @pl.when(b + 1 < nb)
    def _():
        for cp in copies(b + 1, 1 - slot):
            cp.start()

    for cp in copies(b, slot):
        cp.wait()

    key_lane = lax.broadcasted_iota(I32, (1, n_keys), 1)
    lane = lax.broadcasted_iota(I32, (1, LANE), 1)
    for h in range(n_heads):
        blks = blocks(b, h)
        qh = q_ref[0, h:h + 1, :]
        q8 = jnp.broadcast_to(qh, (SUBLANE, HEAD_DIM))
        blk_of_key = jnp.where(key_lane < B, blks[0], jnp.where(key_lane < 2 * B, blks[1], blks[2]))
        dist = (past_len - blk_of_key * B - (key_lane & (B - 1))).astype(F32)
        s = jnp.concatenate([_dot(q8.astype(BF16), kbuf[slot, h, pp].astype(BF16)) for pp in range(n_sel_pages)],
                            axis=1)
        s = s - slopes_ref[h] * dist
        s_self = jnp.sum(qh * kn_ref[0, h:h + 1, :], axis=1, keepdims=True)
        m = jnp.maximum(jnp.max(s, axis=1, keepdims=True), s_self)
        e = jnp.exp(s - m)
        e_self = jnp.exp(s_self - m)
        l = jnp.sum(e, axis=1, keepdims=True) + e_self
        o = e_self * vn_ref[0, h:h + 1, :]
        for pp in range(n_sel_pages):
            o = o + _dot_nt(e[:, pp * PAGE_SIZE:(pp + 1) * PAGE_SIZE].astype(BF16), vbuf[slot, h, pp].astype(BF16))
        oa_ref[0, h:h + 1, :] = (o / l)[0:1, :]

    qm_row = qm_ref[pl.ds(b, 1), :]
    outs = []
    for pr in range(mem_w // LANE):
        cs = slice(pr * LANE, (pr + 1) * LANE)
        mkp = mk_ref[0, :, cs]
        mvp = mv_ref[0, :, cs]
        pair = []
        for hh in range(2):
            qh = jnp.where(_head_mask((1, LANE), hh), qm_row[:, cs], 0.0)
            s = _dot_nt(jnp.broadcast_to(qh, (SUBLANE, LANE)), mkp, precision=HI)
            e = jnp.exp(s - jnp.max(s, axis=1, keepdims=True))
            o = _dot(e, mvp, precision=HI) / jnp.sum(e, axis=1, keepdims=True)
            pair.append(o[0:1, :])
        outs.append(jnp.where(lane < HEAD_DIM, pair[0], pair[1]))
    om_ref[0] = jnp.concatenate(outs, axis=1)


def _sample_attn(pt_flat, sel_flat, slopes, q3, kn3, vn3, qm, mk, mv, ck, cv, *, n_pages):
    bs, n_heads, hd = q3.shape
    _, n_mem, mem_w = mk.shape
    n_keys = MOBA_TOPK * MOBA_BLOCK
    per_b = lambda a: pl.BlockSpec((1,) + a.shape[1:], lambda b, pt, sel: (b,) + (0,) * (a.ndim - 1))
    return pl.pallas_call(
        functools.partial(_sample_attn_kernel, n_pages=n_pages, n_heads=n_heads, past_len=n_pages * PAGE_SIZE,
                          mem_w=mem_w),
        grid_spec=pltpu.PrefetchScalarGridSpec(
            num_scalar_prefetch=2,
            grid=(bs,),
            in_specs=[pl.BlockSpec(memory_space=pltpu.SMEM), per_b(q3), per_b(kn3), per_b(vn3),
                      pl.BlockSpec(qm.shape, lambda b, pt, sel: (0, 0)), per_b(mk), per_b(mv),
                      pl.BlockSpec(memory_space=pl.ANY), pl.BlockSpec(memory_space=pl.ANY)],
            out_specs=[pl.BlockSpec((1, n_heads, hd), lambda b, pt, sel: (b, 0, 0)),
                       pl.BlockSpec((1, 1, mem_w), lambda b, pt, sel: (b, 0, 0))],
            scratch_shapes=[pltpu.VMEM((2, n_heads, n_keys // PAGE_SIZE, hd, PAGE_SIZE), F32),
                            pltpu.VMEM((2, n_heads, n_keys // PAGE_SIZE, hd, PAGE_SIZE), F32),
                            pltpu.SemaphoreType.DMA((2, 2))]),
        out_shape=[jax.ShapeDtypeStruct((bs, n_heads, hd), F32), jax.ShapeDtypeStruct((bs, 1, mem_w), F32)],
        compiler_params=_params("arbitrary"),
        name="sample_attn",
    )(pt_flat, sel_flat, slopes, q3, kn3, vn3, qm, mk, mv, ck, cv)


def _sample_layer(x, cache_k, cache_v, mem_k, mem_v, state, page_table, wts, dims, alpha):
    bs, dec_seq, d = x.shape
    assert dec_seq == 1
    att_w, mem_w, conv_w = dims
    n_heads = att_w // HEAD_DIM
    n_pages = page_table.shape[1]
    assert (n_pages * PAGE_SIZE) % MOBA_BLOCK == 0 and n_pages * PAGE_SIZE // MOBA_BLOCK >= MOBA_TOPK
    x2d = x.reshape(bs, d)
    q, k, v, qm, oc, u = _sample_inproj(x2d, wts["w_in_f32"], wts["w_conv"], state[:, 0, :], state[:, 1, :],
                                        att_w=att_w, mem_w=mem_w, conv_w=conv_w)
    pt_flat = page_table.reshape(-1)
    heads = lambda a: a.reshape(bs, n_heads, HEAD_DIM)
    pagesT = lambda c: jnp.transpose(c, (0, 1, 3, 4, 2))
    chunk = 16 if n_pages % 32 == 0 else 8
    assert n_pages % (2 * chunk) == 0
    sel = _kscan(pt_flat, q.reshape(bs, n_heads, HEAD_DIM, 1), pagesT(cache_k), n_pages=n_pages, chunk=chunk)
    sel_flat = sel[:, :, :MOBA_TOPK].reshape(-1)
    oa, om = _sample_attn(pt_flat, sel_flat, wts["slopes"], heads(q), heads(k), heads(v), qm,
                          mem_k.reshape(bs, -1, mem_w), mem_v.reshape(bs, -1, mem_w), pagesT(cache_k),
                          pagesT(cache_v), n_pages=n_pages)
    y = _finish(x2d, oa.reshape(bs, att_w), oc, om.reshape(bs, mem_w), wts, wts["w_out_f32"], tile=bs,
                tm=bs * TOP_K, alpha=alpha)
    conv_state = jnp.stack([state[:, 1, :], u], axis=1)
    return y.reshape(bs, 1, d), k, v, conv_state


def _pick_tile(n, pref):
    return pref if n % pref == 0 else n


def _prompt_layer(x, mem, wts, dims, alpha):
    batch, seq, d = x.shape
    att_w, mem_w, conv_w = dims
    n = batch * seq
    n_mem = mem.shape[1]
    x2d = x.reshape(n, d)
    tile = 512
    q, k, v, kT, vT, kmean, qm, oc, tail = _inproj(x2d, wts["w_in"], wts["w_conv"], seq=seq, tile=tile, att_w=att_w,
                                                   mem_w=mem_w, conv_w=conv_w)
    oa = _moba_prompt(q, k, v, kmean.reshape(n // MOBA_BLOCK, att_w), wts["slopes"], batch=batch, seq=seq)
    mk, mv = _memkv(mem.reshape(batch * n_mem, d), wts["w_mem_kv"], tile=_pick_tile(batch * n_mem, 512), mem_w=mem_w)
    om = _memattn_prompt(qm, mk, mv, batch=batch, seq=seq, n_mem=n_mem, tq=512)
    y = _finish(x2d, oa, oc, om, wts, wts["w_out"], tile=512, tm=256, alpha=alpha)
    conv_state = tail.reshape(batch, seq // tile, SUBLANE, conv_w)[:, -1, SUBLANE - (CONV_K - 1):, :]
    rows = lambda t: jnp.transpose(t, (0, 3, 1, 2))
    return y.reshape(batch, seq, d), rows(kT), rows(vT), conv_state, mk, mv


def _prep_weights(l, w_in, w_mem_kv, w_conv, g_mix, w_out, ln1_g, ln1_b, w_router, b_router, w_gate_up, b_gate_up,
                  w_down, b_down, ln2_g, ln2_b):
    n_heads = 8
    row = lambda a: a[l][None, :]
    return {
        "w_in": w_in[l].astype(BF16), "w_in_f32": w_in[l], "w_mem_kv": w_mem_kv[l].astype(BF16), "w_conv": w_conv[l],
        "g_mix": row(g_mix), "w_out": w_out[l].astype(BF16), "w_out_f32": w_out[l], "ln1_g": row(ln1_g),
        "ln1_b": row(ln1_b),
        "w_rT": w_router[l].T, "b_r": b_router[l][:, None],
        "w_gu": w_gate_up[l].astype(BF16), "b_gu": b_gate_up[l][:, None, :],
        "w_d": w_down[l].astype(BF16), "b_d": b_down[l][:, None, :],
        "ln2_g": row(ln2_g), "ln2_b": row(ln2_b),
        "slopes": 2.0 ** (-8.0 * jnp.arange(1, n_heads + 1, dtype=F32) / n_heads),
    }


def kernel(x_prompt, x_sample, cache_k, cache_v, cache_mem_k, cache_mem_v, state_conv, page_table, mem_prompt, w_in, w_mem_kv, w_conv, g_mix, w_out, ln1_g, ln1_b, w_router, b_router, w_gate_up, b_gate_up, w_down, b_down, ln2_g, ln2_b):
    depth = w_in.shape[0]
    assert depth == 1
    alpha = (2 * depth) ** 0.25
    n_heads = cache_k.shape[3]
    att_w = n_heads * HEAD_DIM
    mem_w = cache_mem_k.shape[3] * HEAD_DIM
    conv_w = state_conv.shape[3]
    dims = (att_w, mem_w, conv_w)
    wts = _prep_weights(0, w_in, w_mem_kv, w_conv, g_mix, w_out, ln1_g, ln1_b, w_router, b_router, w_gate_up,
                        b_gate_up, w_down, b_down, ln2_g, ln2_b)
    bp, seq, d = x_prompt.shape
    n_mem = mem_prompt.shape[1]
    y_p, k_p, v_p, conv_p, mk_p, mv_p = _prompt_layer(x_prompt, mem_prompt, wts, dims, alpha)
    bs = x_sample.shape[0]
    y_s, k_s, v_s, conv_s = _sample_layer(x_sample, cache_k, cache_v, cache_mem_k[0], cache_mem_v[0],
                                          state_conv[0], page_table, wts, dims, alpha)
    return (y_p, y_s,
            k_p[None], v_p[None], conv_p[None], mk_p.reshape(1, bp, n_mem, mem_w // HEAD_DIM, HEAD_DIM),
            mv_p.reshape(1, bp, n_mem, mem_w // HEAD_DIM, HEAD_DIM),
            k_s.reshape(1, bs, 1, n_heads, HEAD_DIM), v_s.reshape(1, bs, 1, n_heads, HEAD_DIM), conv_s[None])
```

```python
import functools

import jax
import jax.numpy as jnp
from jax import lax
from jax.experimental import pallas as pl
from jax.experimental.pallas import tpu as pltpu

F32 = jnp.float32
BF16 = jnp.bfloat16
I32 = jnp.int32

HEAD_DIM = 64
MOBA_BLOCK = 256
MOBA_TOPK = 3
PAGE_SIZE = 128
TOP_K = 4
CONV_K = 3
SWIGLU_LIMIT = 7.0
SWIGLU_ALPHA = 1.702
LN_EPS = 1e-5
Q_SCALE = HEAD_DIM ** -0.5
NEG = -1e30
LANE = 128
SUBLANE = 8
VMEM_LIMIT = 56 * 1024 * 1024
HI = lax.Precision.HIGHEST


def _params(*sem):
    return pltpu.CompilerParams(dimension_semantics=sem, vmem_limit_bytes=VMEM_LIMIT)


def _dot_nt(a, b, precision=None):
    return lax.dot_general(a, b, (((1,), (1,)), ((), ())), precision=precision, preferred_element_type=F32)


def _dot(a, b, precision=None):
    return jnp.dot(a, b, precision=precision, preferred_element_type=F32)


def _layernorm(z, g, b):
    zc = z - jnp.mean(z, axis=-1, keepdims=True)
    var = jnp.mean(zc * zc, axis=-1, keepdims=True)
    return zc * lax.rsqrt(var + LN_EPS) * g + b


def _rms(a):
    return a * lax.rsqrt(jnp.mean(a * a, axis=-1, keepdims=True) + LN_EPS)


def _head_mask(shape, hh):
    lane = lax.broadcasted_iota(I32, shape, len(shape) - 1)
    return (lane >= HEAD_DIM * hh) & (lane < HEAD_DIM * (hh + 1))


def _inproj_kernel(x_ref, w_ref, wc_ref, q_ref, k_ref, v_ref, kt_ref, vt_ref, km_ref, qm_ref, oc_ref, tail_ref, ubuf,
                   *, tile, tiles_per_seq, att_w, mem_w, conv_w):
    i = pl.program_id(0)
    x = x_ref[...].astype(BF16)
    c1, c2, c3 = att_w, 2 * att_w, 3 * att_w
    c4 = c3 + mem_w
    c5, c6, c7 = c4 + conv_w, c4 + 2 * conv_w, c4 + 3 * conv_w
    n_heads = att_w // HEAD_DIM

    def proj(lo, hi):
        return _dot(x, w_ref[:, lo:hi])

    q_ref[...] = (proj(0, c1) * Q_SCALE).astype(BF16)
    k = proj(c1, c2)
    k_ref[...] = k
    kt_ref[0] = k.T.reshape(n_heads, HEAD_DIM, tile)
    km_ref[0] = jnp.sum(k.reshape(tile // MOBA_BLOCK, MOBA_BLOCK, att_w), axis=1) * (1.0 / MOBA_BLOCK)
    v = proj(c2, c3)
    v_ref[...] = v
    vt_ref[0] = v.T.reshape(n_heads, HEAD_DIM, tile)
    qm_ref[...] = (proj(c3, c4) * Q_SCALE).astype(BF16)
    gb = proj(c4, c5)
    u = proj(c5, c6) * proj(c6, c7)

    @pl.when(i % tiles_per_seq == 0)
    def _():
        ubuf[0:SUBLANE, :] = jnp.zeros((SUBLANE, conv_w), F32)

    ubuf[SUBLANE:SUBLANE + tile, :] = u
    u1 = ubuf[SUBLANE - 1:SUBLANE - 1 + tile, :]
    u2 = ubuf[SUBLANE - 2:SUBLANE - 2 + tile, :]
    cz = wc_ref[0:1, :] * u2 + wc_ref[1:2, :] * u1 + wc_ref[2:3, :] * u
    oc_ref[...] = gb * cz
    tail = ubuf[tile:tile + SUBLANE, :]
    tail_ref[0] = tail
    ubuf[0:SUBLANE, :] = tail


def _inproj(x2d, w_bf, w_conv, *, seq, tile, att_w, mem_w, conv_w):
    n, d = x2d.shape
    nt = n // tile
    kern = functools.partial(_inproj_kernel, tile=tile, tiles_per_seq=seq // tile, att_w=att_w, mem_w=mem_w,
                             conv_w=conv_w)
    row = lambda w: pl.BlockSpec((tile, w), lambda i: (i, 0))
    tps = seq // tile
    n_heads = att_w // HEAD_DIM
    headsT = pl.BlockSpec((1, n_heads, HEAD_DIM, tile), lambda i: (i // tps, 0, 0, i % tps))
    headsT_shape = jax.ShapeDtypeStruct((n // seq, n_heads, HEAD_DIM, seq), F32)
    return pl.pallas_call(
        kern,
        grid=(nt,),
        in_specs=[row(d), pl.BlockSpec(w_bf.shape, lambda i: (0, 0)), pl.BlockSpec(w_conv.shape, lambda i: (0, 0))],
        out_specs=[row(att_w), row(att_w), row(att_w), headsT, headsT,
                   pl.BlockSpec((1, tile // MOBA_BLOCK, att_w), lambda i: (i, 0, 0)),
                   row(mem_w), row(conv_w), pl.BlockSpec((1, SUBLANE, conv_w), lambda i: (i, 0, 0))],
        out_shape=[jax.ShapeDtypeStruct((n, att_w), BF16), jax.ShapeDtypeStruct((n, att_w), F32),
                   jax.ShapeDtypeStruct((n, att_w), F32), headsT_shape, headsT_shape,
                   jax.ShapeDtypeStruct((nt, tile // MOBA_BLOCK, att_w), F32),
                   jax.ShapeDtypeStruct((n, mem_w), BF16), jax.ShapeDtypeStruct((n, conv_w), F32),
                   jax.ShapeDtypeStruct((nt, SUBLANE, conv_w), F32)],
        scratch_shapes=[pltpu.VMEM((tile + SUBLANE, conv_w), F32)],
        compiler_params=_params("arbitrary"),
        name="inproj",
    )(x2d, w_bf, w_conv)


def _moba_kernel(slopes_ref, q_ref, k_ref, v_ref, km_ref, o_ref, kbf, vT, q2_s, m_s, l_s, acc_s, *, nblk):
    p = pl.program_id(1)
    B = MOBA_BLOCK
    W = 2 * B

    for c in range(nblk):
        kbf[c] = k_ref[c * B:(c + 1) * B, :].astype(BF16)
        vT[c] = v_ref[c * B:(c + 1) * B, :].T.astype(BF16)
        q = q_ref[c * B:(c + 1) * B, :]
        zero = jnp.zeros_like(q)
        q2_s[c * W:c * W + B, :] = jnp.where(_head_mask(q.shape, 0), q, zero)
        q2_s[c * W + B:(c + 1) * W, :] = jnp.where(_head_mask(q.shape, 1), q, zero)

    second = lax.broadcasted_iota(I32, (1, W), 1) >= B
    slope = jnp.where(second, slopes_ref[2 * p + 1], slopes_ref[2 * p])
    key_i = lax.broadcasted_iota(I32, (B, W), 0)
    qry_i = lax.broadcasted_iota(I32, (B, W), 1) & (B - 1)
    base = slope * (key_i - qry_i).astype(F32)
    causal = key_i <= qry_i

    blk = lax.broadcasted_iota(I32, (nblk, nblk * W), 0)
    tile_of_lane = lax.broadcasted_iota(I32, (nblk, nblk * W), 1) >> (W.bit_length() - 1)
    g = _dot_nt(km_ref[...], q2_s[...].astype(F32), precision=HI)
    g = jnp.where(blk < tile_of_lane, g, -jnp.inf)
    sel_bias = jnp.full((nblk, nblk * W), NEG, F32)
    for _ in range(MOBA_TOPK):
        m = jnp.max(g, axis=0, keepdims=True)
        cand = (g == m) & (g > -jnp.inf)
        idx = jnp.min(jnp.where(cand, blk, nblk), axis=0, keepdims=True)
        pick = blk == idx
        sel_bias = jnp.where(pick, 0.0, sel_bias)
        g = jnp.where(pick, -jnp.inf, g)

    for j in reversed(range(nblk)):
        s_all = _dot_nt(kbf[j], q2_s[j * W:, :])
        probs, scales = [], []
        for i in range(j, nblk):
            here = slice((i - j) * W, (i - j + 1) * W)
            lanes = slice(i * W, (i + 1) * W)
            if i == j:
                s = jnp.where(causal, s_all[:, here] + base, NEG)
                m_new = jnp.max(s, axis=0, keepdims=True)
                pj = jnp.exp(s - m_new)
                l_s[:, lanes] = jnp.sum(pj, axis=0, keepdims=True)
                scales.append(None)
            else:
                s = s_all[:, here] + base + (sel_bias[j:j + 1, lanes] - slope * float(B * (i - j)))
                m_old = m_s[:, lanes]
                m_new = jnp.maximum(m_old, jnp.max(s, axis=0, keepdims=True))
                a = jnp.exp(m_old - m_new)
                pj = jnp.exp(s - m_new)
                l_s[:, lanes] = a * l_s[:, lanes] + jnp.sum(pj, axis=0, keepdims=True)
                scales.append(a)
            m_s[:, lanes] = m_new
            probs.append(pj.astype(BF16))
        pv = _dot(vT[j], jnp.concatenate(probs, axis=1))
        for i in range(j, nblk):
            here = slice((i - j) * W, (i - j + 1) * W)
            lanes = slice(i * W, (i + 1) * W)
            a = scales[i - j]
            acc_s[:, lanes] = pv[:, here] if a is None else a * acc_s[:, lanes] + pv[:, here]

    sub = lax.broadcasted_iota(I32, (2 * HEAD_DIM, B), 0)
    for i in range(nblk):
        o = acc_s[:, i * W:(i + 1) * W] / l_s[:, i * W:(i + 1) * W]
        o_ref[i * B:(i + 1) * B, :] = jnp.where(sub < HEAD_DIM, o[:, :B], o[:, B:]).T


def _moba_prompt(q, k, v, kmean, slopes, *, batch, seq):
    n, att_w = q.shape
    nblk = seq // MOBA_BLOCK
    B = MOBA_BLOCK
    kern = functools.partial(_moba_kernel, nblk=nblk)
    return pl.pallas_call(
        kern,
        grid_spec=pltpu.PrefetchScalarGridSpec(
            num_scalar_prefetch=1,
            grid=(batch, att_w // LANE),
            in_specs=[pl.BlockSpec((seq, LANE), lambda b, p, s: (b, p)),
                      pl.BlockSpec((seq, LANE), lambda b, p, s: (b, p)),
                      pl.BlockSpec((seq, LANE), lambda b, p, s: (b, p)),
                      pl.BlockSpec((nblk, LANE), lambda b, p, s: (b, p))],
            out_specs=pl.BlockSpec((seq, LANE), lambda b, p, s: (b, p)),
            scratch_shapes=[pltpu.VMEM((nblk, B, LANE), BF16), pltpu.VMEM((nblk, LANE, B), BF16),
                            pltpu.VMEM((nblk * 2 * B, LANE), BF16), pltpu.VMEM((1, nblk * 2 * B), F32),
                            pltpu.VMEM((1, nblk * 2 * B), F32), pltpu.VMEM((LANE, nblk * 2 * B), F32)]),
        out_shape=jax.ShapeDtypeStruct((n, att_w), F32),
        compiler_params=_params("arbitrary", "arbitrary"),
        name="moba_prompt",
    )(slopes, q, k, v, kmean)


def _memkv_kernel(x_ref, w_ref, mk_ref, mv_ref, *, mem_w):
    r = _dot(x_ref[...].astype(BF16), w_ref[...])
    mk_ref[...] = r[:, :mem_w]
    mv_ref[...] = r[:, mem_w:]


def _memkv(mem2d, w_bf, *, tile, mem_w):
    n, d = mem2d.shape
    return pl.pallas_call(
        functools.partial(_memkv_kernel, mem_w=mem_w),
        grid=(n // tile,),
        in_specs=[pl.BlockSpec((tile, d), lambda i: (i, 0)), pl.BlockSpec(w_bf.shape, lambda i: (0, 0))],
        out_specs=[pl.BlockSpec((tile, mem_w), lambda i: (i, 0))] * 2,
        out_shape=[jax.ShapeDtypeStruct((n, mem_w), F32)] * 2,
        compiler_params=_params("arbitrary"),
        name="memkv",
    )(mem2d, w_bf)


def _memattn_kernel(qm_ref, mk_ref, mv_ref, o_ref, *, tq, mem_w):
    sub = lax.broadcasted_iota(I32, (LANE, tq), 0)
    for pr in range(mem_w // LANE):
        cs = slice(LANE * pr, LANE * (pr + 1))
        qp = qm_ref[:, cs]
        mkp = mk_ref[:, cs].astype(BF16)
        mvT = mv_ref[:, cs].T.astype(BF16)
        outs = []
        for hh in range(2):
            qh = jnp.where(_head_mask(qp.shape, hh), qp, jnp.zeros_like(qp))
            s = _dot_nt(mkp, qh)
            m = jnp.max(s, axis=0, keepdims=True)
            e = jnp.exp(s - m)
            l = jnp.sum(e, axis=0, keepdims=True)
            outs.append(_dot(mvT, e.astype(BF16)) / l)
        o_ref[:, cs] = jnp.where(sub < HEAD_DIM, outs[0], outs[1]).T


def _memattn_prompt(qm, mk, mv, *, batch, seq, n_mem, tq):
    n, mem_w = qm.shape
    nq = seq // tq
    return pl.pallas_call(
        functools.partial(_memattn_kernel, tq=tq, mem_w=mem_w),
        grid=(batch, nq),
        in_specs=[pl.BlockSpec((tq, mem_w), lambda b, i: (b * nq + i, 0)),
                  pl.BlockSpec((n_mem, mem_w), lambda b, i: (b, 0)),
                  pl.BlockSpec((n_mem, mem_w), lambda b, i: (b, 0))],
        out_specs=pl.BlockSpec((tq, mem_w), lambda b, i: (b * nq + i, 0)),
        out_shape=jax.ShapeDtypeStruct((n, mem_w), F32),
        compiler_params=_params("arbitrary", "arbitrary"),
        name="memattn_prompt",
    )(qm, mk, mv)


SORT_TILE = 256


def _sort_rows(ts, n_exp):
    return TOP_K * ts + SUBLANE * n_exp


def _finish1_kernel(x_ref, oa_ref, oc_ref, om_ref, gmix_ref, wout_ref, g1_ref, b1_ref, wrT_ref, br_ref,
                    x1_ref, xs_ref, route_ref, gw_ref, cpad_ref, loff_ref, cbase_ref, tot_ref, carry,
                    *, tile, ts, n_exp, alpha):
    i = pl.program_id(0)
    lt = _sort_rows(ts, n_exp)

    @pl.when(i == 0)
    def _():
        carry[...] = jnp.zeros_like(carry)

    mix = jnp.concatenate([_rms(oa_ref[...]), _rms(oc_ref[...]), _rms(om_ref[...])], axis=-1) * gmix_ref[...]
    if wout_ref.dtype == BF16:
        z = alpha * x_ref[...] + _dot(mix.astype(BF16), wout_ref[...])
    else:
        z = alpha * x_ref[...] + _dot(mix, wout_ref[...], precision=HI)
    x1 = _layernorm(z, g1_ref[...], b1_ref[...])
    x1_ref[...] = x1
    x1b = x1.astype(BF16)

    g = _dot_nt(wrT_ref[...], x1, precision=HI) + br_ref[...]
    eidx = lax.broadcasted_iota(I32, (n_exp, tile), 0)
    picks, vals = [], []
    for k in range(TOP_K):
        m = jnp.max(g, axis=0, keepdims=True)
        idx = jnp.min(jnp.where(g == m, eidx, n_exp), axis=0, keepdims=True)
        pick = eidx == idx
        route_ref[k:k + 1, :] = idx
        picks.append(pick)
        vals.append(m)
        g = jnp.where(pick, -jnp.inf, g)
    ex = [jnp.exp(v - vals[0]) for v in vals]
    denom = ex[0] + ex[1] + ex[2] + ex[3]
    for k in range(TOP_K):
        gw_ref[k:k + 1, :] = ex[k] / denom
    gw_ref[TOP_K:, :] = jnp.zeros((gw_ref.shape[0] - TOP_K, tile), F32)

    t_src = lax.broadcasted_iota(I32, (ts, ts), 0)
    t_dst = lax.broadcasted_iota(I32, (ts, ts), 1)
    before = jnp.where(t_src < t_dst, 1.0, 0.0).astype(BF16)
    e_src = lax.broadcasted_iota(I32, (n_exp, n_exp), 1)
    e_dst = lax.broadcasted_iota(I32, (n_exp, n_exp), 0)
    lower = jnp.where(e_src < e_dst, 1.0, 0.0).astype(BF16)
    slot = lax.broadcasted_iota(I32, (lt, ts), 0)
    for sub in range(tile // ts):
        cs = slice(sub * ts, (sub + 1) * ts)
        pk = [p[:, cs] for p in picks]
        onehot = jnp.zeros((n_exp, ts), F32)
        for p in pk:
            onehot = onehot + jnp.where(p, 1.0, 0.0)
        cnt = jnp.sum(onehot, axis=1, keepdims=True)
        cpad = jnp.floor((cnt + (SUBLANE - 1)) * (1.0 / SUBLANE)) * SUBLANE
        cpad_l = jnp.broadcast_to(cpad, (n_exp, LANE))
        loff_l = _dot(lower, cpad_l.astype(BF16))
        pos = _dot(onehot.astype(BF16), before) + loff_l[:, 0:1]
        perm = jnp.zeros((lt, ts), F32)
        for k in range(TOP_K):
            lpos = jnp.sum(jnp.where(pk[k], pos, 0.0), axis=0, keepdims=True).astype(I32)
            route_ref[TOP_K + k:TOP_K + k + 1, cs] = lpos
            perm = perm + jnp.where(slot == lpos, 1.0, 0.0)
        perm = perm.astype(BF16)
        xs_ref[sub * lt:(sub + 1) * lt, :] = _dot(perm, x1b[cs, :])
        cpad_ref[sub] = cpad_l
        loff_ref[sub] = loff_l
        cbase_ref[sub] = jnp.broadcast_to(carry[...], (n_exp, LANE))
        carry[...] = carry[...] + cpad
    tot_ref[...] = jnp.broadcast_to(carry[...], tot_ref.shape)


def _finish1(x2d, oa, oc, om, g_mix, wout_bf, ln_g, ln_b, w_rT, b_r, *, tile, ts, alpha):
    n, d = x2d.shape
    n_exp = w_rT.shape[0]
    lt = _sort_rows(ts, n_exp)
    spt = tile // ts
    kern = functools.partial(_finish1_kernel, tile=tile, ts=ts, n_exp=n_exp, alpha=alpha)
    row = lambda w: pl.BlockSpec((tile, w), lambda i: (i, 0))
    full = lambda a: pl.BlockSpec(a.shape, lambda i: (0,) * a.ndim)
    meta = pl.BlockSpec((spt, n_exp, LANE), lambda i: (i, 0, 0))
    meta_shape = jax.ShapeDtypeStruct((n // ts, n_exp, LANE), F32)
    return pl.pallas_call(
        kern,
        grid=(n // tile,),
        in_specs=[row(d), row(oa.shape[1]), row(oc.shape[1]), row(om.shape[1]), full(g_mix), full(wout_bf),
                  full(ln_g), full(ln_b), full(w_rT), full(b_r)],
        out_specs=[row(d), pl.BlockSpec((spt * lt, d), lambda i: (i, 0)),
                   pl.BlockSpec((2 * TOP_K, tile), lambda i: (0, i)), pl.BlockSpec((2 * TOP_K, tile), lambda i: (0, i)),
                   meta, meta, meta, pl.BlockSpec((n_exp, LANE), lambda i: (0, 0))],
        out_shape=[jax.ShapeDtypeStruct((n, d), F32), jax.ShapeDtypeStruct((n // ts * lt, d), F32),
                   jax.ShapeDtypeStruct((2 * TOP_K, n), I32), jax.ShapeDtypeStruct((2 * TOP_K, n), F32),
                   meta_shape, meta_shape, meta_shape, jax.ShapeDtypeStruct((n_exp, LANE), F32)],
        scratch_shapes=[pltpu.VMEM((n_exp, 1), F32)],
        compiler_params=_params("arbitrary"),
        name="finish1",
    )(x2d, oa, oc, om, g_mix, wout_bf, ln_g, ln_b, w_rT, b_r)


def _run_sizes(ts):
    sizes, s = [], SUBLANE
    while s <= max(ts, SUBLANE):
        sizes.append(s)
        s *= 2
    return sizes


def _for_each_piece(length, sizes, fn):
    off = 0
    for sz in sizes:
        @pl.when((length & sz) != 0)
        def _(off=off, sz=sz):
            fn(off, sz)
        off = off + (length & sz)


def _dispatch_kernel(cpad_ref, loff_ref, base_ref, fill_ref, xs_ref, zero_ref, buf_ref, sem, fill_sem,
                     *, lt, n_exp, sizes, tm):
    t = pl.program_id(0)

    def run(e, wait):
        n = cpad_ref[t * n_exp + e]
        src = loff_ref[t * n_exp + e]
        dst = base_ref[t * n_exp + e]

        def piece(off, sz):
            cp = pltpu.make_async_copy(xs_ref.at[pl.ds(pl.multiple_of(src + off, SUBLANE), sz), :],
                                       buf_ref.at[pl.ds(pl.multiple_of(dst + off, SUBLANE), sz), :], sem)
            cp.wait() if wait else cp.start()

        _for_each_piece(n, sizes, piece)

    def issue(e, c):
        run(e, False)
        return c

    def drain(e, c):
        run(e, True)
        return c

    lax.fori_loop(0, n_exp, issue, 0)

    @pl.when(t == pl.num_programs(0) - 1)
    def _():
        start = fill_ref[0]
        rest = buf_ref.shape[0] - start
        n_full = rest // tm

        def chunk(c, wait):
            cp = pltpu.make_async_copy(zero_ref, buf_ref.at[pl.ds(pl.multiple_of(start + c * tm, SUBLANE), tm), :],
                                       fill_sem)
            cp.wait() if wait else cp.start()

        def piece(wait):
            def fn(off, sz):
                cp = pltpu.make_async_copy(
                    zero_ref.at[pl.ds(0, sz), :],
                    buf_ref.at[pl.ds(pl.multiple_of(start + n_full * tm + off, SUBLANE), sz), :], fill_sem)
                cp.wait() if wait else cp.start()
            return fn

        tail_sizes = [s for s in _run_sizes(tm) if s < tm]
        for wait in (False, True):
            lax.fori_loop(0, n_full, lambda c, carry, wait=wait: (chunk(c, wait), carry)[1], 0)
            _for_each_piece(rest - n_full * tm, tail_sizes, piece(wait))

    lax.fori_loop(0, n_exp, drain, 0)


def _dispatch(cpad, loff, base, fill, xs, *, ts, n_exp, m_pad, tm):
    w = xs.shape[1]
    lt = _sort_rows(ts, n_exp)
    n_sub = xs.shape[0] // lt
    zero = jnp.zeros((tm, w), F32)
    return pl.pallas_call(
        functools.partial(_dispatch_kernel, lt=lt, n_exp=n_exp, sizes=_run_sizes(ts), tm=tm),
        grid_spec=pltpu.PrefetchScalarGridSpec(
            num_scalar_prefetch=4,
            grid=(n_sub,),
            in_specs=[pl.BlockSpec((lt, w), lambda t, *_: (t, 0)), pl.BlockSpec((tm, w), lambda t, *_: (0, 0))],
            out_specs=pl.BlockSpec(memory_space=pl.ANY),
            scratch_shapes=[pltpu.SemaphoreType.DMA(()), pltpu.SemaphoreType.DMA(())]),
        out_shape=jax.ShapeDtypeStruct((m_pad, w), F32),
        compiler_params=_params("arbitrary"),
        name="dispatch",
    )(cpad, loff, base, fill, xs, zero)


def _ffn_kernel(otile_ref, tile_ref, exp_ref, lo_ref, hi_ref, lhs_ref, wgu_ref, bgu_ref, wd_ref, bd_ref, out_ref,
                *, tm, d_ff):
    g = pl.program_id(0)
    lo = lo_ref[g]
    hi = hi_ref[g]

    @pl.when(hi == lo)
    def _():
        out_ref[...] = jnp.zeros_like(out_ref)

    @pl.when(hi > lo)
    def _():
        gu = _dot(lhs_ref[...].astype(BF16), wgu_ref[0]) + bgu_ref[0]
        gate = jnp.minimum(gu[:, :d_ff], SWIGLU_LIMIT)
        up = jnp.clip(gu[:, d_ff:], -SWIGLU_LIMIT, SWIGLU_LIMIT)
        hid = (up + 1.0) * gate * jax.nn.sigmoid(SWIGLU_ALPHA * gate)
        o = _dot(hid.astype(BF16), wd_ref[0]) + bd_ref[0]

        @pl.when(lo == 0)
        def _():
            out_ref[...] = o

        @pl.when(lo > 0)
        def _():
            rows = lax.broadcasted_iota(I32, (tm, 1), 0)
            out_ref[...] = jnp.where((rows >= lo) & (rows < hi), o, out_ref[...])


def _ffn(sched, buf, wgu_bf, b_gu, wd_bf, b_d, *, tm):
    m, w = buf.shape
    n_exp, d, d_ff2 = wgu_bf.shape
    out_tiles, tiles, experts, los, his = sched
    return pl.pallas_call(
        functools.partial(_ffn_kernel, tm=tm, d_ff=d_ff2 // 2),
        grid_spec=pltpu.PrefetchScalarGridSpec(
            num_scalar_prefetch=5,
            grid=(tiles.shape[0],),
            in_specs=[pl.BlockSpec((tm, w), lambda g, ot, t, e, lo, hi: (t[g], 0)),
                      pl.BlockSpec((1, d, d_ff2), lambda g, ot, t, e, lo, hi: (e[g], 0, 0)),
                      pl.BlockSpec((1, 1, d_ff2), lambda g, ot, t, e, lo, hi: (e[g], 0, 0)),
                      pl.BlockSpec((1, d_ff2 // 2, d), lambda g, ot, t, e, lo, hi: (e[g], 0, 0)),
                      pl.BlockSpec((1, 1, d), lambda g, ot, t, e, lo, hi: (e[g], 0, 0))],
            out_specs=pl.BlockSpec((tm, d), lambda g, ot, t, e, lo, hi: (ot[g], 0))),
        out_shape=jax.ShapeDtypeStruct((m, d), F32),
        compiler_params=_params("arbitrary"),
        name="expert_ffn",
    )(out_tiles, tiles, experts, los, his, buf, wgu_bf, b_gu, wd_bf, b_d)


def _moe_schedule(counts, n_rows, tm):
    n_exp = counts.shape[0]
    n_tiles = n_rows // tm
    n_items = n_tiles + n_exp - 1
    ends = jnp.cumsum(counts)
    starts = ends - counts
    first_tile = starts // tm
    n_it = jnp.where(counts > 0, (ends - 1) // tm - first_tile + 1, 0)
    it_end = jnp.cumsum(n_it)
    it_start = it_end - n_it
    item = jnp.arange(n_items, dtype=I32)
    n_real = it_end[-1]
    g = jnp.minimum(item, n_real - 1)
    e = jnp.minimum(jnp.sum((it_end[None, :] <= g[:, None]).astype(I32), axis=1), n_exp - 1)
    of_e = e[:, None] == jnp.arange(n_exp, dtype=I32)[None, :]
    at_e = lambda table: jnp.sum(jnp.where(of_e, table[None, :], 0), axis=1)
    tile = (at_e(first_tile) + g - at_e(it_start)).astype(I32)
    valid = item < n_real
    lo = jnp.where(valid, jnp.clip(at_e(starts) - tile * tm, 0, tm), 0).astype(I32)
    hi = jnp.where(valid, jnp.clip(at_e(ends) - tile * tm, 0, tm), 0).astype(I32)
    used = (ends[-1] + tm - 1) // tm
    out_tile = jnp.where(valid, tile, jnp.minimum(used + item - n_real, n_tiles - 1)).astype(I32)
    return (out_tile, tile, e, lo, hi), starts.astype(I32)


def _combine_kernel(cpad_ref, loff_ref, base_ref, route_ref, gw_ref, x1_ref, g2_ref, b2_ref, eo_ref, y_ref,
                    gbuf, sem, *, ts, n_exp, sizes, alpha):
    t = pl.program_id(0)
    nt = pl.num_programs(0)
    lt = gbuf.shape[1]
    d = gbuf.shape[2]

    def tile_copies(tt, slot, wait):
        def run(e, c):
            n = cpad_ref[tt * n_exp + e]
            src = base_ref[tt * n_exp + e]
            dst = loff_ref[tt * n_exp + e]

            def piece(off, sz):
                cp = pltpu.make_async_copy(eo_ref.at[pl.ds(pl.multiple_of(src + off, SUBLANE), sz), :],
                                           gbuf.at[slot, pl.ds(pl.multiple_of(dst + off, SUBLANE), sz), :],
                                           sem.at[slot])
                cp.wait() if wait else cp.start()

            _for_each_piece(n, sizes, piece)
            return c

        lax.fori_loop(0, n_exp, run, 0, unroll=4)

    def fetch(tt, slot):
        gbuf[slot, TOP_K * ts:, :] = jnp.zeros((lt - TOP_K * ts, d), F32)
        tile_copies(tt, slot, False)

    slot = t % 2

    @pl.when(t == 0)
    def _():
        fetch(t, slot)

    @pl.when(t + 1 < nt)
    def _():
        fetch(t + 1, 1 - slot)

    tile_copies(t, slot, True)

    r = lax.broadcasted_iota(I32, (ts, ts), 0)
    c = lax.broadcasted_iota(I32, (ts, ts), 1)
    eye = jnp.where(r == c, 1.0, 0.0)
    wcol = _dot_nt(eye, gw_ref[...], precision=HI)
    pcol = _dot_nt(eye, route_ref[...].astype(F32), precision=HI)
    rows = gbuf[slot].astype(BF16)
    slot_i = lax.broadcasted_iota(I32, (ts, lt), 1)
    unsort = jnp.zeros((ts, lt), F32)
    for k in range(TOP_K):
        lpos = pcol[:, TOP_K + k:TOP_K + k + 1].astype(I32)
        unsort = unsort + jnp.where(slot_i == lpos, wcol[:, k:k + 1], 0.0)
    hi = unsort.astype(BF16)
    lo = (unsort - hi.astype(F32)).astype(BF16)
    moe = _dot(hi, rows) + _dot(lo, rows)
    y_ref[...] = _layernorm(alpha * x1_ref[...] + moe, g2_ref[...], b2_ref[...])


def _combine(cpad, loff, base, route, gw, x1, ln_g, ln_b, eo, *, ts, n_exp, alpha):
    n, d = x1.shape
    lt = _sort_rows(ts, n_exp)
    full = lambda a: pl.BlockSpec(a.shape, lambda i, *_: (0,) * a.ndim)
    return pl.pallas_call(
        functools.partial(_combine_kernel, ts=ts, n_exp=n_exp, sizes=_run_sizes(ts), alpha=alpha),
        grid_spec=pltpu.PrefetchScalarGridSpec(
            num_scalar_prefetch=3,
            grid=(n // ts,),
            in_specs=[pl.BlockSpec((2 * TOP_K, ts), lambda i, *_: (0, i)),
                      pl.BlockSpec((2 * TOP_K, ts), lambda i, *_: (0, i)),
                      pl.BlockSpec((ts, d), lambda i, *_: (i, 0)), full(ln_g), full(ln_b),
                      pl.BlockSpec(memory_space=pl.ANY)],
            out_specs=pl.BlockSpec((ts, d), lambda i, *_: (i, 0)),
            scratch_shapes=[pltpu.VMEM((2, lt, d), F32), pltpu.SemaphoreType.DMA((2,))]),
        out_shape=jax.ShapeDtypeStruct((n, d), F32),
        compiler_params=_params("arbitrary"),
        name="combine",
    )(cpad, loff, base, route, gw, x1, ln_g, ln_b, eo)


def _finish(x2d, oa, oc, om, wts, w_out, *, tile, tm, alpha):
    n = x2d.shape[0]
    n_exp = wts["w_rT"].shape[0]
    ts = _pick_tile(n, SORT_TILE)
    n_sub = n // ts
    x1, xs, route, gw, cpad, loff, cbase, tot = _finish1(
        x2d, oa, oc, om, wts["g_mix"], w_out, wts["ln1_g"], wts["ln1_b"], wts["w_rT"], wts["b_r"],
        tile=_pick_tile(n, tile), ts=ts, alpha=alpha)
    m_pad = n_sub * _sort_rows(ts, n_exp) + tm
    counts = tot[:, 0].astype(I32)
    sched, offs = _moe_schedule(counts, m_pad, tm)
    flat = lambda a: a[:, :, 0].astype(I32).reshape(-1)
    cpad, loff = flat(cpad), flat(loff)
    base = (cbase[:, :, 0].astype(I32) + offs[None, :]).reshape(-1)
    fill = jnp.sum(counts, keepdims=True)
    buf = _dispatch(cpad, loff, base, fill, xs, ts=ts, n_exp=n_exp, m_pad=m_pad, tm=tm)
    eo = _ffn(sched, buf, wts["w_gu"], wts["b_gu"], wts["w_d"], wts["b_d"], tm=tm)
    return _combine(cpad, loff, base, route, gw, x1, wts["ln2_g"], wts["ln2_b"], eo, ts=ts, n_exp=n_exp, alpha=alpha)


def _sample_inproj_kernel(x_ref, w_ref, wc_ref, p0_ref, p1_ref, q_ref, k_ref, v_ref, qm_ref, oc_ref, u_ref,
                          *, att_w, mem_w, conv_w):
    x = x_ref[...]
    c1, c2, c3 = att_w, 2 * att_w, 3 * att_w
    c4 = c3 + mem_w
    c5, c6, c7 = c4 + conv_w, c4 + 2 * conv_w, c4 + 3 * conv_w

    def proj(lo, hi):
        return _dot(x, w_ref[:, lo:hi], precision=HI)

    q_ref[...] = proj(0, c1) * Q_SCALE
    k_ref[...] = proj(c1, c2)
    v_ref[...] = proj(c2, c3)
    qm_ref[...] = proj(c3, c4) * Q_SCALE
    u = proj(c5, c6) * proj(c6, c7)
    cz = wc_ref[0:1, :] * p0_ref[...] + wc_ref[1:2, :] * p1_ref[...] + wc_ref[2:3, :] * u
    oc_ref[...] = proj(c4, c5) * cz
    u_ref[...] = u


def _sample_inproj(x2d, w_bf, w_conv, prev0, prev1, *, att_w, mem_w, conv_w):
    n = x2d.shape[0]
    args = (x2d, w_bf, w_conv, prev0, prev1)
    full = lambda a: pl.BlockSpec(a.shape, lambda i: (0,) * a.ndim)
    widths = (att_w, att_w, att_w, mem_w, conv_w, conv_w)
    return pl.pallas_call(
        functools.partial(_sample_inproj_kernel, att_w=att_w, mem_w=mem_w, conv_w=conv_w),
        grid=(1,),
        in_specs=[full(a) for a in args],
        out_specs=[pl.BlockSpec((n, w), lambda i: (0, 0)) for w in widths],
        out_shape=[jax.ShapeDtypeStruct((n, w), F32) for w in widths],
        compiler_params=_params("arbitrary"),
        name="sample_inproj",
    )(*args)


def _kscan_kernel(pt_ref, q_ref, ck_ref, sel_ref, kbuf, ksum, sem, *, n_pages, chunk, n_heads):
    b = pl.program_id(0)
    n_chunks = n_pages // chunk
    pages_per_blk = MOBA_BLOCK // PAGE_SIZE
    nblk = ksum.shape[0]
    q_col = q_ref[0]

    def copies(bb, c, slot):
        return [pltpu.make_async_copy(ck_ref.at[0, pt_ref[bb * n_pages + c * chunk + j]], kbuf.at[slot, j],
                                      sem.at[slot]) for j in range(chunk)]

    @pl.when(b == 0)
    def _():
        for cp in copies(b, 0, 0):
            cp.start()

    def body(c, carry):
        slot = c % 2
        last = c + 1 == n_chunks

        @pl.when(jnp.logical_or(jnp.logical_not(last), b + 1 < pl.num_programs(0)))
        def _():
            for cp in copies(jnp.where(last, b + 1, b), jnp.where(last, 0, c + 1), 1 - slot):
                cp.start()

        for cp in copies(b, c, slot):
            cp.wait()
        for jb in range(chunk // pages_per_blk):
            s = kbuf[slot, pages_per_blk * jb]
            for pg in range(1, pages_per_blk):
                s = s + kbuf[slot, pages_per_blk * jb + pg]
            ksum[c * (chunk // pages_per_blk) + jb] = jnp.sum(s * q_col, axis=1)
        return carry

    lax.fori_loop(0, n_chunks, body, 0)

    g = jnp.sum(ksum[...], axis=2, keepdims=True) * (1.0 / MOBA_BLOCK)
    blk = lax.broadcasted_iota(I32, g.shape, 0)
    sel_ref[...] = jnp.zeros(sel_ref.shape, I32)
    for r in range(MOBA_TOPK):
        m = jnp.max(g, axis=0, keepdims=True)
        idx = jnp.min(jnp.where((g == m) & (g > -jnp.inf), blk, nblk), axis=0, keepdims=True)
        sel_ref[0, :, r:r + 1] = idx[0]
        g = jnp.where(blk == idx, -jnp.inf, g)


def _kscan(pt_flat, q_col, ckT, *, n_pages, chunk):
    bs, n_heads, hd, _ = q_col.shape
    nblk = n_pages * PAGE_SIZE // MOBA_BLOCK
    return pl.pallas_call(
        functools.partial(_kscan_kernel, n_pages=n_pages, chunk=chunk, n_heads=n_heads),
        grid_spec=pltpu.PrefetchScalarGridSpec(
            num_scalar_prefetch=1,
            grid=(bs,),
            in_specs=[pl.BlockSpec((1, n_heads, hd, 1), lambda b, pt: (b, 0, 0, 0)), pl.BlockSpec(memory_space=pl.ANY)],
            out_specs=pl.BlockSpec((1, n_heads, LANE), lambda b, pt: (b, 0, 0)),
            scratch_shapes=[pltpu.VMEM((2, chunk, n_heads, hd, PAGE_SIZE), F32),
                            pltpu.VMEM((nblk, n_heads, PAGE_SIZE), F32), pltpu.SemaphoreType.DMA((2,))]),
        out_shape=jax.ShapeDtypeStruct((bs, n_heads, LANE), I32),
        compiler_params=_params("arbitrary"),
        name="kscan",
    )(pt_flat, q_col, ckT)


def _sample_attn_kernel(pt_ref, sel_ref, slopes_ref, q_ref, kn_ref, vn_ref, qm_ref, mk_ref, mv_ref, ck_ref, cv_ref,
                        oa_ref, om_ref, kbuf, vbuf, sem, *, n_pages, n_heads, past_len, mem_w):
    b = pl.program_id(0)
    nb = pl.num_programs(0)
    B = MOBA_BLOCK
    pages_per_blk = B // PAGE_SIZE
    n_keys = MOBA_TOPK * B

    def blocks(bb, h):
        return [sel_ref[(bb * n_heads + h) * MOBA_TOPK + s] for s in range(MOBA_TOPK)]

    n_sel_pages = MOBA_TOPK * pages_per_blk

    def copies(bb, slot):
        cps = []
        for h in range(n_heads):
            blks = blocks(bb, h)
            for s in range(MOBA_TOPK):
                for half in range(pages_per_blk):
                    pg = pt_ref[bb * n_pages + pages_per_blk * blks[s] + half]
                    pp = s * pages_per_blk + half
                    cps.append(pltpu.make_async_copy(ck_ref.at[0, pg, h], kbuf.at[slot, h, pp], sem.at[0, slot]))
                    cps.append(pltpu.make_async_copy(cv_ref.at[0, pg, h], vbuf.at[slot, h, pp], sem.at[1, slot]))
        return cps

    slot = b % 2

    @pl.when(b == 0)
    def _():
        for cp in copies(b, slot):
            cp.start()

    @pl.when(b + 1 < nb)
    def _():
        for cp in copies(b + 1, 1 - slot):
            cp.start()

    for cp in copies(b, slot):
        cp.wait()

    key_lane = lax.broadcasted_iota(I32, (1, n_keys), 1)
    lane = lax.broadcasted_iota(I32, (1, LANE), 1)
    for h in range(n_heads):
        blks = blocks(b, h)
        qh = q_ref[0, h:h + 1, :]
        q8 = jnp.broadcast_to(qh, (SUBLANE, HEAD_DIM))
        blk_of_key = jnp.where(key_lane < B, blks[0], jnp.where(key_lane < 2 * B, blks[1], blks[2]))
        dist = (past_len - blk_of_key * B - (key_lane & (B - 1))).astype(F32)
        s = jnp.concatenate([_dot(q8.astype(BF16), kbuf[slot, h, pp].astype(BF16)) for pp in range(n_sel_pages)],
                            axis=1)
        s = s - slopes_ref[h] * dist
        s_self = jnp.sum(qh * kn_ref[0, h:h + 1, :], axis=1, keepdims=True)
        m = jnp.maximum(jnp.max(s, axis=1, keepdims=True), s_self)
        e = jnp.exp(s - m)
        e_self = jnp.exp(s_self - m)
        l = jnp.sum(e, axis=1, keepdims=True) + e_self
        o = e_self * vn_ref[0, h:h + 1, :]
        for pp in range(n_sel_pages):
            o = o + _dot_nt(e[:, pp * PAGE_SIZE:(pp + 1) * PAGE_SIZE].astype(BF16), vbuf[slot, h, pp].astype(BF16))
        oa_ref[0, h:h + 1, :] = (o / l)[0:1, :]

    qm_row = qm_ref[pl.ds(b, 1), :]
    outs = []
    for pr in range(mem_w // LANE):
        cs = slice(pr * LANE, (pr + 1) * LANE)
        mkp = mk_ref[0, :, cs]
        mvp = mv_ref[0, :, cs]
        pair = []
        for hh in range(2):
            qh = jnp.where(_head_mask((1, LANE), hh), qm_row[:, cs], 0.0)
            s = _dot_nt(jnp.broadcast_to(qh, (SUBLANE, LANE)), mkp, precision=HI)
            e = jnp.exp(s - jnp.max(s, axis=1, keepdims=True))
            o = _dot(e, mvp, precision=HI) / jnp.sum(e, axis=1, keepdims=True)
            pair.append(o[0:1, :])
        outs.append(jnp.where(lane < HEAD_DIM, pair[0], pair[1]))
    om_ref[0] = jnp.concatenate(outs, axis=1)


def _sample_attn(pt_flat, sel_flat, slopes, q3, kn3, vn3, qm, mk, mv, ck, cv, *, n_pages):
    bs, n_heads, hd = q3.shape
    _, n_mem, mem_w = mk.shape
    n_keys = MOBA_TOPK * MOBA_BLOCK
    per_b = lambda a: pl.BlockSpec((1,) + a.shape[1:], lambda b, pt, sel: (b,) + (0,) * (a.ndim - 1))
    return pl.pallas_call(
        functools.partial(_sample_attn_kernel, n_pages=n_pages, n_heads=n_heads, past_len=n_pages * PAGE_SIZE,
                          mem_w=mem_w),
        grid_spec=pltpu.PrefetchScalarGridSpec(
            num_scalar_prefetch=2,
            grid=(bs,),
            in_specs=[pl.BlockSpec(memory_space=pltpu.SMEM), per_b(q3), per_b(kn3), per_b(vn3),
                      pl.BlockSpec(qm.shape, lambda b, pt, sel: (0, 0)), per_b(mk), per_b(mv),
                      pl.BlockSpec(memory_space=pl.ANY), pl.BlockSpec(memory_space=pl.ANY)],
            out_specs=[pl.BlockSpec((1, n_heads, hd), lambda b, pt, sel: (b, 0, 0)),
                       pl.BlockSpec((1, 1, mem_w), lambda b, pt, sel: (b, 0, 0))],
            scratch_shapes=[pltpu.VMEM((2, n_heads, n_keys // PAGE_SIZE, hd, PAGE_SIZE), F32),
                            pltpu.VMEM((2, n_heads, n_keys // PAGE_SIZE, hd, PAGE_SIZE), F32),
                            pltpu.SemaphoreType.DMA((2, 2))]),
        out_shape=[jax.ShapeDtypeStruct((bs, n_heads, hd), F32), jax.ShapeDtypeStruct((bs, 1, mem_w), F32)],
        compiler_params=_params("arbitrary"),
        name="sample_attn",
    )(pt_flat, sel_flat, slopes, q3, kn3, vn3, qm, mk, mv, ck, cv)


def _sample_layer(x, cache_k, cache_v, mem_k, mem_v, state, page_table, wts, dims, alpha):
    bs, dec_seq, d = x.shape
    assert dec_seq == 1
    att_w, mem_w, conv_w = dims
    n_heads = att_w // HEAD_DIM
    n_pages = page_table.shape[1]
    assert (n_pages * PAGE_SIZE) % MOBA_BLOCK == 0 and n_pages * PAGE_SIZE // MOBA_BLOCK >= MOBA_TOPK
    x2d = x.reshape(bs, d)
    q, k, v, qm, oc, u = _sample_inproj(x2d, wts["w_in_f32"], wts["w_conv"], state[:, 0, :], state[:, 1, :],
                                        att_w=att_w, mem_w=mem_w, conv_w=conv_w)
    pt_flat = page_table.reshape(-1)
    heads = lambda a: a.reshape(bs, n_heads, HEAD_DIM)
    pagesT = lambda c: jnp.transpose(c, (0, 1, 3, 4, 2))
    chunk = 16 if n_pages % 32 == 0 else 8
    assert n_pages % (2 * chunk) == 0
    sel = _kscan(pt_flat, q.reshape(bs, n_heads, HEAD_DIM, 1), pagesT(cache_k), n_pages=n_pages, chunk=chunk)
    sel_flat = sel[:, :, :MOBA_TOPK].reshape(-1)
    oa, om = _sample_attn(pt_flat, sel_flat, wts["slopes"], heads(q), heads(k), heads(v), qm,
                          mem_k.reshape(bs, -1, mem_w), mem_v.reshape(bs, -1, mem_w), pagesT(cache_k),
                          pagesT(cache_v), n_pages=n_pages)
    y = _finish(x2d, oa.reshape(bs, att_w), oc, om.reshape(bs, mem_w), wts, wts["w_out_f32"], tile=bs,
                tm=bs * TOP_K, alpha=alpha)
    conv_state = jnp.stack([state[:, 1, :], u], axis=1)
    return y.reshape(bs, 1, d), k, v, conv_state


def _pick_tile(n, pref):
    return pref if n % pref == 0 else n


def _prompt_layer(x, mem, wts, dims, alpha):
    batch, seq, d = x.shape
    att_w, mem_w, conv_w = dims
    n = batch * seq
    n_mem = mem.shape[1]
    x2d = x.reshape(n, d)
    tile = 512
    q, k, v, kT, vT, kmean, qm, oc, tail = _inproj(x2d, wts["w_in"], wts["w_conv"], seq=seq, tile=tile, att_w=att_w,
                                                   mem_w=mem_w, conv_w=conv_w)
    oa = _moba_prompt(q, k, v, kmean.reshape(n // MOBA_BLOCK, att_w), wts["slopes"], batch=batch, seq=seq)
    mk, mv = _memkv(mem.reshape(batch * n_mem, d), wts["w_mem_kv"], tile=_pick_tile(batch * n_mem, 512), mem_w=mem_w)
    om = _memattn_prompt(qm, mk, mv, batch=batch, seq=seq, n_mem=n_mem, tq=512)
    y = _finish(x2d, oa, oc, om, wts, wts["w_out"], tile=512, tm=512, alpha=alpha)
    conv_state = tail.reshape(batch, seq // tile, SUBLANE, conv_w)[:, -1, SUBLANE - (CONV_K - 1):, :]
    rows = lambda t: jnp.transpose(t, (0, 3, 1, 2))
    return y.reshape(batch, seq, d), rows(kT), rows(vT), conv_state, mk, mv


def _prep_weights(l, w_in, w_mem_kv, w_conv, g_mix, w_out, ln1_g, ln1_b, w_router, b_router, w_gate_up, b_gate_up,
                  w_down, b_down, ln2_g, ln2_b):
    n_heads = 8
    row = lambda a: a[l][None, :]
    return {
        "w_in": w_in[l].astype(BF16), "w_in_f32": w_in[l], "w_mem_kv": w_mem_kv[l].astype(BF16), "w_conv": w_conv[l],
        "g_mix": row(g_mix), "w_out": w_out[l].astype(BF16), "w_out_f32": w_out[l], "ln1_g": row(ln1_g),
        "ln1_b": row(ln1_b),
        "w_rT": w_router[l].T, "b_r": b_router[l][:, None],
        "w_gu": w_gate_up[l].astype(BF16), "b_gu": b_gate_up[l][:, None, :],
        "w_d": w_down[l].astype(BF16), "b_d": b_down[l][:, None, :],
        "ln2_g": row(ln2_g), "ln2_b": row(ln2_b),
        "slopes": 2.0 ** (-8.0 * jnp.arange(1, n_heads + 1, dtype=F32) / n_heads),
    }


def kernel(x_prompt, x_sample, cache_k, cache_v, cache_mem_k, cache_mem_v, state_conv, page_table, mem_prompt, w_in, w_mem_kv, w_conv, g_mix, w_out, ln1_g, ln1_b, w_router, b_router, w_gate_up, b_gate_up, w_down, b_down, ln2_g, ln2_b):
    depth = w_in.shape[0]
    assert depth == 1
    alpha = (2 * depth) ** 0.25
    n_heads = cache_k.shape[3]
    att_w = n_heads * HEAD_DIM
    mem_w = cache_mem_k.shape[3] * HEAD_DIM
    conv_w = state_conv.shape[3]
    dims = (att_w, mem_w, conv_w)
    wts = _prep_weights(0, w_in, w_mem_kv, w_conv, g_mix, w_out, ln1_g, ln1_b, w_router, b_router, w_gate_up,
                        b_gate_up, w_down, b_down, ln2_g, ln2_b)
    bp, seq, d = x_prompt.shape
    n_mem = mem_prompt.shape[1]
    y_p, k_p, v_p, conv_p, mk_p, mv_p = _prompt_layer(x_prompt, mem_prompt, wts, dims, alpha)
    bs = x_sample.shape[0]
    y_s, k_s, v_s, conv_s = _sample_layer(x_sample, cache_k, cache_v, cache_mem_k[0], cache_mem_v[0],
                                          state_conv[0], page_table, wts, dims, alpha)
    return (y_p, y_s,
            k_p[None], v_p[None], conv_p[None], mk_p.reshape(1, bp, n_mem, mem_w // HEAD_DIM, HEAD_DIM),
            mv_p.reshape(1, bp, n_mem, mem_w // HEAD_DIM, HEAD_DIM),
            k_s.reshape(1, bs, 1, n_heads, HEAD_DIM), v_s.reshape(1, bs, 1, n_heads, HEAD_DIM), conv_s[None])
```

```python
import functools

import jax
import jax.numpy as jnp
from jax import lax
from jax.experimental import pallas as pl
from jax.experimental.pallas import tpu as pltpu

F32 = jnp.float32
BF16 = jnp.bfloat16
I32 = jnp.int32

HEAD_DIM = 64
MOBA_BLOCK = 256
MOBA_TOPK = 3
PAGE_SIZE = 128
TOP_K = 4
CONV_K = 3
SWIGLU_LIMIT = 7.0
SWIGLU_ALPHA = 1.702
LN_EPS = 1e-5
Q_SCALE = HEAD_DIM ** -0.5
NEG = -1e30
LANE = 128
SUBLANE = 8
VMEM_LIMIT = 56 * 1024 * 1024
HI = lax.Precision.HIGHEST


def _params(*sem):
    return pltpu.CompilerParams(dimension_semantics=sem, vmem_limit_bytes=VMEM_LIMIT)


def _dot_nt(a, b, precision=None):
    return lax.dot_general(a, b, (((1,), (1,)), ((), ())), precision=precision, preferred_element_type=F32)


def _dot(a, b, precision=None):
    return jnp.dot(a, b, precision=precision, preferred_element_type=F32)


def _layernorm(z, g, b):
    zc = z - jnp.mean(z, axis=-1, keepdims=True)
    var = jnp.mean(zc * zc, axis=-1, keepdims=True)
    return zc * lax.rsqrt(var + LN_EPS) * g + b


def _rms(a):
    return a * lax.rsqrt(jnp.mean(a * a, axis=-1, keepdims=True) + LN_EPS)


def _head_mask(shape, hh):
    lane = lax.broadcasted_iota(I32, shape, len(shape) - 1)
    return (lane >= HEAD_DIM * hh) & (lane < HEAD_DIM * (hh + 1))


def _inproj_kernel(x_ref, w_ref, wc_ref, q_ref, k_ref, v_ref, kt_ref, vt_ref, km_ref, qm_ref, oc_ref, tail_ref, ubuf,
                   *, tile, tiles_per_seq, att_w, mem_w, conv_w):
    i = pl.program_id(0)
    x = x_ref[...].astype(BF16)
    c1, c2, c3 = att_w, 2 * att_w, 3 * att_w
    c4 = c3 + mem_w
    c5, c6, c7 = c4 + conv_w, c4 + 2 * conv_w, c4 + 3 * conv_w
    n_heads = att_w // HEAD_DIM

    def proj(lo, hi):
        return _dot(x, w_ref[:, lo:hi])

    q_ref[...] = (proj(0, c1) * Q_SCALE).astype(BF16)
    k = proj(c1, c2)
    k_ref[...] = k
    kt_ref[0] = k.T.reshape(n_heads, HEAD_DIM, tile)
    km_ref[0] = jnp.sum(k.reshape(tile // MOBA_BLOCK, MOBA_BLOCK, att_w), axis=1) * (1.0 / MOBA_BLOCK)
    v = proj(c2, c3)
    v_ref[...] = v
    vt_ref[0] = v.T.reshape(n_heads, HEAD_DIM, tile)
    qm_ref[...] = (proj(c3, c4) * Q_SCALE).astype(BF16)
    gb = proj(c4, c5)
    u = proj(c5, c6) * proj(c6, c7)

    @pl.when(i % tiles_per_seq == 0)
    def _():
        ubuf[0:SUBLANE, :] = jnp.zeros((SUBLANE, conv_w), F32)

    ubuf[SUBLANE:SUBLANE + tile, :] = u
    u1 = ubuf[SUBLANE - 1:SUBLANE - 1 + tile, :]
    u2 = ubuf[SUBLANE - 2:SUBLANE - 2 + tile, :]
    cz = wc_ref[0:1, :] * u2 + wc_ref[1:2, :] * u1 + wc_ref[2:3, :] * u
    oc_ref[...] = gb * cz
    tail = ubuf[tile:tile + SUBLANE, :]
    tail_ref[0] = tail
    ubuf[0:SUBLANE, :] = tail


def _inproj(x2d, w_bf, w_conv, *, seq, tile, att_w, mem_w, conv_w):
    n, d = x2d.shape
    nt = n // tile
    kern = functools.partial(_inproj_kernel, tile=tile, tiles_per_seq=seq // tile, att_w=att_w, mem_w=mem_w,
                             conv_w=conv_w)
    row = lambda w: pl.BlockSpec((tile, w), lambda i: (i, 0))
    tps = seq // tile
    n_heads = att_w // HEAD_DIM
    headsT = pl.BlockSpec((1, n_heads, HEAD_DIM, tile), lambda i: (i // tps, 0, 0, i % tps))
    headsT_shape = jax.ShapeDtypeStruct((n // seq, n_heads, HEAD_DIM, seq), F32)
    return pl.pallas_call(
        kern,
        grid=(nt,),
        in_specs=[row(d), pl.BlockSpec(w_bf.shape, lambda i: (0, 0)), pl.BlockSpec(w_conv.shape, lambda i: (0, 0))],
        out_specs=[row(att_w), row(att_w), row(att_w), headsT, headsT,
                   pl.BlockSpec((1, tile // MOBA_BLOCK, att_w), lambda i: (i, 0, 0)),
                   row(mem_w), row(conv_w), pl.BlockSpec((1, SUBLANE, conv_w), lambda i: (i, 0, 0))],
        out_shape=[jax.ShapeDtypeStruct((n, att_w), BF16), jax.ShapeDtypeStruct((n, att_w), F32),
                   jax.ShapeDtypeStruct((n, att_w), F32), headsT_shape, headsT_shape,
                   jax.ShapeDtypeStruct((nt, tile // MOBA_BLOCK, att_w), F32),
                   jax.ShapeDtypeStruct((n, mem_w), BF16), jax.ShapeDtypeStruct((n, conv_w), F32),
                   jax.ShapeDtypeStruct((nt, SUBLANE, conv_w), F32)],
        scratch_shapes=[pltpu.VMEM((tile + SUBLANE, conv_w), F32)],
        compiler_params=_params("arbitrary"),
        name="inproj",
    )(x2d, w_bf, w_conv)


def _moba_kernel(slopes_ref, q_ref, k_ref, v_ref, km_ref, o_ref, kbf, vT, q2_s, m_s, l_s, acc_s, *, nblk):
    p = pl.program_id(1)
    B = MOBA_BLOCK
    W = 2 * B

    for c in range(nblk):
        kbf[c] = k_ref[c * B:(c + 1) * B, :].astype(BF16)
        vT[c] = v_ref[c * B:(c + 1) * B, :].T.astype(BF16)
        q = q_ref[c * B:(c + 1) * B, :]
        zero = jnp.zeros_like(q)
        q2_s[c * W:c * W + B, :] = jnp.where(_head_mask(q.shape, 0), q, zero)
        q2_s[c * W + B:(c + 1) * W, :] = jnp.where(_head_mask(q.shape, 1), q, zero)

    second = lax.broadcasted_iota(I32, (1, W), 1) >= B
    slope = jnp.where(second, slopes_ref[2 * p + 1], slopes_ref[2 * p])
    key_i = lax.broadcasted_iota(I32, (B, W), 0)
    qry_i = lax.broadcasted_iota(I32, (B, W), 1) & (B - 1)
    base = slope * (key_i - qry_i).astype(F32)
    causal = key_i <= qry_i

    blk = lax.broadcasted_iota(I32, (nblk, nblk * W), 0)
    tile_of_lane = lax.broadcasted_iota(I32, (nblk, nblk * W), 1) >> (W.bit_length() - 1)
    g = _dot_nt(km_ref[...], q2_s[...].astype(F32), precision=HI)
    g = jnp.where(blk < tile_of_lane, g, -jnp.inf)
    sel_bias = jnp.full((nblk, nblk * W), NEG, F32)
    for _ in range(MOBA_TOPK):
        m = jnp.max(g, axis=0, keepdims=True)
        cand = (g == m) & (g > -jnp.inf)
        idx = jnp.min(jnp.where(cand, blk, nblk), axis=0, keepdims=True)
        pick = blk == idx
        sel_bias = jnp.where(pick, 0.0, sel_bias)
        g = jnp.where(pick, -jnp.inf, g)

    for j in reversed(range(nblk)):
        s_all = _dot_nt(kbf[j], q2_s[j * W:, :])
        probs, scales = [], []
        for i in range(j, nblk):
            here = slice((i - j) * W, (i - j + 1) * W)
            lanes = slice(i * W, (i + 1) * W)
            if i == j:
                s = jnp.where(causal, s_all[:, here] + base, NEG)
                m_new = jnp.max(s, axis=0, keepdims=True)
                pj = jnp.exp(s - m_new)
                l_s[:, lanes] = jnp.sum(pj, axis=0, keepdims=True)
                scales.append(None)
            else:
                s = s_all[:, here] + base + (sel_bias[j:j + 1, lanes] - slope * float(B * (i - j)))
                m_old = m_s[:, lanes]
                m_new = jnp.maximum(m_old, jnp.max(s, axis=0, keepdims=True))
                a = jnp.exp(m_old - m_new)
                pj = jnp.exp(s - m_new)
                l_s[:, lanes] = a * l_s[:, lanes] + jnp.sum(pj, axis=0, keepdims=True)
                scales.append(a)
            m_s[:, lanes] = m_new
            probs.append(pj.astype(BF16))
        pv = _dot(vT[j], jnp.concatenate(probs, axis=1))
        for i in range(j, nblk):
            here = slice((i - j) * W, (i - j + 1) * W)
            lanes = slice(i * W, (i + 1) * W)
            a = scales[i - j]
            acc_s[:, lanes] = pv[:, here] if a is None else a * acc_s[:, lanes] + pv[:, here]

    sub = lax.broadcasted_iota(I32, (2 * HEAD_DIM, B), 0)
    for i in range(nblk):
        o = acc_s[:, i * W:(i + 1) * W] / l_s[:, i * W:(i + 1) * W]
        o_ref[i * B:(i + 1) * B, :] = jnp.where(sub < HEAD_DIM, o[:, :B], o[:, B:]).T


def _moba_prompt(q, k, v, kmean, slopes, *, batch, seq):
    n, att_w = q.shape
    nblk = seq // MOBA_BLOCK
    B = MOBA_BLOCK
    kern = functools.partial(_moba_kernel, nblk=nblk)
    return pl.pallas_call(
        kern,
        grid_spec=pltpu.PrefetchScalarGridSpec(
            num_scalar_prefetch=1,
            grid=(batch, att_w // LANE),
            in_specs=[pl.BlockSpec((seq, LANE), lambda b, p, s: (b, p)),
                      pl.BlockSpec((seq, LANE), lambda b, p, s: (b, p)),
                      pl.BlockSpec((seq, LANE), lambda b, p, s: (b, p)),
                      pl.BlockSpec((nblk, LANE), lambda b, p, s: (b, p))],
            out_specs=pl.BlockSpec((seq, LANE), lambda b, p, s: (b, p)),
            scratch_shapes=[pltpu.VMEM((nblk, B, LANE), BF16), pltpu.VMEM((nblk, LANE, B), BF16),
                            pltpu.VMEM((nblk * 2 * B, LANE), BF16), pltpu.VMEM((1, nblk * 2 * B), F32),
                            pltpu.VMEM((1, nblk * 2 * B), F32), pltpu.VMEM((LANE, nblk * 2 * B), F32)]),
        out_shape=jax.ShapeDtypeStruct((n, att_w), F32),
        compiler_params=_params("arbitrary", "arbitrary"),
        name="moba_prompt",
    )(slopes, q, k, v, kmean)


def _memkv_kernel(x_ref, w_ref, mk_ref, mv_ref, *, mem_w):
    r = _dot(x_ref[...].astype(BF16), w_ref[...])
    mk_ref[...] = r[:, :mem_w]
    mv_ref[...] = r[:, mem_w:]


def _memkv(mem2d, w_bf, *, tile, mem_w):
    n, d = mem2d.shape
    return pl.pallas_call(
        functools.partial(_memkv_kernel, mem_w=mem_w),
        grid=(n // tile,),
        in_specs=[pl.BlockSpec((tile, d), lambda i: (i, 0)), pl.BlockSpec(w_bf.shape, lambda i: (0, 0))],
        out_specs=[pl.BlockSpec((tile, mem_w), lambda i: (i, 0))] * 2,
        out_shape=[jax.ShapeDtypeStruct((n, mem_w), F32)] * 2,
        compiler_params=_params("arbitrary"),
        name="memkv",
    )(mem2d, w_bf)


def _memattn_kernel(qm_ref, mk_ref, mv_ref, o_ref, *, tq, mem_w):
    sub = lax.broadcasted_iota(I32, (LANE, tq), 0)
    for pr in range(mem_w // LANE):
        cs = slice(LANE * pr, LANE * (pr + 1))
        qp = qm_ref[:, cs]
        mkp = mk_ref[:, cs].astype(BF16)
        mvT = mv_ref[:, cs].T.astype(BF16)
        outs = []
        for hh in range(2):
            qh = jnp.where(_head_mask(qp.shape, hh), qp, jnp.zeros_like(qp))
            s = _dot_nt(mkp, qh)
            m = jnp.max(s, axis=0, keepdims=True)
            e = jnp.exp(s - m)
            l = jnp.sum(e, axis=0, keepdims=True)
            outs.append(_dot(mvT, e.astype(BF16)) / l)
        o_ref[:, cs] = jnp.where(sub < HEAD_DIM, outs[0], outs[1]).T


def _memattn_prompt(qm, mk, mv, *, batch, seq, n_mem, tq):
    n, mem_w = qm.shape
    nq = seq // tq
    return pl.pallas_call(
        functools.partial(_memattn_kernel, tq=tq, mem_w=mem_w),
        grid=(batch, nq),
        in_specs=[pl.BlockSpec((tq, mem_w), lambda b, i: (b * nq + i, 0)),
                  pl.BlockSpec((n_mem, mem_w), lambda b, i: (b, 0)),
                  pl.BlockSpec((n_mem, mem_w), lambda b, i: (b, 0))],
        out_specs=pl.BlockSpec((tq, mem_w), lambda b, i: (b * nq + i, 0)),
        out_shape=jax.ShapeDtypeStruct((n, mem_w), F32),
        compiler_params=_params("arbitrary", "arbitrary"),
        name="memattn_prompt",
    )(qm, mk, mv)


SORT_TILE = 256


def _sort_rows(ts, n_exp):
    return TOP_K * ts + SUBLANE * n_exp


def _finish1_kernel(x_ref, oa_ref, oc_ref, om_ref, gmix_ref, wout_ref, g1_ref, b1_ref, wrT_ref, br_ref,
                    x1_ref, xs_ref, route_ref, gw_ref, cpad_ref, loff_ref, cbase_ref, tot_ref, carry,
                    *, tile, ts, n_exp, alpha):
    i = pl.program_id(0)
    lt = _sort_rows(ts, n_exp)

    @pl.when(i == 0)
    def _():
        carry[...] = jnp.zeros_like(carry)

    mix = jnp.concatenate([_rms(oa_ref[...]), _rms(oc_ref[...]), _rms(om_ref[...])], axis=-1) * gmix_ref[...]
    if wout_ref.dtype == BF16:
        z = alpha * x_ref[...] + _dot(mix.astype(BF16), wout_ref[...])
    else:
        z = alpha * x_ref[...] + _dot(mix, wout_ref[...], precision=HI)
    x1 = _layernorm(z, g1_ref[...], b1_ref[...])
    x1_ref[...] = x1
    x1b = x1.astype(BF16)

    g = _dot_nt(wrT_ref[...], x1, precision=HI) + br_ref[...]
    eidx = lax.broadcasted_iota(I32, (n_exp, tile), 0)
    picks, vals = [], []
    for k in range(TOP_K):
        m = jnp.max(g, axis=0, keepdims=True)
        idx = jnp.min(jnp.where(g == m, eidx, n_exp), axis=0, keepdims=True)
        pick = eidx == idx
        route_ref[k:k + 1, :] = idx
        picks.append(pick)
        vals.append(m)
        g = jnp.where(pick, -jnp.inf, g)
    ex = [jnp.exp(v - vals[0]) for v in vals]
    denom = ex[0] + ex[1] + ex[2] + ex[3]
    for k in range(TOP_K):
        gw_ref[k:k + 1, :] = ex[k] / denom
    gw_ref[TOP_K:, :] = jnp.zeros((gw_ref.shape[0] - TOP_K, tile), F32)

    t_src = lax.broadcasted_iota(I32, (ts, ts), 0)
    t_dst = lax.broadcasted_iota(I32, (ts, ts), 1)
    before = jnp.where(t_src < t_dst, 1.0, 0.0).astype(BF16)
    e_src = lax.broadcasted_iota(I32, (n_exp, n_exp), 1)
    e_dst = lax.broadcasted_iota(I32, (n_exp, n_exp), 0)
    lower = jnp.where(e_src < e_dst, 1.0, 0.0).astype(BF16)
    slot = lax.broadcasted_iota(I32, (lt, ts), 0)
    for sub in range(tile // ts):
        cs = slice(sub * ts, (sub + 1) * ts)
        pk = [p[:, cs] for p in picks]
        onehot = jnp.zeros((n_exp, ts), F32)
        for p in pk:
            onehot = onehot + jnp.where(p, 1.0, 0.0)
        cnt = jnp.sum(onehot, axis=1, keepdims=True)
        cpad = jnp.floor((cnt + (SUBLANE - 1)) * (1.0 / SUBLANE)) * SUBLANE
        cpad_l = jnp.broadcast_to(cpad, (n_exp, LANE))
        loff_l = _dot(lower, cpad_l.astype(BF16))
        pos = _dot(onehot.astype(BF16), before) + loff_l[:, 0:1]
        perm = jnp.zeros((lt, ts), F32)
        for k in range(TOP_K):
            lpos = jnp.sum(jnp.where(pk[k], pos, 0.0), axis=0, keepdims=True).astype(I32)
            route_ref[TOP_K + k:TOP_K + k + 1, cs] = lpos
            perm = perm + jnp.where(slot == lpos, 1.0, 0.0)
        perm = perm.astype(BF16)
        xs_ref[sub * lt:(sub + 1) * lt, :] = _dot(perm, x1b[cs, :])
        cpad_ref[sub] = cpad_l
        loff_ref[sub] = loff_l
        cbase_ref[sub] = jnp.broadcast_to(carry[...], (n_exp, LANE))
        carry[...] = carry[...] + cpad
    tot_ref[...] = jnp.broadcast_to(carry[...], tot_ref.shape)


def _finish1(x2d, oa, oc, om, g_mix, wout_bf, ln_g, ln_b, w_rT, b_r, *, tile, ts, alpha):
    n, d = x2d.shape
    n_exp = w_rT.shape[0]
    lt = _sort_rows(ts, n_exp)
    spt = tile // ts
    kern = functools.partial(_finish1_kernel, tile=tile, ts=ts, n_exp=n_exp, alpha=alpha)
    row = lambda w: pl.BlockSpec((tile, w), lambda i: (i, 0))
    full = lambda a: pl.BlockSpec(a.shape, lambda i: (0,) * a.ndim)
    meta = pl.BlockSpec((spt, n_exp, LANE), lambda i: (i, 0, 0))
    meta_shape = jax.ShapeDtypeStruct((n // ts, n_exp, LANE), F32)
    return pl.pallas_call(
        kern,
        grid=(n // tile,),
        in_specs=[row(d), row(oa.shape[1]), row(oc.shape[1]), row(om.shape[1]), full(g_mix), full(wout_bf),
                  full(ln_g), full(ln_b), full(w_rT), full(b_r)],
        out_specs=[row(d), pl.BlockSpec((spt * lt, d), lambda i: (i, 0)),
                   pl.BlockSpec((2 * TOP_K, tile), lambda i: (0, i)), pl.BlockSpec((2 * TOP_K, tile), lambda i: (0, i)),
                   meta, meta, meta, pl.BlockSpec((n_exp, LANE), lambda i: (0, 0))],
        out_shape=[jax.ShapeDtypeStruct((n, d), F32), jax.ShapeDtypeStruct((n // ts * lt, d), F32),
                   jax.ShapeDtypeStruct((2 * TOP_K, n), I32), jax.ShapeDtypeStruct((2 * TOP_K, n), F32),
                   meta_shape, meta_shape, meta_shape, jax.ShapeDtypeStruct((n_exp, LANE), F32)],
        scratch_shapes=[pltpu.VMEM((n_exp, 1), F32)],
        compiler_params=_params("arbitrary"),
        name="finish1",
    )(x2d, oa, oc, om, g_mix, wout_bf, ln_g, ln_b, w_rT, b_r)


def _run_sizes(ts):
    sizes, s = [], SUBLANE
    while s <= max(ts, SUBLANE):
        sizes.append(s)
        s *= 2
    return sizes


def _for_each_piece(length, sizes, fn):
    off = 0
    for sz in sizes:
        @pl.when((length & sz) != 0)
        def _(off=off, sz=sz):
            fn(off, sz)
        off = off + (length & sz)


def _dispatch_kernel(cpad_ref, loff_ref, base_ref, fill_ref, xs_ref, zero_ref, buf_ref, sem, fill_sem,
                     *, lt, n_exp, sizes, tm):
    t = pl.program_id(0)

    def run(e, wait):
        n = cpad_ref[t * n_exp + e]
        src = loff_ref[t * n_exp + e]
        dst = base_ref[t * n_exp + e]

        def piece(off, sz):
            cp = pltpu.make_async_copy(xs_ref.at[pl.ds(pl.multiple_of(src + off, SUBLANE), sz), :],
                                       buf_ref.at[pl.ds(pl.multiple_of(dst + off, SUBLANE), sz), :], sem)
            cp.wait() if wait else cp.start()

        _for_each_piece(n, sizes, piece)

    def issue(e, c):
        run(e, False)
        return c

    def drain(e, c):
        run(e, True)
        return c

    lax.fori_loop(0, n_exp, issue, 0)

    @pl.when(t == pl.num_programs(0) - 1)
    def _():
        start = fill_ref[0]
        rest = buf_ref.shape[0] - start
        n_full = rest // tm

        def chunk(c, wait):
            cp = pltpu.make_async_copy(zero_ref, buf_ref.at[pl.ds(pl.multiple_of(start + c * tm, SUBLANE), tm), :],
                                       fill_sem)
            cp.wait() if wait else cp.start()

        def piece(wait):
            def fn(off, sz):
                cp = pltpu.make_async_copy(
                    zero_ref.at[pl.ds(0, sz), :],
                    buf_ref.at[pl.ds(pl.multiple_of(start + n_full * tm + off, SUBLANE), sz), :], fill_sem)
                cp.wait() if wait else cp.start()
            return fn

        tail_sizes = [s for s in _run_sizes(tm) if s < tm]
        for wait in (False, True):
            lax.fori_loop(0, n_full, lambda c, carry, wait=wait: (chunk(c, wait), carry)[1], 0)
            _for_each_piece(rest - n_full * tm, tail_sizes, piece(wait))

    lax.fori_loop(0, n_exp, drain, 0)


def _dispatch(cpad, loff, base, fill, xs, *, ts, n_exp, m_pad, tm):
    w = xs.shape[1]
    lt = _sort_rows(ts, n_exp)
    n_sub = xs.shape[0] // lt
    zero = jnp.zeros((tm, w), F32)
    return pl.pallas_call(
        functools.partial(_dispatch_kernel, lt=lt, n_exp=n_exp, sizes=_run_sizes(ts), tm=tm),
        grid_spec=pltpu.PrefetchScalarGridSpec(
            num_scalar_prefetch=4,
            grid=(n_sub,),
            in_specs=[pl.BlockSpec((lt, w), lambda t, *_: (t, 0)), pl.BlockSpec((tm, w), lambda t, *_: (0, 0))],
            out_specs=pl.BlockSpec(memory_space=pl.ANY),
            scratch_shapes=[pltpu.SemaphoreType.DMA(()), pltpu.SemaphoreType.DMA(())]),
        out_shape=jax.ShapeDtypeStruct((m_pad, w), F32),
        compiler_params=_params("arbitrary"),
        name="dispatch",
    )(cpad, loff, base, fill, xs, zero)


def _ffn_kernel(otile_ref, tile_ref, exp_ref, lo_ref, hi_ref, lhs_ref, wgu_ref, bgu_ref, wd_ref, bd_ref, out_ref,
                wgu_bf, wd_bf, *, tm, d_ff):
    g = pl.program_id(0)
    lo = lo_ref[g]
    hi = hi_ref[g]

    @pl.when(hi == lo)
    def _():
        out_ref[...] = jnp.zeros_like(out_ref)

    new_expert = jnp.logical_or(g == 0, exp_ref[g] != exp_ref[jnp.maximum(g - 1, 0)])

    @pl.when(jnp.logical_and(new_expert, hi > lo))
    def _():
        wgu_bf[...] = wgu_ref[0].astype(BF16)
        wd_bf[...] = wd_ref[0].astype(BF16)

    @pl.when(hi > lo)
    def _():
        gu = _dot(lhs_ref[...].astype(BF16), wgu_bf[...]) + bgu_ref[0]
        gate = jnp.minimum(gu[:, :d_ff], SWIGLU_LIMIT)
        up = jnp.clip(gu[:, d_ff:], -SWIGLU_LIMIT, SWIGLU_LIMIT)
        hid = (up + 1.0) * gate * jax.nn.sigmoid(SWIGLU_ALPHA * gate)
        o = _dot(hid.astype(BF16), wd_bf[...]) + bd_ref[0]

        @pl.when(lo == 0)
        def _():
            out_ref[...] = o

        @pl.when(lo > 0)
        def _():
            rows = lax.broadcasted_iota(I32, (tm, 1), 0)
            out_ref[...] = jnp.where((rows >= lo) & (rows < hi), o, out_ref[...])


def _ffn(sched, buf, w_gu, b_gu, w_d, b_d, *, tm):
    m, w = buf.shape
    n_exp, d, d_ff2 = w_gu.shape
    out_tiles, tiles, experts, los, his = sched
    return pl.pallas_call(
        functools.partial(_ffn_kernel, tm=tm, d_ff=d_ff2 // 2),
        grid_spec=pltpu.PrefetchScalarGridSpec(
            num_scalar_prefetch=5,
            grid=(tiles.shape[0],),
            in_specs=[pl.BlockSpec((tm, w), lambda g, ot, t, e, lo, hi: (t[g], 0)),
                      pl.BlockSpec((1, d, d_ff2), lambda g, ot, t, e, lo, hi: (e[g], 0, 0)),
                      pl.BlockSpec((1, 1, d_ff2), lambda g, ot, t, e, lo, hi: (e[g], 0, 0)),
                      pl.BlockSpec((1, d_ff2 // 2, d), lambda g, ot, t, e, lo, hi: (e[g], 0, 0)),
                      pl.BlockSpec((1, 1, d), lambda g, ot, t, e, lo, hi: (e[g], 0, 0))],
            out_specs=pl.BlockSpec((tm, d), lambda g, ot, t, e, lo, hi: (ot[g], 0)),
            scratch_shapes=[pltpu.VMEM((d, d_ff2), BF16), pltpu.VMEM((d_ff2 // 2, d), BF16)]),
        out_shape=jax.ShapeDtypeStruct((m, d), F32),
        compiler_params=_params("arbitrary"),
        name="expert_ffn",
    )(out_tiles, tiles, experts, los, his, buf, w_gu, b_gu, w_d, b_d)


def _moe_schedule(counts, n_rows, tm):
    n_exp = counts.shape[0]
    n_tiles = n_rows // tm
    n_items = n_tiles + n_exp - 1
    ends = jnp.cumsum(counts)
    starts = ends - counts
    first_tile = starts // tm
    n_it = jnp.where(counts > 0, (ends - 1) // tm - first_tile + 1, 0)
    it_end = jnp.cumsum(n_it)
    it_start = it_end - n_it
    item = jnp.arange(n_items, dtype=I32)
    n_real = it_end[-1]
    g = jnp.minimum(item, n_real - 1)
    e = jnp.minimum(jnp.sum((it_end[None, :] <= g[:, None]).astype(I32), axis=1), n_exp - 1)
    of_e = e[:, None] == jnp.arange(n_exp, dtype=I32)[None, :]
    at_e = lambda table: jnp.sum(jnp.where(of_e, table[None, :], 0), axis=1)
    tile = (at_e(first_tile) + g - at_e(it_start)).astype(I32)
    valid = item < n_real
    lo = jnp.where(valid, jnp.clip(at_e(starts) - tile * tm, 0, tm), 0).astype(I32)
    hi = jnp.where(valid, jnp.clip(at_e(ends) - tile * tm, 0, tm), 0).astype(I32)
    used = (ends[-1] + tm - 1) // tm
    out_tile = jnp.where(valid, tile, jnp.minimum(used + item - n_real, n_tiles - 1)).astype(I32)
    return (out_tile, tile, e, lo, hi), starts.astype(I32)


def _combine_kernel(cpad_ref, loff_ref, base_ref, route_ref, gw_ref, x1_ref, g2_ref, b2_ref, eo_ref, y_ref,
                    gbuf, sem, *, ts, n_exp, sizes, alpha):
    t = pl.program_id(0)
    nt = pl.num_programs(0)
    lt = gbuf.shape[1]
    d = gbuf.shape[2]

    def tile_copies(tt, slot, wait):
        def run(e, c):
            n = cpad_ref[tt * n_exp + e]
            src = base_ref[tt * n_exp + e]
            dst = loff_ref[tt * n_exp + e]

            def piece(off, sz):
                cp = pltpu.make_async_copy(eo_ref.at[pl.ds(pl.multiple_of(src + off, SUBLANE), sz), :],
                                           gbuf.at[slot, pl.ds(pl.multiple_of(dst + off, SUBLANE), sz), :],
                                           sem.at[slot])
                cp.wait() if wait else cp.start()

            _for_each_piece(n, sizes, piece)
            return c

        lax.fori_loop(0, n_exp, run, 0, unroll=4)

    def fetch(tt, slot):
        gbuf[slot, TOP_K * ts:, :] = jnp.zeros((lt - TOP_K * ts, d), F32)
        tile_copies(tt, slot, False)

    slot = t % 2

    @pl.when(t == 0)
    def _():
        fetch(t, slot)

    @pl.when(t + 1 < nt)
    def _():
        fetch(t + 1, 1 - slot)

    tile_copies(t, slot, True)

    r = lax.broadcasted_iota(I32, (ts, ts), 0)
    c = lax.broadcasted_iota(I32, (ts, ts), 1)
    eye = jnp.where(r == c, 1.0, 0.0)
    wcol = _dot_nt(eye, gw_ref[...], precision=HI)
    pcol = _dot_nt(eye, route_ref[...].astype(F32), precision=HI)
    rows = gbuf[slot].astype(BF16)
    slot_i = lax.broadcasted_iota(I32, (ts, lt), 1)
    unsort = jnp.zeros((ts, lt), F32)
    for k in range(TOP_K):
        lpos = pcol[:, TOP_K + k:TOP_K + k + 1].astype(I32)
        unsort = unsort + jnp.where(slot_i == lpos, wcol[:, k:k + 1], 0.0)
    hi = unsort.astype(BF16)
    lo = (unsort - hi.astype(F32)).astype(BF16)
    moe = _dot(hi, rows) + _dot(lo, rows)
    y_ref[...] = _layernorm(alpha * x1_ref[...] + moe, g2_ref[...], b2_ref[...])


def _combine(cpad, loff, base, route, gw, x1, ln_g, ln_b, eo, *, ts, n_exp, alpha):
    n, d = x1.shape
    lt = _sort_rows(ts, n_exp)
    full = lambda a: pl.BlockSpec(a.shape, lambda i, *_: (0,) * a.ndim)
    return pl.pallas_call(
        functools.partial(_combine_kernel, ts=ts, n_exp=n_exp, sizes=_run_sizes(ts), alpha=alpha),
        grid_spec=pltpu.PrefetchScalarGridSpec(
            num_scalar_prefetch=3,
            grid=(n // ts,),
            in_specs=[pl.BlockSpec((2 * TOP_K, ts), lambda i, *_: (0, i)),
                      pl.BlockSpec((2 * TOP_K, ts), lambda i, *_: (0, i)),
                      pl.BlockSpec((ts, d), lambda i, *_: (i, 0)), full(ln_g), full(ln_b),
                      pl.BlockSpec(memory_space=pl.ANY)],
            out_specs=pl.BlockSpec((ts, d), lambda i, *_: (i, 0)),
            scratch_shapes=[pltpu.VMEM((2, lt, d), F32), pltpu.SemaphoreType.DMA((2,))]),
        out_shape=jax.ShapeDtypeStruct((n, d), F32),
        compiler_params=_params("arbitrary"),
        name="combine",
    )(cpad, loff, base, route, gw, x1, ln_g, ln_b, eo)


def _finish(x2d, oa, oc, om, wts, w_out, *, tile, tm, alpha):
    n = x2d.shape[0]
    n_exp = wts["w_rT"].shape[0]
    ts = _pick_tile(n, SORT_TILE)
    n_sub = n // ts
    x1, xs, route, gw, cpad, loff, cbase, tot = _finish1(
        x2d, oa, oc, om, wts["g_mix"], w_out, wts["ln1_g"], wts["ln1_b"], wts["w_rT"], wts["b_r"],
        tile=_pick_tile(n, tile), ts=ts, alpha=alpha)
    m_pad = n_sub * _sort_rows(ts, n_exp) + tm
    counts = tot[:, 0].astype(I32)
    sched, offs = _moe_schedule(counts, m_pad, tm)
    flat = lambda a: a[:, :, 0].astype(I32).reshape(-1)
    cpad, loff = flat(cpad), flat(loff)
    base = (cbase[:, :, 0].astype(I32) + offs[None, :]).reshape(-1)
    fill = jnp.sum(counts, keepdims=True)
    buf = _dispatch(cpad, loff, base, fill, xs, ts=ts, n_exp=n_exp, m_pad=m_pad, tm=tm)
    eo = _ffn(sched, buf, wts["w_gu"], wts["b_gu"], wts["w_d"], wts["b_d"], tm=tm)
    return _combine(cpad, loff, base, route, gw, x1, wts["ln2_g"], wts["ln2_b"], eo, ts=ts, n_exp=n_exp, alpha=alpha)


def _sample_inproj_kernel(x_ref, w_ref, wc_ref, p0_ref, p1_ref, q_ref, k_ref, v_ref, qm_ref, oc_ref, u_ref,
                          *, att_w, mem_w, conv_w):
    x = x_ref[...]
    c1, c2, c3 = att_w, 2 * att_w, 3 * att_w
    c4 = c3 + mem_w
    c5, c6, c7 = c4 + conv_w, c4 + 2 * conv_w, c4 + 3 * conv_w

    def proj(lo, hi):
        return _dot(x, w_ref[:, lo:hi], precision=HI)

    q_ref[...] = proj(0, c1) * Q_SCALE
    k_ref[...] = proj(c1, c2)
    v_ref[...] = proj(c2, c3)
    qm_ref[...] = proj(c3, c4) * Q_SCALE
    u = proj(c5, c6) * proj(c6, c7)
    cz = wc_ref[0:1, :] * p0_ref[...] + wc_ref[1:2, :] * p1_ref[...] + wc_ref[2:3, :] * u
    oc_ref[...] = proj(c4, c5) * cz
    u_ref[...] = u


def _sample_inproj(x2d, w_bf, w_conv, prev0, prev1, *, att_w, mem_w, conv_w):
    n = x2d.shape[0]
    args = (x2d, w_bf, w_conv, prev0, prev1)
    full = lambda a: pl.BlockSpec(a.shape, lambda i: (0,) * a.ndim)
    widths = (att_w, att_w, att_w, mem_w, conv_w, conv_w)
    return pl.pallas_call(
        functools.partial(_sample_inproj_kernel, att_w=att_w, mem_w=mem_w, conv_w=conv_w),
        grid=(1,),
        in_specs=[full(a) for a in args],
        out_specs=[pl.BlockSpec((n, w), lambda i: (0, 0)) for w in widths],
        out_shape=[jax.ShapeDtypeStruct((n, w), F32) for w in widths],
        compiler_params=_params("arbitrary"),
        name="sample_inproj",
    )(*args)


def _kscan_kernel(pt_ref, q_ref, ck_ref, sel_ref, kbuf, ksum, sem, *, n_pages, chunk, n_heads):
    b = pl.program_id(0)
    n_chunks = n_pages // chunk
    pages_per_blk = MOBA_BLOCK // PAGE_SIZE
    nblk = ksum.shape[0]
    q_col = q_ref[0]

    def copies(bb, c, slot):
        return [pltpu.make_async_copy(ck_ref.at[0, pt_ref[bb * n_pages + c * chunk + j]], kbuf.at[slot, j],
                                      sem.at[slot]) for j in range(chunk)]

    @pl.when(b == 0)
    def _():
        for cp in copies(b, 0, 0):
            cp.start()

    def body(c, carry):
        slot = c % 2
        last = c + 1 == n_chunks

        @pl.when(jnp.logical_or(jnp.logical_not(last), b + 1 < pl.num_programs(0)))
        def _():
            for cp in copies(jnp.where(last, b + 1, b), jnp.where(last, 0, c + 1), 1 - slot):
                cp.start()

        for cp in copies(b, c, slot):
            cp.wait()
        for jb in range(chunk // pages_per_blk):
            s = kbuf[slot, pages_per_blk * jb]
            for pg in range(1, pages_per_blk):
                s = s + kbuf[slot, pages_per_blk * jb + pg]
            ksum[c * (chunk // pages_per_blk) + jb] = jnp.sum(s * q_col, axis=1)
        return carry

    lax.fori_loop(0, n_chunks, body, 0)

    g = jnp.sum(ksum[...], axis=2, keepdims=True) * (1.0 / MOBA_BLOCK)
    blk = lax.broadcasted_iota(I32, g.shape, 0)
    sel_ref[...] = jnp.zeros(sel_ref.shape, I32)
    for r in range(MOBA_TOPK):
        m = jnp.max(g, axis=0, keepdims=True)
        idx = jnp.min(jnp.where((g == m) & (g > -jnp.inf), blk, nblk), axis=0, keepdims=True)
        sel_ref[0, :, r:r + 1] = idx[0]
        g = jnp.where(blk == idx, -jnp.inf, g)


def _kscan(pt_flat, q_col, ckT, *, n_pages, chunk):
    bs, n_heads, hd, _ = q_col.shape
    nblk = n_pages * PAGE_SIZE // MOBA_BLOCK
    return pl.pallas_call(
        functools.partial(_kscan_kernel, n_pages=n_pages, chunk=chunk, n_heads=n_heads),
        grid_spec=pltpu.PrefetchScalarGridSpec(
            num_scalar_prefetch=1,
            grid=(bs,),
            in_specs=[pl.BlockSpec((1, n_heads, hd, 1), lambda b, pt: (b, 0, 0, 0)), pl.BlockSpec(memory_space=pl.ANY)],
            out_specs=pl.BlockSpec((1, n_heads, LANE), lambda b, pt: (b, 0, 0)),
            scratch_shapes=[pltpu.VMEM((2, chunk, n_heads, hd, PAGE_SIZE), F32),
                            pltpu.VMEM((nblk, n_heads, PAGE_SIZE), F32), pltpu.SemaphoreType.DMA((2,))]),
        out_shape=jax.ShapeDtypeStruct((bs, n_heads, LANE), I32),
        compiler_params=_params("arbitrary"),
        name="kscan",
    )(pt_flat, q_col, ckT)


def _sample_attn_kernel(pt_ref, sel_ref, slopes_ref, q_ref, kn_ref, vn_ref, qm_ref, mk_ref, mv_ref, ck_ref, cv_ref,
                        oa_ref, om_ref, kbuf, vbuf, sem, *, n_pages, n_heads, past_len, mem_w):
    b = pl.program_id(0)
    nb = pl.num_programs(0)
    B = MOBA_BLOCK
    pages_per_blk = B // PAGE_SIZE
    n_keys = MOBA_TOPK * B

    def blocks(bb, h):
        return [sel_ref[(bb * n_heads + h) * MOBA_TOPK + s] for s in range(MOBA_TOPK)]

    n_sel_pages = MOBA_TOPK * pages_per_blk

    def copies(bb, slot):
        cps = []
        for h in range(n_heads):
            blks = blocks(bb, h)
            for s in range(MOBA_TOPK):
                for half in range(pages_per_blk):
                    pg = pt_ref[bb * n_pages + pages_per_blk * blks[s] + half]
                    pp = s * pages_per_blk + half
                    cps.append(pltpu.make_async_copy(ck_ref.at[0, pg, h], kbuf.at[slot, h, pp], sem.at[0, slot]))
                    cps.append(pltpu.make_async_copy(cv_ref.at[0, pg, h], vbuf.at[slot, h, pp], sem.at[1, slot]))
        return cps

    slot = b % 2

    @pl.when(b == 0)
    def _():
        for cp in copies(b, slot):
            cp.start()

    @pl.when(b + 1 < nb)
    def _():
        for cp in copies(b + 1, 1 - slot):
            cp.start()

    for cp in copies(b, slot):
        cp.wait()

    key_lane = lax.broadcasted_iota(I32, (1, n_keys), 1)
    lane = lax.broadcasted_iota(I32, (1, LANE), 1)
    for h in range(n_heads):
        blks = blocks(b, h)
        qh = q_ref[0, h:h + 1, :]
        q8 = jnp.broadcast_to(qh, (SUBLANE, HEAD_DIM))
        blk_of_key = jnp.where(key_lane < B, blks[0], jnp.where(key_lane < 2 * B, blks[1], blks[2]))
        dist = (past_len - blk_of_key * B - (key_lane & (B - 1))).astype(F32)
        s = jnp.concatenate([_dot(q8.astype(BF16), kbuf[slot, h, pp].astype(BF16)) for pp in range(n_sel_pages)],
                            axis=1)
        s = s - slopes_ref[h] * dist
        s_self = jnp.sum(qh * kn_ref[0, h:h + 1, :], axis=1, keepdims=True)
        m = jnp.maximum(jnp.max(s, axis=1, keepdims=True), s_self)
        e = jnp.exp(s - m)
        e_self = jnp.exp(s_self - m)
        l = jnp.sum(e, axis=1, keepdims=True) + e_self
        o = e_self * vn_ref[0, h:h + 1, :]
        for pp in range(n_sel_pages):
            o = o + _dot_nt(e[:, pp * PAGE_SIZE:(pp + 1) * PAGE_SIZE].astype(BF16), vbuf[slot, h, pp].astype(BF16))
        oa_ref[0, h:h + 1, :] = (o / l)[0:1, :]

    for h in range(mem_w // HEAD_DIM):
        q8 = jnp.broadcast_to(qm_ref[0, h:h + 1, :], (SUBLANE, HEAD_DIM))
        s = _dot(q8, mk_ref[0, 0, h], precision=HI)
        e = jnp.exp(s - jnp.max(s, axis=1, keepdims=True))
        o = _dot_nt(e, mv_ref[0, 0, h], precision=HI) / jnp.sum(e, axis=1, keepdims=True)
        om_ref[0, h:h + 1, :] = o[0:1, :]


def _sample_attn(pt_flat, sel_flat, slopes, q3, kn3, vn3, qm3, mkT, mvT, ck, cv, *, n_pages):
    bs, n_heads, hd = q3.shape
    mem_heads = qm3.shape[1]
    mem_w = mem_heads * hd
    n_keys = MOBA_TOPK * MOBA_BLOCK
    per_b = lambda a: pl.BlockSpec((1,) + a.shape[1:], lambda b, pt, sel: (b,) + (0,) * (a.ndim - 1))
    mem_b = lambda a: pl.BlockSpec((1, 1) + a.shape[2:], lambda b, pt, sel: (0, b) + (0,) * (a.ndim - 2))
    return pl.pallas_call(
        functools.partial(_sample_attn_kernel, n_pages=n_pages, n_heads=n_heads, past_len=n_pages * PAGE_SIZE,
                          mem_w=mem_w),
        grid_spec=pltpu.PrefetchScalarGridSpec(
            num_scalar_prefetch=2,
            grid=(bs,),
            in_specs=[pl.BlockSpec(memory_space=pltpu.SMEM), per_b(q3), per_b(kn3), per_b(vn3), per_b(qm3),
                      mem_b(mkT), mem_b(mvT), pl.BlockSpec(memory_space=pl.ANY), pl.BlockSpec(memory_space=pl.ANY)],
            out_specs=[pl.BlockSpec((1, n_heads, hd), lambda b, pt, sel: (b, 0, 0)),
                       pl.BlockSpec((1, mem_heads, hd), lambda b, pt, sel: (b, 0, 0))],
            scratch_shapes=[pltpu.VMEM((2, n_heads, n_keys // PAGE_SIZE, hd, PAGE_SIZE), F32),
                            pltpu.VMEM((2, n_heads, n_keys // PAGE_SIZE, hd, PAGE_SIZE), F32),
                            pltpu.SemaphoreType.DMA((2, 2))]),
        out_shape=[jax.ShapeDtypeStruct((bs, n_heads, hd), F32), jax.ShapeDtypeStruct((bs, mem_heads, hd), F32)],
        compiler_params=_params("arbitrary"),
        name="sample_attn",
    )(pt_flat, sel_flat, slopes, q3, kn3, vn3, qm3, mkT, mvT, ck, cv)


def _sample_layer(x, cache_k, cache_v, mem_k, mem_v, state, page_table, wts, dims, alpha):
    bs, dec_seq, d = x.shape
    assert dec_seq == 1
    att_w, mem_w, conv_w = dims
    n_heads = att_w // HEAD_DIM
    n_pages = page_table.shape[1]
    assert (n_pages * PAGE_SIZE) % MOBA_BLOCK == 0 and n_pages * PAGE_SIZE // MOBA_BLOCK >= MOBA_TOPK
    x2d = x.reshape(bs, d)
    q, k, v, qm, oc, u = _sample_inproj(x2d, wts["w_in_f32"], wts["w_conv"], state[:, 0, :], state[:, 1, :],
                                        att_w=att_w, mem_w=mem_w, conv_w=conv_w)
    pt_flat = page_table.reshape(-1)
    heads = lambda a: a.reshape(bs, n_heads, HEAD_DIM)
    pagesT = lambda c: jnp.transpose(c, (0, 1, 3, 4, 2))
    chunk = 16 if n_pages % 32 == 0 else 8
    assert n_pages % (2 * chunk) == 0
    sel = _kscan(pt_flat, q.reshape(bs, n_heads, HEAD_DIM, 1), pagesT(cache_k), n_pages=n_pages, chunk=chunk)
    sel_flat = sel[:, :, :MOBA_TOPK].reshape(-1)
    oa, om = _sample_attn(pt_flat, sel_flat, wts["slopes"], heads(q), heads(k), heads(v),
                          qm.reshape(bs, mem_w // HEAD_DIM, HEAD_DIM), pagesT(mem_k), pagesT(mem_v), pagesT(cache_k),
                          pagesT(cache_v), n_pages=n_pages)
    y = _finish(x2d, oa.reshape(bs, att_w), oc, om.reshape(bs, mem_w), wts, wts["w_out_f32"], tile=bs,
                tm=bs * TOP_K, alpha=alpha)
    conv_state = jnp.stack([state[:, 1, :], u], axis=1)
    return y.reshape(bs, 1, d), k, v, conv_state


def _pick_tile(n, pref):
    return pref if n % pref == 0 else n


def _prompt_layer(x, mem, wts, dims, alpha):
    batch, seq, d = x.shape
    att_w, mem_w, conv_w = dims
    n = batch * seq
    n_mem = mem.shape[1]
    x2d = x.reshape(n, d)
    tile = 512
    q, k, v, kT, vT, kmean, qm, oc, tail = _inproj(x2d, wts["w_in"], wts["w_conv"], seq=seq, tile=tile, att_w=att_w,
                                                   mem_w=mem_w, conv_w=conv_w)
    oa = _moba_prompt(q, k, v, kmean.reshape(n // MOBA_BLOCK, att_w), wts["slopes"], batch=batch, seq=seq)
    mk, mv = _memkv(mem.reshape(batch * n_mem, d), wts["w_mem_kv"], tile=_pick_tile(batch * n_mem, 512), mem_w=mem_w)
    om = _memattn_prompt(qm, mk, mv, batch=batch, seq=seq, n_mem=n_mem, tq=512)
    y = _finish(x2d, oa, oc, om, wts, wts["w_out"], tile=512, tm=512, alpha=alpha)
    conv_state = tail.reshape(batch, seq // tile, SUBLANE, conv_w)[:, -1, SUBLANE - (CONV_K - 1):, :]
    rows = lambda t: jnp.transpose(t, (0, 3, 1, 2))
    return y.reshape(batch, seq, d), rows(kT), rows(vT), conv_state, mk, mv


def _prep_weights(l, w_in, w_mem_kv, w_conv, g_mix, w_out, ln1_g, ln1_b, w_router, b_router, w_gate_up, b_gate_up,
                  w_down, b_down, ln2_g, ln2_b):
    n_heads = 8
    row = lambda a: a[l][None, :]
    return {
        "w_in": w_in[l].astype(BF16), "w_in_f32": w_in[l], "w_mem_kv": w_mem_kv[l].astype(BF16), "w_conv": w_conv[l],
        "g_mix": row(g_mix), "w_out": w_out[l].astype(BF16), "w_out_f32": w_out[l], "ln1_g": row(ln1_g),
        "ln1_b": row(ln1_b),
        "w_rT": w_router[l].T, "b_r": b_router[l][:, None],
        "w_gu": w_gate_up[l], "b_gu": b_gate_up[l][:, None, :],
        "w_d": w_down[l], "b_d": b_down[l][:, None, :],
        "ln2_g": row(ln2_g), "ln2_b": row(ln2_b),
        "slopes": 2.0 ** (-8.0 * jnp.arange(1, n_heads + 1, dtype=F32) / n_heads),
    }


def kernel(x_prompt, x_sample, cache_k, cache_v, cache_mem_k, cache_mem_v, state_conv, page_table, mem_prompt, w_in, w_mem_kv, w_conv, g_mix, w_out, ln1_g, ln1_b, w_router, b_router, w_gate_up, b_gate_up, w_down, b_down, ln2_g, ln2_b):
    depth = w_in.shape[0]
    assert depth == 1
    alpha = (2 * depth) ** 0.25
    n_heads = cache_k.shape[3]
    att_w = n_heads * HEAD_DIM
    mem_w = cache_mem_k.shape[3] * HEAD_DIM
    conv_w = state_conv.shape[3]
    dims = (att_w, mem_w, conv_w)
    wts = _prep_weights(0, w_in, w_mem_kv, w_conv, g_mix, w_out, ln1_g, ln1_b, w_router, b_router, w_gate_up,
                        b_gate_up, w_down, b_down, ln2_g, ln2_b)
    bp, seq, d = x_prompt.shape
    n_mem = mem_prompt.shape[1]
    y_p, k_p, v_p, conv_p, mk_p, mv_p = _prompt_layer(x_prompt, mem_prompt, wts, dims, alpha)
    bs = x_sample.shape[0]
    y_s, k_s, v_s, conv_s = _sample_layer(x_sample, cache_k, cache_v, cache_mem_k, cache_mem_v,
                                          state_conv[0], page_table, wts, dims, alpha)
    return (y_p, y_s,
            k_p[None], v_p[None], conv_p[None], mk_p.reshape(1, bp, n_mem, mem_w // HEAD_DIM, HEAD_DIM),
            mv_p.reshape(1, bp, n_mem, mem_w // HEAD_DIM, HEAD_DIM),
            k_s.reshape(1, bs, 1, n_heads, HEAD_DIM), v_s.reshape(1, bs, 1, n_heads, HEAD_DIM), conv_s[None])
```

```python
import functools

import jax
import jax.numpy as jnp
from jax import lax
from jax.experimental import pallas as pl
from jax.experimental.pallas import tpu as pltpu

F32 = jnp.float32
BF16 = jnp.bfloat16
I32 = jnp.int32

HEAD_DIM = 64
MOBA_BLOCK = 256
MOBA_TOPK = 3
PAGE_SIZE = 128
TOP_K = 4
CONV_K = 3
SWIGLU_LIMIT = 7.0
SWIGLU_ALPHA = 1.702
LN_EPS = 1e-5
Q_SCALE = HEAD_DIM ** -0.5
NEG = -1e30
LANE = 128
SUBLANE = 8
VMEM_LIMIT = 56 * 1024 * 1024
HI = lax.Precision.HIGHEST


def _params(*sem):
    return pltpu.CompilerParams(dimension_semantics=sem, vmem_limit_bytes=VMEM_LIMIT)


def _dot_nt(a, b, precision=None):
    return lax.dot_general(a, b, (((1,), (1,)), ((), ())), precision=precision, preferred_element_type=F32)


def _dot(a, b, precision=None):
    return jnp.dot(a, b, precision=precision, preferred_element_type=F32)


def _layernorm(z, g, b):
    zc = z - jnp.mean(z, axis=-1, keepdims=True)
    var = jnp.mean(zc * zc, axis=-1, keepdims=True)
    return zc * lax.rsqrt(var + LN_EPS) * g + b


def _rms(a):
    return a * lax.rsqrt(jnp.mean(a * a, axis=-1, keepdims=True) + LN_EPS)


def _head_mask(shape, hh):
    lane = lax.broadcasted_iota(I32, shape, len(shape) - 1)
    return (lane >= HEAD_DIM * hh) & (lane < HEAD_DIM * (hh + 1))


def _inproj_kernel(x_ref, w_ref, wc_ref, q_ref, k_ref, v_ref, kt_ref, vt_ref, km_ref, qm_ref, oc_ref, tail_ref, ubuf,
                   *, tile, tiles_per_seq, att_w, mem_w, conv_w):
    i = pl.program_id(0)
    x = x_ref[...].astype(BF16)
    c1, c2, c3 = att_w, 2 * att_w, 3 * att_w
    c4 = c3 + mem_w
    c5, c6, c7 = c4 + conv_w, c4 + 2 * conv_w, c4 + 3 * conv_w
    n_heads = att_w // HEAD_DIM

    def proj(lo, hi):
        return _dot(x, w_ref[:, lo:hi])

    q_ref[...] = (proj(0, c1) * Q_SCALE).astype(BF16)
    k = proj(c1, c2)
    k_ref[...] = k
    kt_ref[0] = k.T.reshape(n_heads, HEAD_DIM, tile)
    km_ref[0] = jnp.sum(k.reshape(tile // MOBA_BLOCK, MOBA_BLOCK, att_w), axis=1) * (1.0 / MOBA_BLOCK)
    v = proj(c2, c3)
    v_ref[...] = v
    vt_ref[0] = v.T.reshape(n_heads, HEAD_DIM, tile)
    qm_ref[...] = (proj(c3, c4) * Q_SCALE).astype(BF16)
    gb = proj(c4, c5)
    u = proj(c5, c6) * proj(c6, c7)

    @pl.when(i % tiles_per_seq == 0)
    def _():
        ubuf[0:SUBLANE, :] = jnp.zeros((SUBLANE, conv_w), F32)

    ubuf[SUBLANE:SUBLANE + tile, :] = u
    u1 = ubuf[SUBLANE - 1:SUBLANE - 1 + tile, :]
    u2 = ubuf[SUBLANE - 2:SUBLANE - 2 + tile, :]
    cz = wc_ref[0:1, :] * u2 + wc_ref[1:2, :] * u1 + wc_ref[2:3, :] * u
    oc_ref[...] = gb * cz
    tail = ubuf[tile:tile + SUBLANE, :]
    tail_ref[0] = tail
    ubuf[0:SUBLANE, :] = tail


def _inproj(x2d, w_bf, w_conv, *, seq, tile, att_w, mem_w, conv_w):
    n, d = x2d.shape
    nt = n // tile
    kern = functools.partial(_inproj_kernel, tile=tile, tiles_per_seq=seq // tile, att_w=att_w, mem_w=mem_w,
                             conv_w=conv_w)
    row = lambda w: pl.BlockSpec((tile, w), lambda i: (i, 0))
    tps = seq // tile
    n_heads = att_w // HEAD_DIM
    headsT = pl.BlockSpec((1, n_heads, HEAD_DIM, tile), lambda i: (i // tps, 0, 0, i % tps))
    headsT_shape = jax.ShapeDtypeStruct((n // seq, n_heads, HEAD_DIM, seq), F32)
    return pl.pallas_call(
        kern,
        grid=(nt,),
        in_specs=[row(d), pl.BlockSpec(w_bf.shape, lambda i: (0, 0)), pl.BlockSpec(w_conv.shape, lambda i: (0, 0))],
        out_specs=[row(att_w), row(att_w), row(att_w), headsT, headsT,
                   pl.BlockSpec((1, tile // MOBA_BLOCK, att_w), lambda i: (i, 0, 0)),
                   row(mem_w), row(conv_w), pl.BlockSpec((1, SUBLANE, conv_w), lambda i: (i, 0, 0))],
        out_shape=[jax.ShapeDtypeStruct((n, att_w), BF16), jax.ShapeDtypeStruct((n, att_w), F32),
                   jax.ShapeDtypeStruct((n, att_w), F32), headsT_shape, headsT_shape,
                   jax.ShapeDtypeStruct((nt, tile // MOBA_BLOCK, att_w), F32),
                   jax.ShapeDtypeStruct((n, mem_w), BF16), jax.ShapeDtypeStruct((n, conv_w), F32),
                   jax.ShapeDtypeStruct((nt, SUBLANE, conv_w), F32)],
        scratch_shapes=[pltpu.VMEM((tile + SUBLANE, conv_w), F32)],
        compiler_params=_params("arbitrary"),
        name="inproj",
    )(x2d, w_bf, w_conv)


def _moba_kernel(slopes_ref, q_ref, k_ref, v_ref, km_ref, o_ref, kbf, vT, q2_s, m_s, l_s, acc_s, *, nblk):
    p = pl.program_id(1)
    B = MOBA_BLOCK
    W = 2 * B

    for c in range(nblk):
        kbf[c] = k_ref[c * B:(c + 1) * B, :].astype(BF16)
        vT[c] = v_ref[c * B:(c + 1) * B, :].T.astype(BF16)
        q = q_ref[c * B:(c + 1) * B, :]
        zero = jnp.zeros_like(q)
        q2_s[c * W:c * W + B, :] = jnp.where(_head_mask(q.shape, 0), q, zero)
        q2_s[c * W + B:(c + 1) * W, :] = jnp.where(_head_mask(q.shape, 1), q, zero)

    second = lax.broadcasted_iota(I32, (1, W), 1) >= B
    slope = jnp.where(second, slopes_ref[2 * p + 1], slopes_ref[2 * p])
    key_i = lax.broadcasted_iota(I32, (B, W), 0)
    qry_i = lax.broadcasted_iota(I32, (B, W), 1) & (B - 1)
    base = slope * (key_i - qry_i).astype(F32)
    causal = key_i <= qry_i

    blk = lax.broadcasted_iota(I32, (nblk, nblk * W), 0)
    tile_of_lane = lax.broadcasted_iota(I32, (nblk, nblk * W), 1) >> (W.bit_length() - 1)
    g = _dot_nt(km_ref[...], q2_s[...].astype(F32), precision=HI)
    g = jnp.where(blk < tile_of_lane, g, -jnp.inf)
    sel_bias = jnp.full((nblk, nblk * W), NEG, F32)
    for _ in range(MOBA_TOPK):
        m = jnp.max(g, axis=0, keepdims=True)
        cand = (g == m) & (g > -jnp.inf)
        idx = jnp.min(jnp.where(cand, blk, nblk), axis=0, keepdims=True)
        pick = blk == idx
        sel_bias = jnp.where(pick, 0.0, sel_bias)
        g = jnp.where(pick, -jnp.inf, g)

    for j in reversed(range(nblk)):
        s_all = _dot_nt(kbf[j], q2_s[j * W:, :])
        probs, scales = [], []
        for i in range(j, nblk):
            here = slice((i - j) * W, (i - j + 1) * W)
            lanes = slice(i * W, (i + 1) * W)
            if i == j:
                s = jnp.where(causal, s_all[:, here] + base, NEG)
                m_new = jnp.max(s, axis=0, keepdims=True)
                pj = jnp.exp(s - m_new)
                l_s[:, lanes] = jnp.sum(pj, axis=0, keepdims=True)
                scales.append(None)
            else:
                s = s_all[:, here] + base + (sel_bias[j:j + 1, lanes] - slope * float(B * (i - j)))
                m_old = m_s[:, lanes]
                m_new = jnp.maximum(m_old, jnp.max(s, axis=0, keepdims=True))
                a = jnp.exp(m_old - m_new)
                pj = jnp.exp(s - m_new)
                l_s[:, lanes] = a * l_s[:, lanes] + jnp.sum(pj, axis=0, keepdims=True)
                scales.append(a)
            m_s[:, lanes] = m_new
            probs.append(pj.astype(BF16))
        pv = _dot(vT[j], jnp.concatenate(probs, axis=1))
        for i in range(j, nblk):
            here = slice((i - j) * W, (i - j + 1) * W)
            lanes = slice(i * W, (i + 1) * W)
            a = scales[i - j]
            acc_s[:, lanes] = pv[:, here] if a is None else a * acc_s[:, lanes] + pv[:, here]

    sub = lax.broadcasted_iota(I32, (2 * HEAD_DIM, B), 0)
    for i in range(nblk):
        o = acc_s[:, i * W:(i + 1) * W] / l_s[:, i * W:(i + 1) * W]
        o_ref[i * B:(i + 1) * B, :] = jnp.where(sub < HEAD_DIM, o[:, :B], o[:, B:]).T


def _moba_prompt(q, k, v, kmean, slopes, *, batch, seq):
    n, att_w = q.shape
    nblk = seq // MOBA_BLOCK
    B = MOBA_BLOCK
    kern = functools.partial(_moba_kernel, nblk=nblk)
    return pl.pallas_call(
        kern,
        grid_spec=pltpu.PrefetchScalarGridSpec(
            num_scalar_prefetch=1,
            grid=(batch, att_w // LANE),
            in_specs=[pl.BlockSpec((seq, LANE), lambda b, p, s: (b, p)),
                      pl.BlockSpec((seq, LANE), lambda b, p, s: (b, p)),
                      pl.BlockSpec((seq, LANE), lambda b, p, s: (b, p)),
                      pl.BlockSpec((nblk, LANE), lambda b, p, s: (b, p))],
            out_specs=pl.BlockSpec((seq, LANE), lambda b, p, s: (b, p)),
            scratch_shapes=[pltpu.VMEM((nblk, B, LANE), BF16), pltpu.VMEM((nblk, LANE, B), BF16),
                            pltpu.VMEM((nblk * 2 * B, LANE), BF16), pltpu.VMEM((1, nblk * 2 * B), F32),
                            pltpu.VMEM((1, nblk * 2 * B), F32), pltpu.VMEM((LANE, nblk * 2 * B), F32)]),
        out_shape=jax.ShapeDtypeStruct((n, att_w), F32),
        compiler_params=_params("arbitrary", "arbitrary"),
        name="moba_prompt",
    )(slopes, q, k, v, kmean)


def _memkv_kernel(x_ref, w_ref, mk_ref, mv_ref, *, mem_w):
    r = _dot(x_ref[...].astype(BF16), w_ref[...])
    mk_ref[...] = r[:, :mem_w]
    mv_ref[...] = r[:, mem_w:]


def _memkv(mem2d, w_bf, *, tile, mem_w):
    n, d = mem2d.shape
    return pl.pallas_call(
        functools.partial(_memkv_kernel, mem_w=mem_w),
        grid=(n // tile,),
        in_specs=[pl.BlockSpec((tile, d), lambda i: (i, 0)), pl.BlockSpec(w_bf.shape, lambda i: (0, 0))],
        out_specs=[pl.BlockSpec((tile, mem_w), lambda i: (i, 0))] * 2,
        out_shape=[jax.ShapeDtypeStruct((n, mem_w), F32)] * 2,
        compiler_params=_params("arbitrary"),
        name="memkv",
    )(mem2d, w_bf)


def _memattn_kernel(qm_ref, mk_ref, mv_ref, o_ref, *, tq, mem_w):
    sub = lax.broadcasted_iota(I32, (LANE, tq), 0)
    for pr in range(mem_w // LANE):
        cs = slice(LANE * pr, LANE * (pr + 1))
        qp = qm_ref[:, cs]
        mkp = mk_ref[:, cs].astype(BF16)
        mvT = mv_ref[:, cs].T.astype(BF16)
        outs = []
        for hh in range(2):
            qh = jnp.where(_head_mask(qp.shape, hh), qp, jnp.zeros_like(qp))
            s = _dot_nt(mkp, qh)
            m = jnp.max(s, axis=0, keepdims=True)
            e = jnp.exp(s - m)
            l = jnp.sum(e, axis=0, keepdims=True)
            outs.append(_dot(mvT, e.astype(BF16)) / l)
        o_ref[:, cs] = jnp.where(sub < HEAD_DIM, outs[0], outs[1]).T


def _memattn_prompt(qm, mk, mv, *, batch, seq, n_mem, tq):
    n, mem_w = qm.shape
    nq = seq // tq
    return pl.pallas_call(
        functools.partial(_memattn_kernel, tq=tq, mem_w=mem_w),
        grid=(batch, nq),
        in_specs=[pl.BlockSpec((tq, mem_w), lambda b, i: (b * nq + i, 0)),
                  pl.BlockSpec((n_mem, mem_w), lambda b, i: (b, 0)),
                  pl.BlockSpec((n_mem, mem_w), lambda b, i: (b, 0))],
        out_specs=pl.BlockSpec((tq, mem_w), lambda b, i: (b * nq + i, 0)),
        out_shape=jax.ShapeDtypeStruct((n, mem_w), F32),
        compiler_params=_params("arbitrary", "arbitrary"),
        name="memattn_prompt",
    )(qm, mk, mv)


SORT_TILE = 256


def _sort_rows(ts, n_exp):
    return TOP_K * ts + SUBLANE * n_exp


def _finish1_kernel(x_ref, oa_ref, oc_ref, om_ref, gmix_ref, wout_ref, g1_ref, b1_ref, wrT_ref, br_ref,
                    x1_ref, xs_ref, route_ref, gw_ref, cpad_ref, loff_ref, cbase_ref, tot_ref, carry,
                    *, tile, ts, n_exp, alpha):
    i = pl.program_id(0)
    lt = _sort_rows(ts, n_exp)

    @pl.when(i == 0)
    def _():
        carry[...] = jnp.zeros_like(carry)

    mix = jnp.concatenate([_rms(oa_ref[...]), _rms(oc_ref[...]), _rms(om_ref[...])], axis=-1) * gmix_ref[...]
    if wout_ref.dtype == BF16:
        z = alpha * x_ref[...] + _dot(mix.astype(BF16), wout_ref[...])
    else:
        z = alpha * x_ref[...] + _dot(mix, wout_ref[...], precision=HI)
    x1 = _layernorm(z, g1_ref[...], b1_ref[...])
    x1_ref[...] = x1
    x1b = x1.astype(BF16)

    g = _dot_nt(wrT_ref[...], x1, precision=HI) + br_ref[...]
    eidx = lax.broadcasted_iota(I32, (n_exp, tile), 0)
    picks, vals = [], []
    for k in range(TOP_K):
        m = jnp.max(g, axis=0, keepdims=True)
        idx = jnp.min(jnp.where(g == m, eidx, n_exp), axis=0, keepdims=True)
        pick = eidx == idx
        route_ref[k:k + 1, :] = idx
        picks.append(pick)
        vals.append(m)
        g = jnp.where(pick, -jnp.inf, g)
    ex = [jnp.exp(v - vals[0]) for v in vals]
    denom = ex[0] + ex[1] + ex[2] + ex[3]
    for k in range(TOP_K):
        gw_ref[k:k + 1, :] = ex[k] / denom
    gw_ref[TOP_K:, :] = jnp.zeros((gw_ref.shape[0] - TOP_K, tile), F32)

    t_src = lax.broadcasted_iota(I32, (ts, ts), 0)
    t_dst = lax.broadcasted_iota(I32, (ts, ts), 1)
    before = jnp.where(t_src < t_dst, 1.0, 0.0).astype(BF16)
    e_src = lax.broadcasted_iota(I32, (n_exp, n_exp), 1)
    e_dst = lax.broadcasted_iota(I32, (n_exp, n_exp), 0)
    lower = jnp.where(e_src < e_dst, 1.0, 0.0).astype(BF16)
    slot = lax.broadcasted_iota(I32, (lt, ts), 0)
    for sub in range(tile // ts):
        cs = slice(sub * ts, (sub + 1) * ts)
        pk = [p[:, cs] for p in picks]
        onehot = jnp.zeros((n_exp, ts), F32)
        for p in pk:
            onehot = onehot + jnp.where(p, 1.0, 0.0)
        cnt = jnp.sum(onehot, axis=1, keepdims=True)
        cpad = jnp.floor((cnt + (SUBLANE - 1)) * (1.0 / SUBLANE)) * SUBLANE
        cpad_l = jnp.broadcast_to(cpad, (n_exp, LANE))
        loff_l = _dot(lower, cpad_l.astype(BF16))
        pos = _dot(onehot.astype(BF16), before) + loff_l[:, 0:1]
        perm = jnp.zeros((lt, ts), F32)
        for k in range(TOP_K):
            lpos = jnp.sum(jnp.where(pk[k], pos, 0.0), axis=0, keepdims=True).astype(I32)
            route_ref[TOP_K + k:TOP_K + k + 1, cs] = lpos
            perm = perm + jnp.where(slot == lpos, 1.0, 0.0)
        perm = perm.astype(BF16)
        xs_ref[sub * lt:(sub + 1) * lt, :] = _dot(perm, x1b[cs, :])
        cpad_ref[sub] = cpad_l
        loff_ref[sub] = loff_l
        cbase_ref[sub] = jnp.broadcast_to(carry[...], (n_exp, LANE))
        carry[...] = carry[...] + cpad
    tot_ref[...] = jnp.broadcast_to(carry[...], tot_ref.shape)


def _finish1(x2d, oa, oc, om, g_mix, wout_bf, ln_g, ln_b, w_rT, b_r, *, tile, ts, alpha):
    n, d = x2d.shape
    n_exp = w_rT.shape[0]
    lt = _sort_rows(ts, n_exp)
    spt = tile // ts
    kern = functools.partial(_finish1_kernel, tile=tile, ts=ts, n_exp=n_exp, alpha=alpha)
    row = lambda w: pl.BlockSpec((tile, w), lambda i: (i, 0))
    full = lambda a: pl.BlockSpec(a.shape, lambda i: (0,) * a.ndim)
    meta = pl.BlockSpec((spt, n_exp, LANE), lambda i: (i, 0, 0))
    meta_shape = jax.ShapeDtypeStruct((n // ts, n_exp, LANE), F32)
    return pl.pallas_call(
        kern,
        grid=(n // tile,),
        in_specs=[row(d), row(oa.shape[1]), row(oc.shape[1]), row(om.shape[1]), full(g_mix), full(wout_bf),
                  full(ln_g), full(ln_b), full(w_rT), full(b_r)],
        out_specs=[row(d), pl.BlockSpec((spt * lt, d), lambda i: (i, 0)),
                   pl.BlockSpec((2 * TOP_K, tile), lambda i: (0, i)), pl.BlockSpec((2 * TOP_K, tile), lambda i: (0, i)),
                   meta, meta, meta, pl.BlockSpec((n_exp, LANE), lambda i: (0, 0))],
        out_shape=[jax.ShapeDtypeStruct((n, d), F32), jax.ShapeDtypeStruct((n // ts * lt, d), F32),
                   jax.ShapeDtypeStruct((2 * TOP_K, n), I32), jax.ShapeDtypeStruct((2 * TOP_K, n), F32),
                   meta_shape, meta_shape, meta_shape, jax.ShapeDtypeStruct((n_exp, LANE), F32)],
        scratch_shapes=[pltpu.VMEM((n_exp, 1), F32)],
        compiler_params=_params("arbitrary"),
        name="finish1",
    )(x2d, oa, oc, om, g_mix, wout_bf, ln_g, ln_b, w_rT, b_r)


def _run_sizes(ts):
    sizes, s = [], SUBLANE
    while s <= max(ts, SUBLANE):
        sizes.append(s)
        s *= 2
    return sizes


def _for_each_piece(length, sizes, fn):
    off = 0
    for sz in sizes:
        @pl.when((length & sz) != 0)
        def _(off=off, sz=sz):
            fn(off, sz)
        off = off + (length & sz)


def _dispatch_kernel(cpad_ref, loff_ref, base_ref, fill_ref, xsa_ref, xsb_ref, zero_ref, buf_ref, sem, fill_sem,
                     *, n_a, n_exp, sizes, tm):
    t = pl.program_id(0)

    def copy_runs(xs_ref, wait):
        def run(e, c):
            n = cpad_ref[t * n_exp + e]
            src = loff_ref[t * n_exp + e]
            dst = base_ref[t * n_exp + e]

            def piece(off, sz):
                cp = pltpu.make_async_copy(xs_ref.at[pl.ds(pl.multiple_of(src + off, SUBLANE), sz), :],
                                           buf_ref.at[pl.ds(pl.multiple_of(dst + off, SUBLANE), sz), :], sem)
                cp.wait() if wait else cp.start()

            _for_each_piece(n, sizes, piece)
            return c

        lax.fori_loop(0, n_exp, run, 0)

    def all_runs(wait):
        @pl.when(t < n_a)
        def _():
            copy_runs(xsa_ref, wait)

        @pl.when(t >= n_a)
        def _():
            copy_runs(xsb_ref, wait)

    all_runs(False)

    @pl.when(t == pl.num_programs(0) - 1)
    def _():
        start = fill_ref[0]
        rest = buf_ref.shape[0] - start
        n_full = rest // tm

        def chunk(c, wait):
            cp = pltpu.make_async_copy(zero_ref, buf_ref.at[pl.ds(pl.multiple_of(start + c * tm, SUBLANE), tm), :],
                                       fill_sem)
            cp.wait() if wait else cp.start()

        def piece(wait):
            def fn(off, sz):
                cp = pltpu.make_async_copy(
                    zero_ref.at[pl.ds(0, sz), :],
                    buf_ref.at[pl.ds(pl.multiple_of(start + n_full * tm + off, SUBLANE), sz), :], fill_sem)
                cp.wait() if wait else cp.start()
            return fn

        tail_sizes = [s for s in _run_sizes(tm) if s < tm]
        for wait in (False, True):
            lax.fori_loop(0, n_full, lambda c, carry, wait=wait: (chunk(c, wait), carry)[1], 0)
            _for_each_piece(rest - n_full * tm, tail_sizes, piece(wait))

    all_runs(True)


def _dispatch(cpad, loff, base, fill, xs_a, xs_b, *, ts_a, ts_b, n_exp, m_pad, tm):
    w = xs_a.shape[1]
    lt_a, lt_b = _sort_rows(ts_a, n_exp), _sort_rows(ts_b, n_exp)
    n_a, n_b = xs_a.shape[0] // lt_a, xs_b.shape[0] // lt_b
    zero = jnp.zeros((tm, w), F32)
    return pl.pallas_call(
        functools.partial(_dispatch_kernel, n_a=n_a, n_exp=n_exp, sizes=_run_sizes(max(ts_a, ts_b)), tm=tm),
        grid_spec=pltpu.PrefetchScalarGridSpec(
            num_scalar_prefetch=4,
            grid=(n_a + n_b,),
            in_specs=[pl.BlockSpec((lt_a, w), lambda t, *_: (jnp.minimum(t, n_a - 1), 0)),
                      pl.BlockSpec((lt_b, w), lambda t, *_: (jnp.maximum(t - n_a, 0), 0)),
                      pl.BlockSpec((tm, w), lambda t, *_: (0, 0))],
            out_specs=pl.BlockSpec(memory_space=pl.ANY),
            scratch_shapes=[pltpu.SemaphoreType.DMA(()), pltpu.SemaphoreType.DMA(())]),
        out_shape=jax.ShapeDtypeStruct((m_pad, w), F32),
        compiler_params=_params("arbitrary"),
        name="dispatch",
    )(cpad, loff, base, fill, xs_a, xs_b, zero)


def _ffn_kernel(otile_ref, tile_ref, exp_ref, lo_ref, hi_ref, lhs_ref, wgu_ref, bgu_ref, wd_ref, bd_ref, out_ref,
                wgu_bf, wd_bf, *, tm, d_ff):
    g = pl.program_id(0)
    lo = lo_ref[g]
    hi = hi_ref[g]

    @pl.when(hi == lo)
    def _():
        out_ref[...] = jnp.zeros_like(out_ref)

    new_expert = jnp.logical_or(g == 0, exp_ref[g] != exp_ref[jnp.maximum(g - 1, 0)])

    @pl.when(jnp.logical_and(new_expert, hi > lo))
    def _():
        wgu_bf[...] = wgu_ref[0].astype(BF16)
        wd_bf[...] = wd_ref[0].astype(BF16)

    @pl.when(hi > lo)
    def _():
        gu = _dot(lhs_ref[...].astype(BF16), wgu_bf[...]) + bgu_ref[0]
        gate = jnp.minimum(gu[:, :d_ff], SWIGLU_LIMIT)
        up = jnp.clip(gu[:, d_ff:], -SWIGLU_LIMIT, SWIGLU_LIMIT)
        hid = (up + 1.0) * gate * jax.nn.sigmoid(SWIGLU_ALPHA * gate)
        o = _dot(hid.astype(BF16), wd_bf[...]) + bd_ref[0]

        @pl.when(lo == 0)
        def _():
            out_ref[...] = o

        @pl.when(lo > 0)
        def _():
            rows = lax.broadcasted_iota(I32, (tm, 1), 0)
            out_ref[...] = jnp.where((rows >= lo) & (rows < hi), o, out_ref[...])


def _ffn(sched, buf, w_gu, b_gu, w_d, b_d, *, tm):
    m, w = buf.shape
    n_exp, d, d_ff2 = w_gu.shape
    out_tiles, tiles, experts, los, his = sched
    return pl.pallas_call(
        functools.partial(_ffn_kernel, tm=tm, d_ff=d_ff2 // 2),
        grid_spec=pltpu.PrefetchScalarGridSpec(
            num_scalar_prefetch=5,
            grid=(tiles.shape[0],),
            in_specs=[pl.BlockSpec((tm, w), lambda g, ot, t, e, lo, hi: (t[g], 0)),
                      pl.BlockSpec((1, d, d_ff2), lambda g, ot, t, e, lo, hi: (e[g], 0, 0)),
                      pl.BlockSpec((1, 1, d_ff2), lambda g, ot, t, e, lo, hi: (e[g], 0, 0)),
                      pl.BlockSpec((1, d_ff2 // 2, d), lambda g, ot, t, e, lo, hi: (e[g], 0, 0)),
                      pl.BlockSpec((1, 1, d), lambda g, ot, t, e, lo, hi: (e[g], 0, 0))],
            out_specs=pl.BlockSpec((tm, d), lambda g, ot, t, e, lo, hi: (ot[g], 0)),
            scratch_shapes=[pltpu.VMEM((d, d_ff2), BF16), pltpu.VMEM((d_ff2 // 2, d), BF16)]),
        out_shape=jax.ShapeDtypeStruct((m, d), F32),
        compiler_params=_params("arbitrary"),
        name="expert_ffn",
    )(out_tiles, tiles, experts, los, his, buf, w_gu, b_gu, w_d, b_d)


def _moe_schedule(counts, n_rows, tm):
    n_exp = counts.shape[0]
    n_tiles = n_rows // tm
    n_items = n_tiles + n_exp - 1
    ends = jnp.cumsum(counts)
    starts = ends - counts
    first_tile = starts // tm
    n_it = jnp.where(counts > 0, (ends - 1) // tm - first_tile + 1, 0)
    it_end = jnp.cumsum(n_it)
    it_start = it_end - n_it
    item = jnp.arange(n_items, dtype=I32)
    n_real = it_end[-1]
    g = jnp.minimum(item, n_real - 1)
    e = jnp.minimum(jnp.sum((it_end[None, :] <= g[:, None]).astype(I32), axis=1), n_exp - 1)
    of_e = e[:, None] == jnp.arange(n_exp, dtype=I32)[None, :]
    at_e = lambda table: jnp.sum(jnp.where(of_e, table[None, :], 0), axis=1)
    tile = (at_e(first_tile) + g - at_e(it_start)).astype(I32)
    valid = item < n_real
    lo = jnp.where(valid, jnp.clip(at_e(starts) - tile * tm, 0, tm), 0).astype(I32)
    hi = jnp.where(valid, jnp.clip(at_e(ends) - tile * tm, 0, tm), 0).astype(I32)
    used = (ends[-1] + tm - 1) // tm
    out_tile = jnp.where(valid, tile, jnp.minimum(used + item - n_real, n_tiles - 1)).astype(I32)
    return (out_tile, tile, e, lo, hi), starts.astype(I32)


def _combine_kernel(cpad_ref, loff_ref, base_ref, route_ref, gw_ref, x1_ref, g2_ref, b2_ref, eo_ref, y_ref,
                    gbuf, sem, *, ts, n_exp, sizes, alpha):
    t = pl.program_id(0)
    nt = pl.num_programs(0)
    lt = gbuf.shape[1]
    d = gbuf.shape[2]

    def tile_copies(tt, slot, wait):
        def run(e, c):
            n = cpad_ref[tt * n_exp + e]
            src = base_ref[tt * n_exp + e]
            dst = loff_ref[tt * n_exp + e]

            def piece(off, sz):
                cp = pltpu.make_async_copy(eo_ref.at[pl.ds(pl.multiple_of(src + off, SUBLANE), sz), :],
                                           gbuf.at[slot, pl.ds(pl.multiple_of(dst + off, SUBLANE), sz), :],
                                           sem.at[slot])
                cp.wait() if wait else cp.start()

            _for_each_piece(n, sizes, piece)
            return c

        lax.fori_loop(0, n_exp, run, 0, unroll=4)

    def fetch(tt, slot):
        gbuf[slot, TOP_K * ts:, :] = jnp.zeros((lt - TOP_K * ts, d), F32)
        tile_copies(tt, slot, False)

    slot = t % 2

    @pl.when(t == 0)
    def _():
        fetch(t, slot)

    @pl.when(t + 1 < nt)
    def _():
        fetch(t + 1, 1 - slot)

    tile_copies(t, slot, True)

    r = lax.broadcasted_iota(I32, (ts, ts), 0)
    c = lax.broadcasted_iota(I32, (ts, ts), 1)
    eye = jnp.where(r == c, 1.0, 0.0)
    wcol = _dot_nt(eye, gw_ref[...], precision=HI)
    pcol = _dot_nt(eye, route_ref[...].astype(F32), precision=HI)
    rows = gbuf[slot].astype(BF16)
    slot_i = lax.broadcasted_iota(I32, (ts, lt), 1)
    unsort = jnp.zeros((ts, lt), F32)
    for k in range(TOP_K):
        lpos = pcol[:, TOP_K + k:TOP_K + k + 1].astype(I32)
        unsort = unsort + jnp.where(slot_i == lpos, wcol[:, k:k + 1], 0.0)
    hi = unsort.astype(BF16)
    lo = (unsort - hi.astype(F32)).astype(BF16)
    moe = _dot(hi, rows) + _dot(lo, rows)
    y_ref[...] = _layernorm(alpha * x1_ref[...] + moe, g2_ref[...], b2_ref[...])


def _combine(cpad, loff, base, route, gw, x1, ln_g, ln_b, eo, *, ts, n_exp, alpha):
    n, d = x1.shape
    lt = _sort_rows(ts, n_exp)
    full = lambda a: pl.BlockSpec(a.shape, lambda i, *_: (0,) * a.ndim)
    return pl.pallas_call(
        functools.partial(_combine_kernel, ts=ts, n_exp=n_exp, sizes=_run_sizes(ts), alpha=alpha),
        grid_spec=pltpu.PrefetchScalarGridSpec(
            num_scalar_prefetch=3,
            grid=(n // ts,),
            in_specs=[pl.BlockSpec((2 * TOP_K, ts), lambda i, *_: (0, i)),
                      pl.BlockSpec((2 * TOP_K, ts), lambda i, *_: (0, i)),
                      pl.BlockSpec((ts, d), lambda i, *_: (i, 0)), full(ln_g), full(ln_b),
                      pl.BlockSpec(memory_space=pl.ANY)],
            out_specs=pl.BlockSpec((ts, d), lambda i, *_: (i, 0)),
            scratch_shapes=[pltpu.VMEM((2, lt, d), F32), pltpu.SemaphoreType.DMA((2,))]),
        out_shape=jax.ShapeDtypeStruct((n, d), F32),
        compiler_params=_params("arbitrary"),
        name="combine",
    )(cpad, loff, base, route, gw, x1, ln_g, ln_b, eo)


def _finish(groups, wts, *, tile, tm, alpha):
    n_exp = wts["w_rT"].shape[0]
    flat = lambda a: a[:, :, 0].astype(I32).reshape(-1)
    routed = []
    for x2d, oa, oc, om, w_out in groups:
        n = x2d.shape[0]
        ts = _pick_tile(n, SORT_TILE)
        x1, xs, route, gw, cpad, loff, cbase, tot = _finish1(
            x2d, oa, oc, om, wts["g_mix"], w_out, wts["ln1_g"], wts["ln1_b"], wts["w_rT"], wts["b_r"],
            tile=_pick_tile(n, tile), ts=ts, alpha=alpha)
        routed.append(dict(ts=ts, x1=x1, xs=xs, route=route, gw=gw, cpad=flat(cpad), loff=flat(loff),
                           cbase=cbase[:, :, 0].astype(I32), total=tot[:, 0].astype(I32)))
    a, b = routed
    counts = a["total"] + b["total"]
    m_pad = (-(-(a["xs"].shape[0] + b["xs"].shape[0]) // tm) + 1) * tm
    sched, offs = _moe_schedule(counts, m_pad, tm)
    a["base"] = (a["cbase"] + offs[None, :]).reshape(-1)
    b["base"] = (b["cbase"] + (offs + a["total"])[None, :]).reshape(-1)
    both = lambda key: jnp.concatenate([a[key], b[key]])
    fill = jnp.sum(counts, keepdims=True)
    buf = _dispatch(both("cpad"), both("loff"), both("base"), fill, a["xs"], b["xs"], ts_a=a["ts"], ts_b=b["ts"],
                    n_exp=n_exp, m_pad=m_pad, tm=tm)
    eo = _ffn(sched, buf, wts["w_gu"], wts["b_gu"], wts["w_d"], wts["b_d"], tm=tm)
    return [_combine(r["cpad"], r["loff"], r["base"], r["route"], r["gw"], r["x1"], wts["ln2_g"], wts["ln2_b"], eo,
                     ts=r["ts"], n_exp=n_exp, alpha=alpha) for r in routed]


def _sample_inproj_kernel(x_ref, w_ref, wc_ref, p0_ref, p1_ref, q_ref, k_ref, v_ref, qm_ref, oc_ref, u_ref,
                          *, att_w, mem_w, conv_w):
    x = x_ref[...]
    c1, c2, c3 = att_w, 2 * att_w, 3 * att_w
    c4 = c3 + mem_w
    c5, c6, c7 = c4 + conv_w, c4 + 2 * conv_w, c4 + 3 * conv_w

    def proj(lo, hi):
        return _dot(x, w_ref[:, lo:hi], precision=HI)

    q_ref[...] = proj(0, c1) * Q_SCALE
    k_ref[...] = proj(c1, c2)
    v_ref[...] = proj(c2, c3)
    qm_ref[...] = proj(c3, c4) * Q_SCALE
    u = proj(c5, c6) * proj(c6, c7)
    cz = wc_ref[0:1, :] * p0_ref[...] + wc_ref[1:2, :] * p1_ref[...] + wc_ref[2:3, :] * u
    oc_ref[...] = proj(c4, c5) * cz
    u_ref[...] = u


def _sample_inproj(x2d, w_bf, w_conv, prev0, prev1, *, att_w, mem_w, conv_w):
    n = x2d.shape[0]
    args = (x2d, w_bf, w_conv, prev0, prev1)
    full = lambda a: pl.BlockSpec(a.shape, lambda i: (0,) * a.ndim)
    widths = (att_w, att_w, att_w, mem_w, conv_w, conv_w)
    return pl.pallas_call(
        functools.partial(_sample_inproj_kernel, att_w=att_w, mem_w=mem_w, conv_w=conv_w),
        grid=(1,),
        in_specs=[full(a) for a in args],
        out_specs=[pl.BlockSpec((n, w), lambda i: (0, 0)) for w in widths],
        out_shape=[jax.ShapeDtypeStruct((n, w), F32) for w in widths],
        compiler_params=_params("arbitrary"),
        name="sample_inproj",
    )(*args)


def _kscan_kernel(pt_ref, q_ref, ck_ref, sel_ref, kbuf, ksum, sem, *, n_pages, chunk, n_heads):
    b = pl.program_id(0)
    n_chunks = n_pages // chunk
    pages_per_blk = MOBA_BLOCK // PAGE_SIZE
    nblk = ksum.shape[0]
    q_col = q_ref[0]

    def copies(bb, c, slot):
        return [pltpu.make_async_copy(ck_ref.at[0, pt_ref[bb * n_pages + c * chunk + j]], kbuf.at[slot, j],
                                      sem.at[slot]) for j in range(chunk)]

    @pl.when(b == 0)
    def _():
        for cp in copies(b, 0, 0):
            cp.start()

    def body(c, carry):
        slot = c % 2
        last = c + 1 == n_chunks

        @pl.when(jnp.logical_or(jnp.logical_not(last), b + 1 < pl.num_programs(0)))
        def _():
            for cp in copies(jnp.where(last, b + 1, b), jnp.where(last, 0, c + 1), 1 - slot):
                cp.start()

        for cp in copies(b, c, slot):
            cp.wait()
        for jb in range(chunk // pages_per_blk):
            s = kbuf[slot, pages_per_blk * jb]
            for pg in range(1, pages_per_blk):
                s = s + kbuf[slot, pages_per_blk * jb + pg]
            ksum[c * (chunk // pages_per_blk) + jb] = jnp.sum(s * q_col, axis=1)
        return carry

    lax.fori_loop(0, n_chunks, body, 0)

    g = jnp.sum(ksum[...], axis=2, keepdims=True) * (1.0 / MOBA_BLOCK)
    blk = lax.broadcasted_iota(I32, g.shape, 0)
    sel_ref[...] = jnp.zeros(sel_ref.shape, I32)
    for r in range(MOBA_TOPK):
        m = jnp.max(g, axis=0, keepdims=True)
        idx = jnp.min(jnp.where((g == m) & (g > -jnp.inf), blk, nblk), axis=0, keepdims=True)
        sel_ref[0, :, r:r + 1] = idx[0]
        g = jnp.where(blk == idx, -jnp.inf, g)


def _kscan(pt_flat, q_col, ckT, *, n_pages, chunk):
    bs, n_heads, hd, _ = q_col.shape
    nblk = n_pages * PAGE_SIZE // MOBA_BLOCK
    return pl.pallas_call(
        functools.partial(_kscan_kernel, n_pages=n_pages, chunk=chunk, n_heads=n_heads),
        grid_spec=pltpu.PrefetchScalarGridSpec(
            num_scalar_prefetch=1,
            grid=(bs,),
            in_specs=[pl.BlockSpec((1, n_heads, hd, 1), lambda b, pt: (b, 0, 0, 0)), pl.BlockSpec(memory_space=pl.ANY)],
            out_specs=pl.BlockSpec((1, n_heads, LANE), lambda b, pt: (b, 0, 0)),
            scratch_shapes=[pltpu.VMEM((2, chunk, n_heads, hd, PAGE_SIZE), F32),
                            pltpu.VMEM((nblk, n_heads, PAGE_SIZE), F32), pltpu.SemaphoreType.DMA((2,))]),
        out_shape=jax.ShapeDtypeStruct((bs, n_heads, LANE), I32),
        compiler_params=_params("arbitrary"),
        name="kscan",
    )(pt_flat, q_col, ckT)


def _sample_attn_kernel(pt_ref, sel_ref, slopes_ref, q_ref, kn_ref, vn_ref, qm_ref, mk_ref, mv_ref, ck_ref, cv_ref,
                        oa_ref, om_ref, kbuf, vbuf, sem, *, n_pages, n_heads, past_len, mem_w):
    b = pl.program_id(0)
    nb = pl.num_programs(0)
    B = MOBA_BLOCK
    pages_per_blk = B // PAGE_SIZE
    n_keys = MOBA_TOPK * B

    def blocks(bb, h):
        return [sel_ref[(bb * n_heads + h) * MOBA_TOPK + s] for s in range(MOBA_TOPK)]

    n_sel_pages = MOBA_TOPK * pages_per_blk

    def copies(bb, slot):
        cps = []
        for h in range(n_heads):
            blks = blocks(bb, h)
            for s in range(MOBA_TOPK):
                for half in range(pages_per_blk):
                    pg = pt_ref[bb * n_pages + pages_per_blk * blks[s] + half]
                    pp = s * pages_per_blk + half
                    cps.append(pltpu.make_async_copy(ck_ref.at[0, pg, h], kbuf.at[slot, h, pp], sem.at[0, slot]))
                    cps.append(pltpu.make_async_copy(cv_ref.at[0, pg, h], vbuf.at[slot, h, pp], sem.at[1, slot]))
        return cps

    slot = b % 2

    @pl.when(b == 0)
    def _():
        for cp in copies(b, slot):
            cp.start()

    @pl.when(b + 1 < nb)
    def _():
        for cp in copies(b + 1, 1 - slot):
            cp.start()

    for cp in copies(b, slot):
        cp.wait()

    key_lane = lax.broadcasted_iota(I32, (1, n_keys), 1)
    lane = lax.broadcasted_iota(I32, (1, LANE), 1)
    for h in range(n_heads):
        blks = blocks(b, h)
        qh = q_ref[0, h:h + 1, :]
        q8 = jnp.broadcast_to(qh, (SUBLANE, HEAD_DIM))
        blk_of_key = jnp.where(key_lane < B, blks[0], jnp.where(key_lane < 2 * B, blks[1], blks[2]))
        dist = (past_len - blk_of_key * B - (key_lane & (B - 1))).astype(F32)
        s = jnp.concatenate([_dot(q8.astype(BF16), kbuf[slot, h, pp].astype(BF16)) for pp in range(n_sel_pages)],
                            axis=1)
        s = s - slopes_ref[h] * dist
        s_self = jnp.sum(qh * kn_ref[0, h:h + 1, :], axis=1, keepdims=True)
        m = jnp.maximum(jnp.max(s, axis=1, keepdims=True), s_self)
        e = jnp.exp(s - m)
        e_self = jnp.exp(s_self - m)
        l = jnp.sum(e, axis=1, keepdims=True) + e_self
        o = e_self * vn_ref[0, h:h + 1, :]
        for pp in range(n_sel_pages):
            o = o + _dot_nt(e[:, pp * PAGE_SIZE:(pp + 1) * PAGE_SIZE].astype(BF16), vbuf[slot, h, pp].astype(BF16))
        oa_ref[0, h:h + 1, :] = (o / l)[0:1, :]

    for h in range(mem_w // HEAD_DIM):
        q8 = jnp.broadcast_to(qm_ref[0, h:h + 1, :], (SUBLANE, HEAD_DIM))
        s = _dot(q8, mk_ref[0, 0, h], precision=HI)
        e = jnp.exp(s - jnp.max(s, axis=1, keepdims=True))
        o = _dot_nt(e, mv_ref[0, 0, h], precision=HI) / jnp.sum(e, axis=1, keepdims=True)
        om_ref[0, h:h + 1, :] = o[0:1, :]


def _sample_attn(pt_flat, sel_flat, slopes, q3, kn3, vn3, qm3, mkT, mvT, ck, cv, *, n_pages):
    bs, n_heads, hd = q3.shape
    mem_heads = qm3.shape[1]
    mem_w = mem_heads * hd
    n_keys = MOBA_TOPK * MOBA_BLOCK
    per_b = lambda a: pl.BlockSpec((1,) + a.shape[1:], lambda b, pt, sel: (b,) + (0,) * (a.ndim - 1))
    mem_b = lambda a: pl.BlockSpec((1, 1) + a.shape[2:], lambda b, pt, sel: (0, b) + (0,) * (a.ndim - 2))
    return pl.pallas_call(
        functools.partial(_sample_attn_kernel, n_pages=n_pages, n_heads=n_heads, past_len=n_pages * PAGE_SIZE,
                          mem_w=mem_w),
        grid_spec=pltpu.PrefetchScalarGridSpec(
            num_scalar_prefetch=2,
            grid=(bs,),
            in_specs=[pl.BlockSpec(memory_space=pltpu.SMEM), per_b(q3), per_b(kn3), per_b(vn3), per_b(qm3),
                      mem_b(mkT), mem_b(mvT), pl.BlockSpec(memory_space=pl.ANY), pl.BlockSpec(memory_space=pl.ANY)],
            out_specs=[pl.BlockSpec((1, n_heads, hd), lambda b, pt, sel: (b, 0, 0)),
                       pl.BlockSpec((1, mem_heads, hd), lambda b, pt, sel: (b, 0, 0))],
            scratch_shapes=[pltpu.VMEM((2, n_heads, n_keys // PAGE_SIZE, hd, PAGE_SIZE), F32),
                            pltpu.VMEM((2, n_heads, n_keys // PAGE_SIZE, hd, PAGE_SIZE), F32),
                            pltpu.SemaphoreType.DMA((2, 2))]),
        out_shape=[jax.ShapeDtypeStruct((bs, n_heads, hd), F32), jax.ShapeDtypeStruct((bs, mem_heads, hd), F32)],
        compiler_params=_params("arbitrary"),
        name="sample_attn",
    )(pt_flat, sel_flat, slopes, q3, kn3, vn3, qm3, mkT, mvT, ck, cv)


def _sample_layer(x, cache_k, cache_v, mem_k, mem_v, state, page_table, wts, dims, alpha):
    bs, dec_seq, d = x.shape
    assert dec_seq == 1
    att_w, mem_w, conv_w = dims
    n_heads = att_w // HEAD_DIM
    n_pages = page_table.shape[1]
    assert (n_pages * PAGE_SIZE) % MOBA_BLOCK == 0 and n_pages * PAGE_SIZE // MOBA_BLOCK >= MOBA_TOPK
    x2d = x.reshape(bs, d)
    q, k, v, qm, oc, u = _sample_inproj(x2d, wts["w_in_f32"], wts["w_conv"], state[:, 0, :], state[:, 1, :],
                                        att_w=att_w, mem_w=mem_w, conv_w=conv_w)
    pt_flat = page_table.reshape(-1)
    heads = lambda a: a.reshape(bs, n_heads, HEAD_DIM)
    pagesT = lambda c: jnp.transpose(c, (0, 1, 3, 4, 2))
    chunk = 16 if n_pages % 32 == 0 else 8
    assert n_pages % (2 * chunk) == 0
    sel = _kscan(pt_flat, q.reshape(bs, n_heads, HEAD_DIM, 1), pagesT(cache_k), n_pages=n_pages, chunk=chunk)
    sel_flat = sel[:, :, :MOBA_TOPK].reshape(-1)
    oa, om = _sample_attn(pt_flat, sel_flat, wts["slopes"], heads(q), heads(k), heads(v),
                          qm.reshape(bs, mem_w // HEAD_DIM, HEAD_DIM), pagesT(mem_k), pagesT(mem_v), pagesT(cache_k),
                          pagesT(cache_v), n_pages=n_pages)
    conv_state = jnp.stack([state[:, 1, :], u], axis=1)
    return (x2d, oa.reshape(bs, att_w), oc, om.reshape(bs, mem_w), wts["w_out_f32"]), k, v, conv_state


def _pick_tile(n, pref):
    return pref if n % pref == 0 else n


def _prompt_layer(x, mem, wts, dims, alpha):
    batch, seq, d = x.shape
    att_w, mem_w, conv_w = dims
    n = batch * seq
    n_mem = mem.shape[1]
    x2d = x.reshape(n, d)
    tile = 512
    q, k, v, kT, vT, kmean, qm, oc, tail = _inproj(x2d, wts["w_in"], wts["w_conv"], seq=seq, tile=tile, att_w=att_w,
                                                   mem_w=mem_w, conv_w=conv_w)
    oa = _moba_prompt(q, k, v, kmean.reshape(n // MOBA_BLOCK, att_w), wts["slopes"], batch=batch, seq=seq)
    mk, mv = _memkv(mem.reshape(batch * n_mem, d), wts["w_mem_kv"], tile=_pick_tile(batch * n_mem, 512), mem_w=mem_w)
    om = _memattn_prompt(qm, mk, mv, batch=batch, seq=seq, n_mem=n_mem, tq=512)
    conv_state = tail.reshape(batch, seq // tile, SUBLANE, conv_w)[:, -1, SUBLANE - (CONV_K - 1):, :]
    rows = lambda t: jnp.transpose(t, (0, 3, 1, 2))
    return (x2d, oa, oc, om, wts["w_out"]), rows(kT), rows(vT), conv_state, mk, mv


def _prep_weights(l, w_in, w_mem_kv, w_conv, g_mix, w_out, ln1_g, ln1_b, w_router, b_router, w_gate_up, b_gate_up,
                  w_down, b_down, ln2_g, ln2_b):
    n_heads = 8
    row = lambda a: a[l][None, :]
    return {
        "w_in": w_in[l].astype(BF16), "w_in_f32": w_in[l], "w_mem_kv": w_mem_kv[l].astype(BF16), "w_conv": w_conv[l],
        "g_mix": row(g_mix), "w_out": w_out[l].astype(BF16), "w_out_f32": w_out[l], "ln1_g": row(ln1_g),
        "ln1_b": row(ln1_b),
        "w_rT": w_router[l].T, "b_r": b_router[l][:, None],
        "w_gu": w_gate_up[l], "b_gu": b_gate_up[l][:, None, :],
        "w_d": w_down[l], "b_d": b_down[l][:, None, :],
        "ln2_g": row(ln2_g), "ln2_b": row(ln2_b),
        "slopes": 2.0 ** (-8.0 * jnp.arange(1, n_heads + 1, dtype=F32) / n_heads),
    }


def kernel(x_prompt, x_sample, cache_k, cache_v, cache_mem_k, cache_mem_v, state_conv, page_table, mem_prompt, w_in, w_mem_kv, w_conv, g_mix, w_out, ln1_g, ln1_b, w_router, b_router, w_gate_up, b_gate_up, w_down, b_down, ln2_g, ln2_b):
    depth = w_in.shape[0]
    assert depth == 1
    alpha = (2 * depth) ** 0.25
    n_heads = cache_k.shape[3]
    att_w = n_heads * HEAD_DIM
    mem_w = cache_mem_k.shape[3] * HEAD_DIM
    conv_w = state_conv.shape[3]
    dims = (att_w, mem_w, conv_w)
    wts = _prep_weights(0, w_in, w_mem_kv, w_conv, g_mix, w_out, ln1_g, ln1_b, w_router, b_router, w_gate_up,
                        b_gate_up, w_down, b_down, ln2_g, ln2_b)
    bp, seq, d = x_prompt.shape
    n_mem = mem_prompt.shape[1]
    mixed_p, k_p, v_p, conv_p, mk_p, mv_p = _prompt_layer(x_prompt, mem_prompt, wts, dims, alpha)
    bs = x_sample.shape[0]
    mixed_s, k_s, v_s, conv_s = _sample_layer(x_sample, cache_k, cache_v, cache_mem_k, cache_mem_v,
                                              state_conv[0], page_table, wts, dims, alpha)
    y_p, y_s = _finish((mixed_p, mixed_s), wts, tile=512, tm=512, alpha=alpha)
    return (y_p.reshape(x_prompt.shape), y_s.reshape(x_sample.shape),
            k_p[None], v_p[None], conv_p[None], mk_p.reshape(1, bp, n_mem, mem_w // HEAD_DIM, HEAD_DIM),
            mv_p.reshape(1, bp, n_mem, mem_w // HEAD_DIM, HEAD_DIM),
            k_s.reshape(1, bs, 1, n_heads, HEAD_DIM), v_s.reshape(1, bs, 1, n_heads, HEAD_DIM), conv_s[None])
```

```python
import functools

import jax
import jax.numpy as jnp
from jax import lax
from jax.experimental import pallas as pl
from jax.experimental.pallas import tpu as pltpu

F32 = jnp.float32
BF16 = jnp.bfloat16
I32 = jnp.int32

HEAD_DIM = 64
MOBA_BLOCK = 256
MOBA_TOPK = 3
PAGE_SIZE = 128
TOP_K = 4
CONV_K = 3
SWIGLU_LIMIT = 7.0
SWIGLU_ALPHA = 1.702
LN_EPS = 1e-5
Q_SCALE = HEAD_DIM ** -0.5
NEG = -1e30
LANE = 128
SUBLANE = 8
VMEM_LIMIT = 56 * 1024 * 1024
HI = lax.Precision.HIGHEST


def _params(*sem):
    return pltpu.CompilerParams(dimension_semantics=sem, vmem_limit_bytes=VMEM_LIMIT)


def _dot_nt(a, b, precision=None):
    return lax.dot_general(a, b, (((1,), (1,)), ((), ())), precision=precision, preferred_element_type=F32)


def _dot(a, b, precision=None):
    return jnp.dot(a, b, precision=precision, preferred_element_type=F32)


def _layernorm(z, g, b):
    zc = z - jnp.mean(z, axis=-1, keepdims=True)
    var = jnp.mean(zc * zc, axis=-1, keepdims=True)
    return zc * lax.rsqrt(var + LN_EPS) * g + b


def _rms(a):
    return a * lax.rsqrt(jnp.mean(a * a, axis=-1, keepdims=True) + LN_EPS)


def _head_mask(shape, hh):
    lane = lax.broadcasted_iota(I32, shape, len(shape) - 1)
    return (lane >= HEAD_DIM * hh) & (lane < HEAD_DIM * (hh + 1))


def _inproj_kernel(x_ref, w_ref, wc_ref, q_ref, k_ref, v_ref, kt_ref, vt_ref, km_ref, qm_ref, oc_ref, tail_ref, ubuf,
                   *, tile, tiles_per_seq, att_w, mem_w, conv_w):
    i = pl.program_id(0)
    x = x_ref[...].astype(BF16)
    c1, c2, c3 = att_w, 2 * att_w, 3 * att_w
    c4 = c3 + mem_w
    c5, c6, c7 = c4 + conv_w, c4 + 2 * conv_w, c4 + 3 * conv_w
    n_heads = att_w // HEAD_DIM

    def proj(lo, hi):
        return _dot(x, w_ref[:, lo:hi])

    q_ref[...] = (proj(0, c1) * Q_SCALE).astype(BF16)
    k = proj(c1, c2)
    k_ref[...] = k
    kt_ref[0] = k.T.reshape(n_heads, HEAD_DIM, tile)
    km_ref[0] = jnp.sum(k.reshape(tile // MOBA_BLOCK, MOBA_BLOCK, att_w), axis=1) * (1.0 / MOBA_BLOCK)
    v = proj(c2, c3)
    v_ref[...] = v
    vt_ref[0] = v.T.reshape(n_heads, HEAD_DIM, tile)
    qm_ref[...] = (proj(c3, c4) * Q_SCALE).astype(BF16)
    gb = proj(c4, c5)
    u = proj(c5, c6) * proj(c6, c7)

    @pl.when(i % tiles_per_seq == 0)
    def _():
        ubuf[0:SUBLANE, :] = jnp.zeros((SUBLANE, conv_w), F32)

    ubuf[SUBLANE:SUBLANE + tile, :] = u
    u1 = ubuf[SUBLANE - 1:SUBLANE - 1 + tile, :]
    u2 = ubuf[SUBLANE - 2:SUBLANE - 2 + tile, :]
    cz = wc_ref[0:1, :] * u2 + wc_ref[1:2, :] * u1 + wc_ref[2:3, :] * u
    oc_ref[...] = gb * cz
    tail = ubuf[tile:tile + SUBLANE, :]
    tail_ref[0] = tail
    ubuf[0:SUBLANE, :] = tail


def _inproj(x2d, w_bf, w_conv, *, seq, tile, att_w, mem_w, conv_w):
    n, d = x2d.shape
    nt = n // tile
    kern = functools.partial(_inproj_kernel, tile=tile, tiles_per_seq=seq // tile, att_w=att_w, mem_w=mem_w,
                             conv_w=conv_w)
    row = lambda w: pl.BlockSpec((tile, w), lambda i: (i, 0))
    tps = seq // tile
    n_heads = att_w // HEAD_DIM
    headsT = pl.BlockSpec((1, n_heads, HEAD_DIM, tile), lambda i: (i // tps, 0, 0, i % tps))
    headsT_shape = jax.ShapeDtypeStruct((n // seq, n_heads, HEAD_DIM, seq), F32)
    return pl.pallas_call(
        kern,
        grid=(nt,),
        in_specs=[row(d), pl.BlockSpec(w_bf.shape, lambda i: (0, 0)), pl.BlockSpec(w_conv.shape, lambda i: (0, 0))],
        out_specs=[row(att_w), row(att_w), row(att_w), headsT, headsT,
                   pl.BlockSpec((1, tile // MOBA_BLOCK, att_w), lambda i: (i, 0, 0)),
                   row(mem_w), row(conv_w), pl.BlockSpec((1, SUBLANE, conv_w), lambda i: (i, 0, 0))],
        out_shape=[jax.ShapeDtypeStruct((n, att_w), BF16), jax.ShapeDtypeStruct((n, att_w), F32),
                   jax.ShapeDtypeStruct((n, att_w), F32), headsT_shape, headsT_shape,
                   jax.ShapeDtypeStruct((nt, tile // MOBA_BLOCK, att_w), F32),
                   jax.ShapeDtypeStruct((n, mem_w), BF16), jax.ShapeDtypeStruct((n, conv_w), F32),
                   jax.ShapeDtypeStruct((nt, SUBLANE, conv_w), F32)],
        scratch_shapes=[pltpu.VMEM((tile + SUBLANE, conv_w), F32)],
        compiler_params=_params("arbitrary"),
        name="inproj",
    )(x2d, w_bf, w_conv)


def _moba_kernel(slopes_ref, q_ref, k_ref, v_ref, km_ref, o_ref, kbf, vT, q2_s, m_s, l_s, acc_s, *, nblk):
    p = pl.program_id(1)
    B = MOBA_BLOCK
    W = 2 * B

    for c in range(nblk):
        kbf[c] = k_ref[c * B:(c + 1) * B, :].astype(BF16)
        vT[c] = v_ref[c * B:(c + 1) * B, :].T.astype(BF16)
        q = q_ref[c * B:(c + 1) * B, :]
        zero = jnp.zeros_like(q)
        q2_s[c * W:c * W + B, :] = jnp.where(_head_mask(q.shape, 0), q, zero)
        q2_s[c * W + B:(c + 1) * W, :] = jnp.where(_head_mask(q.shape, 1), q, zero)

    second = lax.broadcasted_iota(I32, (1, W), 1) >= B
    slope = jnp.where(second, slopes_ref[2 * p + 1], slopes_ref[2 * p])
    key_i = lax.broadcasted_iota(I32, (B, W), 0)
    qry_i = lax.broadcasted_iota(I32, (B, W), 1) & (B - 1)
    base = slope * (key_i - qry_i).astype(F32)
    causal = key_i <= qry_i

    blk = lax.broadcasted_iota(I32, (nblk, nblk * W), 0)
    tile_of_lane = lax.broadcasted_iota(I32, (nblk, nblk * W), 1) >> (W.bit_length() - 1)
    g = _dot_nt(km_ref[...], q2_s[...].astype(F32), precision=HI)
    g = jnp.where(blk < tile_of_lane, g, -jnp.inf)
    sel_bias = jnp.full((nblk, nblk * W), NEG, F32)
    for _ in range(MOBA_TOPK):
        m = jnp.max(g, axis=0, keepdims=True)
        cand = (g == m) & (g > -jnp.inf)
        idx = jnp.min(jnp.where(cand, blk, nblk), axis=0, keepdims=True)
        pick = blk == idx
        sel_bias = jnp.where(pick, 0.0, sel_bias)
        g = jnp.where(pick, -jnp.inf, g)

    for j in reversed(range(nblk)):
        s_all = _dot_nt(kbf[j], q2_s[j * W:, :])
        probs, scales = [], []
        for i in range(j, nblk):
            here = slice((i - j) * W, (i - j + 1) * W)
            lanes = slice(i * W, (i + 1) * W)
            if i == j:
                s = jnp.where(causal, s_all[:, here] + base, NEG)
                m_new = jnp.max(s, axis=0, keepdims=True)
                pj = jnp.exp(s - m_new)
                l_s[:, lanes] = jnp.sum(pj, axis=0, keepdims=True)
                scales.append(None)
            else:
                s = s_all[:, here] + base + (sel_bias[j:j + 1, lanes] - slope * float(B * (i - j)))
                m_old = m_s[:, lanes]
                m_new = jnp.maximum(m_old, jnp.max(s, axis=0, keepdims=True))
                a = jnp.exp(m_old - m_new)
                pj = jnp.exp(s - m_new)
                l_s[:, lanes] = a * l_s[:, lanes] + jnp.sum(pj, axis=0, keepdims=True)
                scales.append(a)
            m_s[:, lanes] = m_new
            probs.append(pj.astype(BF16))
        pv = _dot(vT[j], jnp.concatenate(probs, axis=1))
        for i in range(j, nblk):
            here = slice((i - j) * W, (i - j + 1) * W)
            lanes = slice(i * W, (i + 1) * W)
            a = scales[i - j]
            acc_s[:, lanes] = pv[:, here] if a is None else a * acc_s[:, lanes] + pv[:, here]

    sub = lax.broadcasted_iota(I32, (2 * HEAD_DIM, B), 0)
    for i in range(nblk):
        o = acc_s[:, i * W:(i + 1) * W] / l_s[:, i * W:(i + 1) * W]
        o_ref[i * B:(i + 1) * B, :] = jnp.where(sub < HEAD_DIM, o[:, :B], o[:, B:]).T


def _moba_prompt(q, k, v, kmean, slopes, *, batch, seq):
    n, att_w = q.shape
    nblk = seq // MOBA_BLOCK
    B = MOBA_BLOCK
    kern = functools.partial(_moba_kernel, nblk=nblk)
    return pl.pallas_call(
        kern,
        grid_spec=pltpu.PrefetchScalarGridSpec(
            num_scalar_prefetch=1,
            grid=(batch, att_w // LANE),
            in_specs=[pl.BlockSpec((seq, LANE), lambda b, p, s: (b, p)),
                      pl.BlockSpec((seq, LANE), lambda b, p, s: (b, p)),
                      pl.BlockSpec((seq, LANE), lambda b, p, s: (b, p)),
                      pl.BlockSpec((nblk, LANE), lambda b, p, s: (b, p))],
            out_specs=pl.BlockSpec((seq, LANE), lambda b, p, s: (b, p)),
            scratch_shapes=[pltpu.VMEM((nblk, B, LANE), BF16), pltpu.VMEM((nblk, LANE, B), BF16),
                            pltpu.VMEM((nblk * 2 * B, LANE), BF16), pltpu.VMEM((1, nblk * 2 * B), F32),
                            pltpu.VMEM((1, nblk * 2 * B), F32), pltpu.VMEM((LANE, nblk * 2 * B), F32)]),
        out_shape=jax.ShapeDtypeStruct((n, att_w), F32),
        compiler_params=_params("arbitrary", "arbitrary"),
        name="moba_prompt",
    )(slopes, q, k, v, kmean)


def _memkv_kernel(x_ref, w_ref, mk_ref, mv_ref, mkt_ref, mvt_ref, *, mem_w):
    r = _dot(x_ref[...].astype(BF16), w_ref[...])
    n_mem = r.shape[0]
    mk_ref[...] = r[:, :mem_w]
    mv_ref[...] = r[:, mem_w:]
    mkt_ref[0] = r[:, :mem_w].T.reshape(mem_w // HEAD_DIM, HEAD_DIM, n_mem)
    mvt_ref[0] = r[:, mem_w:].T.reshape(mem_w // HEAD_DIM, HEAD_DIM, n_mem)


def _memkv(mem2d, w_bf, *, n_mem, mem_w):
    n, d = mem2d.shape
    heads_t = pl.BlockSpec((1, mem_w // HEAD_DIM, HEAD_DIM, n_mem), lambda i: (i, 0, 0, 0))
    heads_t_shape = jax.ShapeDtypeStruct((n // n_mem, mem_w // HEAD_DIM, HEAD_DIM, n_mem), F32)
    return pl.pallas_call(
        functools.partial(_memkv_kernel, mem_w=mem_w),
        grid=(n // n_mem,),
        in_specs=[pl.BlockSpec((n_mem, d), lambda i: (i, 0)), pl.BlockSpec(w_bf.shape, lambda i: (0, 0))],
        out_specs=[pl.BlockSpec((n_mem, mem_w), lambda i: (i, 0))] * 2 + [heads_t] * 2,
        out_shape=[jax.ShapeDtypeStruct((n, mem_w), F32)] * 2 + [heads_t_shape] * 2,
        compiler_params=_params("arbitrary"),
        name="memkv",
    )(mem2d, w_bf)


def _memattn_kernel(qm_ref, mk_ref, mv_ref, o_ref, *, tq, mem_w):
    sub = lax.broadcasted_iota(I32, (LANE, tq), 0)
    for pr in range(mem_w // LANE):
        cs = slice(LANE * pr, LANE * (pr + 1))
        qp = qm_ref[:, cs]
        mkp = mk_ref[:, cs].astype(BF16)
        mvT = mv_ref[:, cs].T.astype(BF16)
        outs = []
        for hh in range(2):
            qh = jnp.where(_head_mask(qp.shape, hh), qp, jnp.zeros_like(qp))
            s = _dot_nt(mkp, qh)
            m = jnp.max(s, axis=0, keepdims=True)
            e = jnp.exp(s - m)
            l = jnp.sum(e, axis=0, keepdims=True)
            outs.append(_dot(mvT, e.astype(BF16)) / l)
        o_ref[:, cs] = jnp.where(sub < HEAD_DIM, outs[0], outs[1]).T


def _memattn_prompt(qm, mk, mv, *, batch, seq, n_mem, tq):
    n, mem_w = qm.shape
    nq = seq // tq
    return pl.pallas_call(
        functools.partial(_memattn_kernel, tq=tq, mem_w=mem_w),
        grid=(batch, nq),
        in_specs=[pl.BlockSpec((tq, mem_w), lambda b, i: (b * nq + i, 0)),
                  pl.BlockSpec((n_mem, mem_w), lambda b, i: (b, 0)),
                  pl.BlockSpec((n_mem, mem_w), lambda b, i: (b, 0))],
        out_specs=pl.BlockSpec((tq, mem_w), lambda b, i: (b * nq + i, 0)),
        out_shape=jax.ShapeDtypeStruct((n, mem_w), F32),
        compiler_params=_params("arbitrary", "arbitrary"),
        name="memattn_prompt",
    )(qm, mk, mv)


SORT_TILE = 256


def _sort_rows(ts, n_exp):
    return TOP_K * ts + SUBLANE * n_exp


def _finish1_kernel(x_ref, oa_ref, oc_ref, om_ref, gmix_ref, wout_ref, g1_ref, b1_ref, wrT_ref, br_ref,
                    x1_ref, xs_ref, route_ref, gw_ref, cpad_ref, loff_ref, cbase_ref, tot_ref, carry,
                    *, tile, ts, n_exp, alpha):
    i = pl.program_id(0)
    lt = _sort_rows(ts, n_exp)

    @pl.when(i == 0)
    def _():
        carry[...] = jnp.zeros_like(carry)

    mix = jnp.concatenate([_rms(oa_ref[...]), _rms(oc_ref[...]), _rms(om_ref[...])], axis=-1) * gmix_ref[...]
    if wout_ref.dtype == BF16:
        z = alpha * x_ref[...] + _dot(mix.astype(BF16), wout_ref[...])
    else:
        z = alpha * x_ref[...] + _dot(mix, wout_ref[...], precision=HI)
    x1 = _layernorm(z, g1_ref[...], b1_ref[...])
    x1_ref[...] = x1
    x1b = x1.astype(BF16)

    g = _dot_nt(wrT_ref[...], x1, precision=HI) + br_ref[...]
    eidx = lax.broadcasted_iota(I32, (n_exp, tile), 0)
    picks, vals = [], []
    for k in range(TOP_K):
        m = jnp.max(g, axis=0, keepdims=True)
        idx = jnp.min(jnp.where(g == m, eidx, n_exp), axis=0, keepdims=True)
        pick = eidx == idx
        route_ref[k:k + 1, :] = idx
        picks.append(pick)
        vals.append(m)
        g = jnp.where(pick, -jnp.inf, g)
    ex = [jnp.exp(v - vals[0]) for v in vals]
    denom = ex[0] + ex[1] + ex[2] + ex[3]
    for k in range(TOP_K):
        gw_ref[k:k + 1, :] = ex[k] / denom
    gw_ref[TOP_K:, :] = jnp.zeros((gw_ref.shape[0] - TOP_K, tile), F32)

    t_src = lax.broadcasted_iota(I32, (ts, ts), 0)
    t_dst = lax.broadcasted_iota(I32, (ts, ts), 1)
    before = jnp.where(t_src < t_dst, 1.0, 0.0).astype(BF16)
    e_src = lax.broadcasted_iota(I32, (n_exp, n_exp), 1)
    e_dst = lax.broadcasted_iota(I32, (n_exp, n_exp), 0)
    lower = jnp.where(e_src < e_dst, 1.0, 0.0).astype(BF16)
    slot = lax.broadcasted_iota(I32, (lt, ts), 0)
    for sub in range(tile // ts):
        cs = slice(sub * ts, (sub + 1) * ts)
        pk = [p[:, cs] for p in picks]
        onehot = jnp.zeros((n_exp, ts), F32)
        for p in pk:
            onehot = onehot + jnp.where(p, 1.0, 0.0)
        cnt = jnp.sum(onehot, axis=1, keepdims=True)
        cpad = jnp.floor((cnt + (SUBLANE - 1)) * (1.0 / SUBLANE)) * SUBLANE
        cpad_l = jnp.broadcast_to(cpad, (n_exp, LANE))
        loff_l = _dot(lower, cpad_l.astype(BF16))
        pos = _dot(onehot.astype(BF16), before) + loff_l[:, 0:1]
        perm = jnp.zeros((lt, ts), F32)
        for k in range(TOP_K):
            lpos = jnp.sum(jnp.where(pk[k], pos, 0.0), axis=0, keepdims=True).astype(I32)
            route_ref[TOP_K + k:TOP_K + k + 1, cs] = lpos
            perm = perm + jnp.where(slot == lpos, 1.0, 0.0)
        perm = perm.astype(BF16)
        xs_ref[sub * lt:(sub + 1) * lt, :] = _dot(perm, x1b[cs, :])
        cpad_ref[sub] = cpad_l
        loff_ref[sub] = loff_l
        cbase_ref[sub] = jnp.broadcast_to(carry[...], (n_exp, LANE))
        carry[...] = carry[...] + cpad
    tot_ref[...] = jnp.broadcast_to(carry[...], tot_ref.shape)


def _finish1(x2d, oa, oc, om, g_mix, wout_bf, ln_g, ln_b, w_rT, b_r, *, tile, ts, alpha):
    n, d = x2d.shape
    n_exp = w_rT.shape[0]
    lt = _sort_rows(ts, n_exp)
    spt = tile // ts
    kern = functools.partial(_finish1_kernel, tile=tile, ts=ts, n_exp=n_exp, alpha=alpha)
    row = lambda w: pl.BlockSpec((tile, w), lambda i: (i, 0))
    full = lambda a: pl.BlockSpec(a.shape, lambda i: (0,) * a.ndim)
    meta = pl.BlockSpec((spt, n_exp, LANE), lambda i: (i, 0, 0))
    meta_shape = jax.ShapeDtypeStruct((n // ts, n_exp, LANE), F32)
    return pl.pallas_call(
        kern,
        grid=(n // tile,),
        in_specs=[row(d), row(oa.shape[1]), row(oc.shape[1]), row(om.shape[1]), full(g_mix), full(wout_bf),
                  full(ln_g), full(ln_b), full(w_rT), full(b_r)],
        out_specs=[row(d), pl.BlockSpec((spt * lt, d), lambda i: (i, 0)),
                   pl.BlockSpec((2 * TOP_K, tile), lambda i: (0, i)), pl.BlockSpec((2 * TOP_K, tile), lambda i: (0, i)),
                   meta, meta, meta, pl.BlockSpec((n_exp, LANE), lambda i: (0, 0))],
        out_shape=[jax.ShapeDtypeStruct((n, d), F32), jax.ShapeDtypeStruct((n // ts * lt, d), F32),
                   jax.ShapeDtypeStruct((2 * TOP_K, n), I32), jax.ShapeDtypeStruct((2 * TOP_K, n), F32),
                   meta_shape, meta_shape, meta_shape, jax.ShapeDtypeStruct((n_exp, LANE), F32)],
        scratch_shapes=[pltpu.VMEM((n_exp, 1), F32)],
        compiler_params=_params("arbitrary"),
        name="finish1",
    )(x2d, oa, oc, om, g_mix, wout_bf, ln_g, ln_b, w_rT, b_r)


def _run_sizes(ts):
    sizes, s = [], SUBLANE
    while s <= max(ts, SUBLANE):
        sizes.append(s)
        s *= 2
    return sizes


def _for_each_piece(length, sizes, fn):
    off = 0
    for sz in sizes:
        @pl.when((length & sz) != 0)
        def _(off=off, sz=sz):
            fn(off, sz)
        off = off + (length & sz)


def _dispatch_kernel(cpad_ref, loff_ref, base_ref, fill_ref, xsa_ref, xsb_ref, zero_ref, buf_ref, sem, fill_sem,
                     *, n_a, n_exp, sizes, tm):
    t = pl.program_id(0)

    def copy_runs(xs_ref, wait):
        def run(e, c):
            n = cpad_ref[t * n_exp + e]
            src = loff_ref[t * n_exp + e]
            dst = base_ref[t * n_exp + e]

            def piece(off, sz):
                cp = pltpu.make_async_copy(xs_ref.at[pl.ds(pl.multiple_of(src + off, SUBLANE), sz), :],
                                           buf_ref.at[pl.ds(pl.multiple_of(dst + off, SUBLANE), sz), :], sem)
                cp.wait() if wait else cp.start()

            _for_each_piece(n, sizes, piece)
            return c

        lax.fori_loop(0, n_exp, run, 0)

    def all_runs(wait):
        @pl.when(t < n_a)
        def _():
            copy_runs(xsa_ref, wait)

        @pl.when(t >= n_a)
        def _():
            copy_runs(xsb_ref, wait)

    all_runs(False)

    @pl.when(t == pl.num_programs(0) - 1)
    def _():
        start = fill_ref[0]
        rest = buf_ref.shape[0] - start
        n_full = rest // tm

        def chunk(c, wait):
            cp = pltpu.make_async_copy(zero_ref, buf_ref.at[pl.ds(pl.multiple_of(start + c * tm, SUBLANE), tm), :],
                                       fill_sem)
            cp.wait() if wait else cp.start()

        def piece(wait):
            def fn(off, sz):
                cp = pltpu.make_async_copy(
                    zero_ref.at[pl.ds(0, sz), :],
                    buf_ref.at[pl.ds(pl.multiple_of(start + n_full * tm + off, SUBLANE), sz), :], fill_sem)
                cp.wait() if wait else cp.start()
            return fn

        tail_sizes = [s for s in _run_sizes(tm) if s < tm]
        for wait in (False, True):
            lax.fori_loop(0, n_full, lambda c, carry, wait=wait: (chunk(c, wait), carry)[1], 0)
            _for_each_piece(rest - n_full * tm, tail_sizes, piece(wait))

    all_runs(True)


def _dispatch(cpad, loff, base, fill, xs_a, xs_b, *, ts_a, ts_b, n_exp, m_pad, tm):
    w = xs_a.shape[1]
    lt_a, lt_b = _sort_rows(ts_a, n_exp), _sort_rows(ts_b, n_exp)
    n_a, n_b = xs_a.shape[0] // lt_a, xs_b.shape[0] // lt_b
    zero = jnp.zeros((tm, w), F32)
    return pl.pallas_call(
        functools.partial(_dispatch_kernel, n_a=n_a, n_exp=n_exp, sizes=_run_sizes(max(ts_a, ts_b)), tm=tm),
        grid_spec=pltpu.PrefetchScalarGridSpec(
            num_scalar_prefetch=4,
            grid=(n_a + n_b,),
            in_specs=[pl.BlockSpec((lt_a, w), lambda t, *_: (jnp.minimum(t, n_a - 1), 0)),
                      pl.BlockSpec((lt_b, w), lambda t, *_: (jnp.maximum(t - n_a, 0), 0)),
                      pl.BlockSpec((tm, w), lambda t, *_: (0, 0))],
            out_specs=pl.BlockSpec(memory_space=pl.ANY),
            scratch_shapes=[pltpu.SemaphoreType.DMA(()), pltpu.SemaphoreType.DMA(())]),
        out_shape=jax.ShapeDtypeStruct((m_pad, w), F32),
        compiler_params=_params("arbitrary"),
        name="dispatch",
    )(cpad, loff, base, fill, xs_a, xs_b, zero)


def _ffn_kernel(otile_ref, tile_ref, exp_ref, lo_ref, hi_ref, lhs_ref, wgu_ref, bgu_ref, wd_ref, bd_ref, out_ref,
                wgu_bf, wd_bf, *, tm, d_ff):
    g = pl.program_id(0)
    lo = lo_ref[g]
    hi = hi_ref[g]

    @pl.when(hi == lo)
    def _():
        out_ref[...] = jnp.zeros_like(out_ref)

    new_expert = jnp.logical_or(g == 0, exp_ref[g] != exp_ref[jnp.maximum(g - 1, 0)])

    @pl.when(jnp.logical_and(new_expert, hi > lo))
    def _():
        wgu_bf[...] = wgu_ref[0].astype(BF16)
        wd_bf[...] = wd_ref[0].astype(BF16)

    @pl.when(hi > lo)
    def _():
        gu = _dot(lhs_ref[...].astype(BF16), wgu_bf[...]) + bgu_ref[0]
        gate = jnp.minimum(gu[:, :d_ff], SWIGLU_LIMIT)
        up = jnp.clip(gu[:, d_ff:], -SWIGLU_LIMIT, SWIGLU_LIMIT)
        hid = (up + 1.0) * gate * jax.nn.sigmoid(SWIGLU_ALPHA * gate)
        o = _dot(hid.astype(BF16), wd_bf[...]) + bd_ref[0]

        @pl.when(lo == 0)
        def _():
            out_ref[...] = o

        @pl.when(lo > 0)
        def _():
            rows = lax.broadcasted_iota(I32, (tm, 1), 0)
            out_ref[...] = jnp.where((rows >= lo) & (rows < hi), o, out_ref[...])


def _ffn(sched, buf, w_gu, b_gu, w_d, b_d, *, tm):
    m, w = buf.shape
    n_exp, d, d_ff2 = w_gu.shape
    out_tiles, tiles, experts, los, his = sched
    return pl.pallas_call(
        functools.partial(_ffn_kernel, tm=tm, d_ff=d_ff2 // 2),
        grid_spec=pltpu.PrefetchScalarGridSpec(
            num_scalar_prefetch=5,
            grid=(tiles.shape[0],),
            in_specs=[pl.BlockSpec((tm, w), lambda g, ot, t, e, lo, hi: (t[g], 0)),
                      pl.BlockSpec((1, d, d_ff2), lambda g, ot, t, e, lo, hi: (e[g], 0, 0)),
                      pl.BlockSpec((1, 1, d_ff2), lambda g, ot, t, e, lo, hi: (e[g], 0, 0)),
                      pl.BlockSpec((1, d_ff2 // 2, d), lambda g, ot, t, e, lo, hi: (e[g], 0, 0)),
                      pl.BlockSpec((1, 1, d), lambda g, ot, t, e, lo, hi: (e[g], 0, 0))],
            out_specs=pl.BlockSpec((tm, d), lambda g, ot, t, e, lo, hi: (ot[g], 0)),
            scratch_shapes=[pltpu.VMEM((d, d_ff2), BF16), pltpu.VMEM((d_ff2 // 2, d), BF16)]),
        out_shape=jax.ShapeDtypeStruct((m, d), F32),
        compiler_params=_params("arbitrary"),
        name="expert_ffn",
    )(out_tiles, tiles, experts, los, his, buf, w_gu, b_gu, w_d, b_d)


def _moe_schedule(counts, n_rows, tm):
    n_exp = counts.shape[0]
    n_tiles = n_rows // tm
    n_items = n_tiles + n_exp - 1
    ends = jnp.cumsum(counts)
    starts = ends - counts
    first_tile = starts // tm
    n_it = jnp.where(counts > 0, (ends - 1) // tm - first_tile + 1, 0)
    it_end = jnp.cumsum(n_it)
    it_start = it_end - n_it
    item = jnp.arange(n_items, dtype=I32)
    n_real = it_end[-1]
    g = jnp.minimum(item, n_real - 1)
    e = jnp.minimum(jnp.sum((it_end[None, :] <= g[:, None]).astype(I32), axis=1), n_exp - 1)
    of_e = e[:, None] == jnp.arange(n_exp, dtype=I32)[None, :]
    at_e = lambda table: jnp.sum(jnp.where(of_e, table[None, :], 0), axis=1)
    tile = (at_e(first_tile) + g - at_e(it_start)).astype(I32)
    valid = item < n_real
    lo = jnp.where(valid, jnp.clip(at_e(starts) - tile * tm, 0, tm), 0).astype(I32)
    hi = jnp.where(valid, jnp.clip(at_e(ends) - tile * tm, 0, tm), 0).astype(I32)
    used = (ends[-1] + tm - 1) // tm
    out_tile = jnp.where(valid, tile, jnp.minimum(used + item - n_real, n_tiles - 1)).astype(I32)
    return (out_tile, tile, e, lo, hi), starts.astype(I32)


def _combine_kernel(cpad_ref, loff_ref, base_ref, route_ref, gw_ref, x1_ref, g2_ref, b2_ref, eo_ref, y_ref,
                    gbuf, sem, *, ts, n_exp, sizes, alpha):
    t = pl.program_id(0)
    nt = pl.num_programs(0)
    lt = gbuf.shape[1]
    d = gbuf.shape[2]

    def tile_copies(tt, slot, wait):
        def run(e, c):
            n = cpad_ref[tt * n_exp + e]
            src = base_ref[tt * n_exp + e]
            dst = loff_ref[tt * n_exp + e]

            def piece(off, sz):
                cp = pltpu.make_async_copy(eo_ref.at[pl.ds(pl.multiple_of(src + off, SUBLANE), sz), :],
                                           gbuf.at[slot, pl.ds(pl.multiple_of(dst + off, SUBLANE), sz), :],
                                           sem.at[slot])
                cp.wait() if wait else cp.start()

            _for_each_piece(n, sizes, piece)
            return c

        lax.fori_loop(0, n_exp, run, 0, unroll=4)

    def fetch(tt, slot):
        gbuf[slot, TOP_K * ts:, :] = jnp.zeros((lt - TOP_K * ts, d), F32)
        tile_copies(tt, slot, False)

    slot = t % 2

    @pl.when(t == 0)
    def _():
        fetch(t, slot)

    @pl.when(t + 1 < nt)
    def _():
        fetch(t + 1, 1 - slot)

    tile_copies(t, slot, True)

    r = lax.broadcasted_iota(I32, (ts, ts), 0)
    c = lax.broadcasted_iota(I32, (ts, ts), 1)
    eye = jnp.where(r == c, 1.0, 0.0)
    wcol = _dot_nt(eye, gw_ref[...], precision=HI)
    pcol = _dot_nt(eye, route_ref[...].astype(F32), precision=HI)
    rows = gbuf[slot].astype(BF16)
    slot_i = lax.broadcasted_iota(I32, (ts, lt), 1)
    unsort = jnp.zeros((ts, lt), F32)
    for k in range(TOP_K):
        lpos = pcol[:, TOP_K + k:TOP_K + k + 1].astype(I32)
        unsort = unsort + jnp.where(slot_i == lpos, wcol[:, k:k + 1], 0.0)
    hi = unsort.astype(BF16)
    lo = (unsort - hi.astype(F32)).astype(BF16)
    moe = _dot(hi, rows) + _dot(lo, rows)
    y_ref[...] = _layernorm(alpha * x1_ref[...] + moe, g2_ref[...], b2_ref[...])


def _combine(cpad, loff, base, route, gw, x1, ln_g, ln_b, eo, *, ts, n_exp, alpha):
    n, d = x1.shape
    lt = _sort_rows(ts, n_exp)
    full = lambda a: pl.BlockSpec(a.shape, lambda i, *_: (0,) * a.ndim)
    return pl.pallas_call(
        functools.partial(_combine_kernel, ts=ts, n_exp=n_exp, sizes=_run_sizes(ts), alpha=alpha),
        grid_spec=pltpu.PrefetchScalarGridSpec(
            num_scalar_prefetch=3,
            grid=(n // ts,),
            in_specs=[pl.BlockSpec((2 * TOP_K, ts), lambda i, *_: (0, i)),
                      pl.BlockSpec((2 * TOP_K, ts), lambda i, *_: (0, i)),
                      pl.BlockSpec((ts, d), lambda i, *_: (i, 0)), full(ln_g), full(ln_b),
                      pl.BlockSpec(memory_space=pl.ANY)],
            out_specs=pl.BlockSpec((ts, d), lambda i, *_: (i, 0)),
            scratch_shapes=[pltpu.VMEM((2, lt, d), F32), pltpu.SemaphoreType.DMA((2,))]),
        out_shape=jax.ShapeDtypeStruct((n, d), F32),
        compiler_params=_params("arbitrary"),
        name="combine",
    )(cpad, loff, base, route, gw, x1, ln_g, ln_b, eo)


def _finish(groups, wts, *, tile, tm, alpha):
    n_exp = wts["w_rT"].shape[0]
    flat = lambda a: a[:, :, 0].astype(I32).reshape(-1)
    routed = []
    for x2d, oa, oc, om, w_out in groups:
        n = x2d.shape[0]
        ts = _pick_tile(n, SORT_TILE)
        x1, xs, route, gw, cpad, loff, cbase, tot = _finish1(
            x2d, oa, oc, om, wts["g_mix"], w_out, wts["ln1_g"], wts["ln1_b"], wts["w_rT"], wts["b_r"],
            tile=_pick_tile(n, tile), ts=ts, alpha=alpha)
        routed.append(dict(ts=ts, x1=x1, xs=xs, route=route, gw=gw, cpad=flat(cpad), loff=flat(loff),
                           cbase=cbase[:, :, 0].astype(I32), total=tot[:, 0].astype(I32)))
    a, b = routed
    counts = a["total"] + b["total"]
    m_pad = (-(-(a["xs"].shape[0] + b["xs"].shape[0]) // tm) + 1) * tm
    sched, offs = _moe_schedule(counts, m_pad, tm)
    a["base"] = (a["cbase"] + offs[None, :]).reshape(-1)
    b["base"] = (b["cbase"] + (offs + a["total"])[None, :]).reshape(-1)
    both = lambda key: jnp.concatenate([a[key], b[key]])
    fill = jnp.sum(counts, keepdims=True)
    buf = _dispatch(both("cpad"), both("loff"), both("base"), fill, a["xs"], b["xs"], ts_a=a["ts"], ts_b=b["ts"],
                    n_exp=n_exp, m_pad=m_pad, tm=tm)
    eo = _ffn(sched, buf, wts["w_gu"], wts["b_gu"], wts["w_d"], wts["b_d"], tm=tm)
    return [_combine(r["cpad"], r["loff"], r["base"], r["route"], r["gw"], r["x1"], wts["ln2_g"], wts["ln2_b"], eo,
                     ts=r["ts"], n_exp=n_exp, alpha=alpha) for r in routed]


def _sample_inproj_kernel(x_ref, w_ref, wc_ref, p0_ref, p1_ref, q_ref, k_ref, v_ref, qm_ref, oc_ref, u_ref,
                          *, att_w, mem_w, conv_w):
    x = x_ref[...]
    c1, c2, c3 = att_w, 2 * att_w, 3 * att_w
    c4 = c3 + mem_w
    c5, c6, c7 = c4 + conv_w, c4 + 2 * conv_w, c4 + 3 * conv_w

    def proj(lo, hi):
        return _dot(x, w_ref[:, lo:hi], precision=HI)

    q_ref[...] = proj(0, c1) * Q_SCALE
    k_ref[...] = proj(c1, c2)
    v_ref[...] = proj(c2, c3)
    qm_ref[...] = proj(c3, c4) * Q_SCALE
    u = proj(c5, c6) * proj(c6, c7)
    cz = wc_ref[0:1, :] * p0_ref[...] + wc_ref[1:2, :] * p1_ref[...] + wc_ref[2:3, :] * u
    oc_ref[...] = proj(c4, c5) * cz
    u_ref[...] = u


def _sample_inproj(x2d, w_bf, w_conv, prev0, prev1, *, att_w, mem_w, conv_w):
    n = x2d.shape[0]
    args = (x2d, w_bf, w_conv, prev0, prev1)
    full = lambda a: pl.BlockSpec(a.shape, lambda i: (0,) * a.ndim)
    widths = (att_w, att_w, att_w, mem_w, conv_w, conv_w)
    return pl.pallas_call(
        functools.partial(_sample_inproj_kernel, att_w=att_w, mem_w=mem_w, conv_w=conv_w),
        grid=(1,),
        in_specs=[full(a) for a in args],
        out_specs=[pl.BlockSpec((n, w), lambda i: (0, 0)) for w in widths],
        out_shape=[jax.ShapeDtypeStruct((n, w), F32) for w in widths],
        compiler_params=_params("arbitrary"),
        name="sample_inproj",
    )(*args)


def _kscan_kernel(pt_ref, q_ref, ck_ref, sel_ref, kbuf, ksum, sem, *, n_pages, chunk, n_heads):
    b = pl.program_id(0)
    n_chunks = n_pages // chunk
    pages_per_blk = MOBA_BLOCK // PAGE_SIZE
    nblk = ksum.shape[0]
    q_col = q_ref[0]

    def copies(bb, c, slot):
        return [pltpu.make_async_copy(ck_ref.at[0, pt_ref[bb * n_pages + c * chunk + j]], kbuf.at[slot, j],
                                      sem.at[slot]) for j in range(chunk)]

    @pl.when(b == 0)
    def _():
        for cp in copies(b, 0, 0):
            cp.start()

    def body(c, carry):
        slot = c % 2
        last = c + 1 == n_chunks

        @pl.when(jnp.logical_or(jnp.logical_not(last), b + 1 < pl.num_programs(0)))
        def _():
            for cp in copies(jnp.where(last, b + 1, b), jnp.where(last, 0, c + 1), 1 - slot):
                cp.start()

        for cp in copies(b, c, slot):
            cp.wait()
        for jb in range(chunk // pages_per_blk):
            s = kbuf[slot, pages_per_blk * jb]
            for pg in range(1, pages_per_blk):
                s = s + kbuf[slot, pages_per_blk * jb + pg]
            ksum[c * (chunk // pages_per_blk) + jb] = jnp.sum(s * q_col, axis=1)
        return carry

    lax.fori_loop(0, n_chunks, body, 0)

    g = jnp.sum(ksum[...], axis=2, keepdims=True) * (1.0 / MOBA_BLOCK)
    blk = lax.broadcasted_iota(I32, g.shape, 0)
    sel_ref[...] = jnp.zeros(sel_ref.shape, I32)
    for r in range(MOBA_TOPK):
        m = jnp.max(g, axis=0, keepdims=True)
        idx = jnp.min(jnp.where((g == m) & (g > -jnp.inf), blk, nblk), axis=0, keepdims=True)
        sel_ref[0, :, r:r + 1] = idx[0]
        g = jnp.where(blk == idx, -jnp.inf, g)


def _kscan(pt_flat, q_col, ckT, *, n_pages, chunk):
    bs, n_heads, hd, _ = q_col.shape
    nblk = n_pages * PAGE_SIZE // MOBA_BLOCK
    return pl.pallas_call(
        functools.partial(_kscan_kernel, n_pages=n_pages, chunk=chunk, n_heads=n_heads),
        grid_spec=pltpu.PrefetchScalarGridSpec(
            num_scalar_prefetch=1,
            grid=(bs,),
            in_specs=[pl.BlockSpec((1, n_heads, hd, 1), lambda b, pt: (b, 0, 0, 0)), pl.BlockSpec(memory_space=pl.ANY)],
            out_specs=pl.BlockSpec((1, n_heads, LANE), lambda b, pt: (b, 0, 0)),
            scratch_shapes=[pltpu.VMEM((2, chunk, n_heads, hd, PAGE_SIZE), F32),
                            pltpu.VMEM((nblk, n_heads, PAGE_SIZE), F32), pltpu.SemaphoreType.DMA((2,))]),
        out_shape=jax.ShapeDtypeStruct((bs, n_heads, LANE), I32),
        compiler_params=_params("arbitrary"),
        name="kscan",
    )(pt_flat, q_col, ckT)


def _sample_attn_kernel(pt_ref, sel_ref, slopes_ref, q_ref, kn_ref, vn_ref, qm_ref, mk_ref, mv_ref, ck_ref, cv_ref,
                        oa_ref, om_ref, kbuf, vbuf, sem, *, n_pages, n_heads, past_len, mem_w):
    b = pl.program_id(0)
    nb = pl.num_programs(0)
    B = MOBA_BLOCK
    pages_per_blk = B // PAGE_SIZE
    n_keys = MOBA_TOPK * B

    def blocks(bb, h):
        return [sel_ref[(bb * n_heads + h) * MOBA_TOPK + s] for s in range(MOBA_TOPK)]

    n_sel_pages = MOBA_TOPK * pages_per_blk

    def copies(bb, slot):
        cps = []
        for h in range(n_heads):
            blks = blocks(bb, h)
            for s in range(MOBA_TOPK):
                for half in range(pages_per_blk):
                    pg = pt_ref[bb * n_pages + pages_per_blk * blks[s] + half]
                    pp = s * pages_per_blk + half
                    cps.append(pltpu.make_async_copy(ck_ref.at[0, pg, h], kbuf.at[slot, h, pp], sem.at[0, slot]))
                    cps.append(pltpu.make_async_copy(cv_ref.at[0, pg, h], vbuf.at[slot, h, pp], sem.at[1, slot]))
        return cps

    slot = b % 2

    @pl.when(b == 0)
    def _():
        for cp in copies(b, slot):
            cp.start()

    @pl.when(b + 1 < nb)
    def _():
        for cp in copies(b + 1, 1 - slot):
            cp.start()

    for cp in copies(b, slot):
        cp.wait()

    key_lane = lax.broadcasted_iota(I32, (1, n_keys), 1)
    for h in range(n_heads):
        blks = blocks(b, h)
        qh = q_ref[0, h:h + 1, :]
        q8 = jnp.broadcast_to(qh, (SUBLANE, HEAD_DIM))
        blk_of_key = jnp.where(key_lane < B, blks[0], jnp.where(key_lane < 2 * B, blks[1], blks[2]))
        dist = (past_len - blk_of_key * B - (key_lane & (B - 1))).astype(F32)
        s = jnp.concatenate([_dot(q8.astype(BF16), kbuf[slot, h, pp].astype(BF16)) for pp in range(n_sel_pages)],
                            axis=1)
        s = s - slopes_ref[h] * dist
        s_self = jnp.sum(qh * kn_ref[0, h:h + 1, :], axis=1, keepdims=True)
        m = jnp.maximum(jnp.max(s, axis=1, keepdims=True), s_self)
        e = jnp.exp(s - m)
        e_self = jnp.exp(s_self - m)
        l = jnp.sum(e, axis=1, keepdims=True) + e_self
        o = e_self * vn_ref[0, h:h + 1, :]
        for pp in range(n_sel_pages):
            o = o + _dot_nt(e[:, pp * PAGE_SIZE:(pp + 1) * PAGE_SIZE].astype(BF16), vbuf[slot, h, pp].astype(BF16))
        oa_ref[0, h:h + 1, :] = (o / l)[0:1, :]

    for h in range(mem_w // HEAD_DIM):
        q8 = jnp.broadcast_to(qm_ref[0, h:h + 1, :], (SUBLANE, HEAD_DIM))
        s = _dot(q8, mk_ref[0, 0, h], precision=HI)
        e = jnp.exp(s - jnp.max(s, axis=1, keepdims=True))
        o = _dot_nt(e, mv_ref[0, 0, h], precision=HI) / jnp.sum(e, axis=1, keepdims=True)
        om_ref[0, h:h + 1, :] = o[0:1, :]


def _sample_attn(pt_flat, sel_flat, slopes, q3, kn3, vn3, qm3, mkT, mvT, ck, cv, *, n_pages):
    bs, n_heads, hd = q3.shape
    mem_heads = qm3.shape[1]
    mem_w = mem_heads * hd
    n_keys = MOBA_TOPK * MOBA_BLOCK
    per_b = lambda a: pl.BlockSpec((1,) + a.shape[1:], lambda b, pt, sel: (b,) + (0,) * (a.ndim - 1))
    mem_b = lambda a: pl.BlockSpec((1, 1) + a.shape[2:], lambda b, pt, sel: (0, b) + (0,) * (a.ndim - 2))
    return pl.pallas_call(
        functools.partial(_sample_attn_kernel, n_pages=n_pages, n_heads=n_heads, past_len=n_pages * PAGE_SIZE,
                          mem_w=mem_w),
        grid_spec=pltpu.PrefetchScalarGridSpec(
            num_scalar_prefetch=2,
            grid=(bs,),
            in_specs=[pl.BlockSpec(memory_space=pltpu.SMEM), per_b(q3), per_b(kn3), per_b(vn3), per_b(qm3),
                      mem_b(mkT), mem_b(mvT), pl.BlockSpec(memory_space=pl.ANY), pl.BlockSpec(memory_space=pl.ANY)],
            out_specs=[pl.BlockSpec((1, n_heads, hd), lambda b, pt, sel: (b, 0, 0)),
                       pl.BlockSpec((1, mem_heads, hd), lambda b, pt, sel: (b, 0, 0))],
            scratch_shapes=[pltpu.VMEM((2, n_heads, n_keys // PAGE_SIZE, hd, PAGE_SIZE), F32),
                            pltpu.VMEM((2, n_heads, n_keys // PAGE_SIZE, hd, PAGE_SIZE), F32),
                            pltpu.SemaphoreType.DMA((2, 2))]),
        out_shape=[jax.ShapeDtypeStruct((bs, n_heads, hd), F32), jax.ShapeDtypeStruct((bs, mem_heads, hd), F32)],
        compiler_params=_params("arbitrary"),
        name="sample_attn",
    )(pt_flat, sel_flat, slopes, q3, kn3, vn3, qm3, mkT, mvT, ck, cv)


def _sample_layer(x, cache_k, cache_v, mem_k, mem_v, state, page_table, wts, dims, alpha):
    bs, dec_seq, d = x.shape
    assert dec_seq == 1
    att_w, mem_w, conv_w = dims
    n_heads = att_w // HEAD_DIM
    n_pages = page_table.shape[1]
    assert (n_pages * PAGE_SIZE) % MOBA_BLOCK == 0 and n_pages * PAGE_SIZE // MOBA_BLOCK >= MOBA_TOPK
    x2d = x.reshape(bs, d)
    q, k, v, qm, oc, u = _sample_inproj(x2d, wts["w_in_f32"], wts["w_conv"], state[:, 0, :], state[:, 1, :],
                                        att_w=att_w, mem_w=mem_w, conv_w=conv_w)
    pt_flat = page_table.reshape(-1)
    heads = lambda a: a.reshape(bs, n_heads, HEAD_DIM)
    pagesT = lambda c: jnp.transpose(c, (0, 1, 3, 4, 2))
    chunk = 16 if n_pages % 32 == 0 else 8
    assert n_pages % (2 * chunk) == 0
    sel = _kscan(pt_flat, q.reshape(bs, n_heads, HEAD_DIM, 1), pagesT(cache_k), n_pages=n_pages, chunk=chunk)
    sel_flat = sel[:, :, :MOBA_TOPK].reshape(-1)
    oa, om = _sample_attn(pt_flat, sel_flat, wts["slopes"], heads(q), heads(k), heads(v),
                          qm.reshape(bs, mem_w // HEAD_DIM, HEAD_DIM), pagesT(mem_k), pagesT(mem_v), pagesT(cache_k),
                          pagesT(cache_v), n_pages=n_pages)
    conv_state = jnp.stack([state[:, 1, :], u], axis=1)
    return (x2d, oa.reshape(bs, att_w), oc, om.reshape(bs, mem_w), wts["w_out_f32"]), k, v, conv_state


def _pick_tile(n, pref):
    return pref if n % pref == 0 else n


def _prompt_layer(x, mem, wts, dims, alpha):
    batch, seq, d = x.shape
    att_w, mem_w, conv_w = dims
    n = batch * seq
    n_mem = mem.shape[1]
    x2d = x.reshape(n, d)
    tile = 512
    q, k, v, kT, vT, kmean, qm, oc, tail = _inproj(x2d, wts["w_in"], wts["w_conv"], seq=seq, tile=tile, att_w=att_w,
                                                   mem_w=mem_w, conv_w=conv_w)
    oa = _moba_prompt(q, k, v, kmean.reshape(n // MOBA_BLOCK, att_w), wts["slopes"], batch=batch, seq=seq)
    mk, mv, mkT, mvT = _memkv(mem.reshape(batch * n_mem, d), wts["w_mem_kv"], n_mem=n_mem, mem_w=mem_w)
    om = _memattn_prompt(qm, mk, mv, batch=batch, seq=seq, n_mem=n_mem, tq=512)
    conv_state = tail.reshape(batch, seq // tile, SUBLANE, conv_w)[:, -1, SUBLANE - (CONV_K - 1):, :]
    rows = lambda t: jnp.transpose(t, (0, 3, 1, 2))
    return (x2d, oa, oc, om, wts["w_out"]), rows(kT), rows(vT), conv_state, rows(mkT), rows(mvT)


def _prep_weights(l, w_in, w_mem_kv, w_conv, g_mix, w_out, ln1_g, ln1_b, w_router, b_router, w_gate_up, b_gate_up,
                  w_down, b_down, ln2_g, ln2_b):
    n_heads = 8
    row = lambda a: a[l][None, :]
    return {
        "w_in": w_in[l].astype(BF16), "w_in_f32": w_in[l], "w_mem_kv": w_mem_kv[l].astype(BF16), "w_conv": w_conv[l],
        "g_mix": row(g_mix), "w_out": w_out[l].astype(BF16), "w_out_f32": w_out[l], "ln1_g": row(ln1_g),
        "ln1_b": row(ln1_b),
        "w_rT": w_router[l].T, "b_r": b_router[l][:, None],
        "w_gu": w_gate_up[l], "b_gu": b_gate_up[l][:, None, :],
        "w_d": w_down[l], "b_d": b_down[l][:, None, :],
        "ln2_g": row(ln2_g), "ln2_b": row(ln2_b),
        "slopes": 2.0 ** (-8.0 * jnp.arange(1, n_heads + 1, dtype=F32) / n_heads),
    }


def kernel(x_prompt, x_sample, cache_k, cache_v, cache_mem_k, cache_mem_v, state_conv, page_table, mem_prompt, w_in, w_mem_kv, w_conv, g_mix, w_out, ln1_g, ln1_b, w_router, b_router, w_gate_up, b_gate_up, w_down, b_down, ln2_g, ln2_b):
    depth = w_in.shape[0]
    assert depth == 1
    alpha = (2 * depth) ** 0.25
    n_heads = cache_k.shape[3]
    att_w = n_heads * HEAD_DIM
    mem_w = cache_mem_k.shape[3] * HEAD_DIM
    conv_w = state_conv.shape[3]
    dims = (att_w, mem_w, conv_w)
    wts = _prep_weights(0, w_in, w_mem_kv, w_conv, g_mix, w_out, ln1_g, ln1_b, w_router, b_router, w_gate_up,
                        b_gate_up, w_down, b_down, ln2_g, ln2_b)
    bp, seq, d = x_prompt.shape
    n_mem = mem_prompt.shape[1]
    mixed_p, k_p, v_p, conv_p, mk_p, mv_p = _prompt_layer(x_prompt, mem_prompt, wts, dims, alpha)
    bs = x_sample.shape[0]
    mixed_s, k_s, v_s, conv_s = _sample_layer(x_sample, cache_k, cache_v, cache_mem_k, cache_mem_v,
                                              state_conv[0], page_table, wts, dims, alpha)
    y_p, y_s = _finish((mixed_p, mixed_s), wts, tile=512, tm=512, alpha=alpha)
    return (y_p.reshape(x_prompt.shape), y_s.reshape(x_sample.shape),
            k_p[None], v_p[None], conv_p[None], mk_p[None], mv_p[None],
            k_s.reshape(1, bs, 1, n_heads, HEAD_DIM), v_s.reshape(1, bs, 1, n_heads, HEAD_DIM), conv_s[None])
```
